```python
import jax, jax.numpy as jnp
from jax import lax
import numpy as np

D_MODEL = 1024
BATCH = 4
SEQ = 4096
DEPTH = 2

N_MIXERS = 2
HEAD_SIZE = 64
N_HEADS = D_MODEL // HEAD_SIZE
DECAY_LORA = 64
ICLR_LORA = 64
GATE_LORA = 160
GN_EPS = 64e-5
LRU_WIDTH = D_MODEL
LRU_BLOCK = 256
LRU_BLOCKS = LRU_WIDTH // LRU_BLOCK
CONV_WIDTH = 4
LRU_C = 8.0
D_FF = 7 * D_MODEL // 2
N_EXPERTS = 8
TOP_K = 2
NORM_EPS = 1e-6
N_RWKV = (DEPTH + 1) // 2
N_LRU = DEPTH // 2
N_DENSE = (DEPTH + 1) // 2
N_MOE = DEPTH // 2

kernel_name = 'hybrid_rwkv7_rglru_moe_adaln'


def rmsnorm(x, g):
    xf = x.astype(jnp.float32)
    y = xf * lax.rsqrt(jnp.mean(xf * xf, axis=-1, keepdims=True) + NORM_EPS)
    return (y * g.astype(jnp.float32)).astype(x.dtype)


def modulate(x, g, shift, scale):
    return rmsnorm(x, g) * (1.0 + scale[:, None, :]) + shift[:, None, :]


def causal_shift(x):
    return jnp.pad(x, ((0, 0), (1, 0), (0, 0)))[:, :-1]


def wkv7_scan(r, w, k, v, a_in, b_in):
    B, T, H, N = r.shape
    xs = tuple(jnp.moveaxis(t, 1, 0) for t in (r, w, k, v, a_in, b_in))

    def step(S, inp):
        r_t, w_t, k_t, v_t, a_t, b_t = inp
        sa = jnp.einsum('bhvk,bhk->bhv', S, a_t)
        S = S * w_t[:, :, None, :] + sa[..., None] * b_t[:, :, None, :] + v_t[..., None] * k_t[:, :, None, :]
        return S, jnp.einsum('bhvk,bhk->bhv', S, r_t)

    S0 = jnp.zeros((B, H, N, N), jnp.float32)
    _, ys = lax.scan(step, S0, xs)
    return jnp.moveaxis(ys, 0, 1)


def rwkv7_time_mix(h, mu, w_rkv, w_o, w0, w1, w2, a0, a1, a2, g1, g2, k_k, k_a, r_k, gn_w, gn_b):
    B, T, C = h.shape
    f32 = jnp.float32
    xx = causal_shift(h) - h
    xm = h[None] + xx[None] * mu[:, None, None, :]
    r, k, v = jnp.einsum('pbtc,pcd->pbtd', xm[:3], w_rkv)
    xw, xa, xg = xm[3], xm[4], xm[5]
    w_log = -jax.nn.softplus(-(w0 + jnp.tanh(xw @ w1) @ w2).astype(f32)) - 0.5
    decay = jnp.exp(-jnp.exp(w_log))
    a = jax.nn.sigmoid((a0 + (xa @ a1) @ a2).astype(f32))
    g = jax.nn.sigmoid(xg @ g1) @ g2

    def heads(t):
        return t.astype(f32).reshape(B, T, N_HEADS, HEAD_SIZE)

    r_h, k_h, v_h, a_h, w_h = heads(r), heads(k), heads(v), heads(a), heads(decay)
    kk = k_h * k_k.reshape(N_HEADS, HEAD_SIZE)
    kk = kk / jnp.maximum(jnp.linalg.norm(kk, axis=-1, keepdims=True), 1e-12)
    k_h = k_h * (1.0 + (a_h - 1.0) * k_a.reshape(N_HEADS, HEAD_SIZE))
    y = wkv7_scan(r_h, w_h, k_h, v_h, -kk, kk * a_h)
    mean = jnp.mean(y, axis=-1, keepdims=True)
    var = jnp.mean(jnp.square(y - mean), axis=-1, keepdims=True)
    y = ((y - mean) * lax.rsqrt(var + GN_EPS)).reshape(B, T, C) * gn_w + gn_b
    bonus = jnp.sum(r_h * k_h * r_k, axis=-1, keepdims=True) * v_h
    y = (y + bonus.reshape(B, T, C)) * g
    return y.astype(h.dtype) @ w_o


def rglru_block(h, w_in, conv_w, conv_b, w_gates, b_gates, lam, w_out):
    B, T, _ = h.shape
    f32 = jnp.float32
    xb, gb = jnp.split(h @ w_in, 2, axis=-1)
    gate = jax.nn.gelu(gb)
    xb = lax.conv_general_dilated(xb, conv_w[:, None, :], window_strides=(1,),
                                  padding=[(CONV_WIDTH - 1, 0)],
                                  dimension_numbers=('NWC', 'WIO', 'NWC'),
                                  feature_group_count=LRU_WIDTH) + conv_b
    xh = xb.reshape(B, T, LRU_BLOCKS, LRU_BLOCK)
    gates = jnp.einsum('btnh,nhg->btng', xh, w_gates) + b_gates
    r_t, i_t = jnp.split(jax.nn.sigmoid(gates.astype(f32)), 2, axis=-1)
    r_t = r_t.reshape(B, T, LRU_WIDTH)
    i_t = i_t.reshape(B, T, LRU_WIDTH)
    log_a = -LRU_C * r_t * jax.nn.softplus(-lam.astype(f32))
    a_t = jnp.exp(log_a)
    b_t = jnp.sqrt(-jnp.expm1(2.0 * log_a)) * (i_t * xb.astype(f32))

    def combine(c1, c2):
        return c1[0] * c2[0], c2[0] * c1[1] + c2[1]

    _, hs = lax.associative_scan(combine, (a_t, b_t), axis=1)
    return (hs.astype(h.dtype) * gate) @ w_out


def swiglu(h, w_gu, w_d):
    g, u = jnp.split(h @ w_gu, 2, axis=-1)
    return (jax.nn.silu(g) * u) @ w_d


def moe_swiglu(h, w_router, b_router, w_gu, w_d):
    B, T, D = h.shape
    t = h.reshape(B * T, D)
    logits = (t @ w_router + b_router).astype(jnp.float32)
    top_v, top_i = lax.top_k(logits, TOP_K)
    probs = jax.nn.softmax(top_v, axis=-1)
    comb = jnp.einsum('nk,nke->ne', probs, jax.nn.one_hot(top_i, N_EXPERTS, dtype=jnp.float32))
    out = jnp.zeros_like(t)
    for e in range(N_EXPERTS):
        out = out + comb[:, e, None].astype(t.dtype) * swiglu(t, w_gu[e], w_d[e])
    return out.reshape(B, T, D)


def setup_inputs(seed: int = 0) -> dict:
    key = jax.random.key(seed)
    counter = [0]

    def nxt():
        counter[0] += 1
        return jax.random.fold_in(key, counter[0])

    def nrm(shape, scale):
        return jax.random.normal(nxt(), shape, jnp.float32) * scale

    def unif(shape, lo, hi):
        return jax.random.uniform(nxt(), shape, jnp.float32, lo, hi)

    D, F, E, W = D_MODEL, D_FF, N_EXPERTS, LRU_WIDTH
    NR, NL, ND, NM = N_RWKV, N_LRU, N_DENSE, N_MOE
    u = unif((NL, W), 0.9, 0.999)
    a_base = u ** (1.0 / LRU_C)
    lam = jnp.log(a_base) - jnp.log1p(-a_base)
    return {
        'x': nrm((BATCH, SEQ, D), 1.0),
        'c': nrm((BATCH, D), 1.0),
        'ada_w': nrm((DEPTH, D, 6 * D), 0.1 * D ** -0.5),
        'ada_b': nrm((DEPTH, 6 * D), 0.01),
        'norm_g': 1.0 + nrm((DEPTH, 2, D), 0.05),
        'final_g': 1.0 + nrm((D,), 0.05),
        'rwkv_mu': unif((NR, 6, D), 0.0, 1.0),
        'rwkv_w_rkv': nrm((NR, 3, D, D), D ** -0.5),
        'rwkv_w_o': nrm((NR, D, D), D ** -0.5),
        'rwkv_w0': unif((NR, D), -6.0, -1.0),
        'rwkv_w1': nrm((NR, D, DECAY_LORA), 0.1 * D ** -0.5),
        'rwkv_w2': nrm((NR, DECAY_LORA, D), 0.1 * DECAY_LORA ** -0.5),
        'rwkv_a0': nrm((NR, D), 0.1),
        'rwkv_a1': nrm((NR, D, ICLR_LORA), D ** -0.5),
        'rwkv_a2': nrm((NR, ICLR_LORA, D), 0.1 * ICLR_LORA ** -0.5),
        'rwkv_g1': nrm((NR, D, GATE_LORA), D ** -0.5),
        'rwkv_g2': nrm((NR, GATE_LORA, D), GATE_LORA ** -0.5),
        'rwkv_k_k': 0.85 + nrm((NR, D), 0.05),
        'rwkv_k_a': 1.0 + nrm((NR, D), 0.05),
        'rwkv_r_k': nrm((NR, N_HEADS, HEAD_SIZE), 0.1),
        'rwkv_gn_w': 1.0 + nrm((NR, D), 0.05),
        'rwkv_gn_b': nrm((NR, D), 0.01),
        'lru_w_in': nrm((NL, D, 2 * W), D ** -0.5),
        'lru_conv_w': nrm((NL, CONV_WIDTH, W), CONV_WIDTH ** -0.5),
        'lru_conv_b': nrm((NL, W), 0.01),
        'lru_w_gates': nrm((NL, LRU_BLOCKS, LRU_BLOCK, 2 * LRU_BLOCK), LRU_BLOCK ** -0.5),
        'lru_b_gates': nrm((NL, LRU_BLOCKS, 2 * LRU_BLOCK), 0.01),
        'lru_lam': lam,
        'lru_w_out': nrm((NL, W, D), W ** -0.5),
        'ffn_w_gu': nrm((ND, D, 2 * F), D ** -0.5),
        'ffn_w_d': nrm((ND, F, D), F ** -0.5),
        'moe_w_router': nrm((NM, D, E), D ** -0.5),
        'moe_b_router': nrm((NM, E), 0.01),
        'moe_w_gu': nrm((NM, E, D, 2 * F), D ** -0.5),
        'moe_w_d': nrm((NM, E, F, D), F ** -0.5),
    }


def reference(x, c, ada_w, ada_b, norm_g, final_g,
              rwkv_mu, rwkv_w_rkv, rwkv_w_o, rwkv_w0, rwkv_w1, rwkv_w2, rwkv_a0, rwkv_a1, rwkv_a2,
              rwkv_g1, rwkv_g2, rwkv_k_k, rwkv_k_a, rwkv_r_k, rwkv_gn_w, rwkv_gn_b,
              lru_w_in, lru_conv_w, lru_conv_b, lru_w_gates, lru_b_gates, lru_lam, lru_w_out,
              ffn_w_gu, ffn_w_d, moe_w_router, moe_b_router, moe_w_gu, moe_w_d):
    cond = jax.nn.silu(c)
    for i in range(DEPTH):
        j = i // N_MIXERS
        mod = cond @ ada_w[i] + ada_b[i]
        sh1, sc1, gt1, sh2, sc2, gt2 = jnp.split(mod, 6, axis=-1)
        h = modulate(x, norm_g[i, 0], sh1, sc1)
        if i % N_MIXERS == 0:
            y = rwkv7_time_mix(h, rwkv_mu[j], rwkv_w_rkv[j], rwkv_w_o[j], rwkv_w0[j], rwkv_w1[j], rwkv_w2[j],
                               rwkv_a0[j], rwkv_a1[j], rwkv_a2[j], rwkv_g1[j], rwkv_g2[j], rwkv_k_k[j],
                               rwkv_k_a[j], rwkv_r_k[j], rwkv_gn_w[j], rwkv_gn_b[j])
        else:
            y = rglru_block(h, lru_w_in[j], lru_conv_w[j], lru_conv_b[j], lru_w_gates[j], lru_b_gates[j],
                            lru_lam[j], lru_w_out[j])
        x = x + ((1.0 + gt1)[:, None, :] * y).astype(x.dtype)
        h = modulate(x, norm_g[i, 1], sh2, sc2)
        if i % 2 == 0:
            y = swiglu(h, ffn_w_gu[i // 2], ffn_w_d[i // 2])
        else:
            y = moe_swiglu(h, moe_w_router[i // 2], moe_b_router[i // 2], moe_w_gu[i // 2], moe_w_d[i // 2])
        x = x + ((1.0 + gt2)[:, None, :] * y).astype(x.dtype)
    return rmsnorm(x, final_g)
```

```python
import functools

import jax
import jax.numpy as jnp
from jax import lax
from jax.experimental import pallas as pl
from jax.experimental.pallas import tpu as pltpu

F32 = jnp.float32
BF16 = jnp.bfloat16
I32 = jnp.int32

HEAD = 64
CHUNK = 64
GROUP = 256
HEADS_PER_GROUP = GROUP // HEAD
GN_EPS = 64e-5
NORM_EPS = 1e-6
LRU_C = 8.0
CONV_WIDTH = 4
LRU_BLOCK = 256
TOP_K = 2
LANES = 128
VMEM_LIMIT = 56 * 1024 * 1024


def _dot(a, b):
    return jnp.dot(a, b, preferred_element_type=F32)


def _dot_nt(a, b):
    return lax.dot_general(a, b, (((1,), (1,)), ((), ())), preferred_element_type=F32)


def _dot_tn(a, b):
    return lax.dot_general(a, b, (((0,), (0,)), ((), ())), preferred_element_type=F32)


def _softplus(u):
    return jnp.maximum(u, 0.0) + jnp.log1p(jnp.exp(-jnp.abs(u)))


def _modulate(x, ng, sh, sc):
    ms = jnp.mean(x * x, axis=-1, keepdims=True)
    return x * lax.rsqrt(ms + NORM_EPS) * ng * (1.0 + sc) + sh


def _split_bf16(x):
    hi = x.astype(BF16)
    lo = (x - hi.astype(F32)).astype(BF16)
    return hi, lo


def _mod_kernel(c_ref, w_ref, b_ref, o_ref):
    c = c_ref[...]
    cond = c * jax.nn.sigmoid(c)
    o_ref[0] = _dot(cond.astype(BF16), w_ref[0].astype(BF16)) + b_ref[0]


def _ada_mod(c, ada_w, ada_b):
    depth, d, d6 = ada_w.shape
    b = c.shape[0]
    rows = 8
    c8 = jnp.pad(c, ((0, rows - b), (0, 0)))
    tn = 1024
    out = pl.pallas_call(
        _mod_kernel,
        grid=(depth, d6 // tn),
        in_specs=[
            pl.BlockSpec((rows, d), lambda i, j: (0, 0)),
            pl.BlockSpec((1, d, tn), lambda i, j: (i, 0, j)),
            pl.BlockSpec((1, 1, tn), lambda i, j: (i, 0, j)),
        ],
        out_specs=pl.BlockSpec((1, rows, tn), lambda i, j: (i, 0, j)),
        out_shape=jax.ShapeDtypeStruct((depth, rows, d6), F32),
        name="ada_mod",
    )(c8, ada_w, ada_b.reshape(depth, 1, d6))
    return out[:, :b]


def _rwkv_pre_kernel(x_ref, sh_ref, sc_ref, ng_ref, mu_ref, wrkv_ref, w1_ref, w2_ref,
                     a1_ref, a2_ref, g1_ref, g2_ref, vec_ref, seg_ref, segt_ref,
                     r_out, lw_out, k_out, v_out, kk_out, b_out, g_out, bonus_out,
                     hbuf):
    t = pl.program_id(1)
    tm = x_ref.shape[1]
    d = x_ref.shape[2]

    h = _modulate(x_ref[0], ng_ref[...], sh_ref[0], sc_ref[0])

    @pl.when(t == 0)
    def _():
        hbuf[0:8, :] = jnp.zeros((8, d), F32)

    hbuf[8:8 + tm, :] = h
    hprev = hbuf[7:7 + tm, :]
    hbuf[0:8, :] = h[tm - 8:tm, :]
    xx = hprev - h

    def mix(p):
        return (h + xx * mu_ref[p:p + 1, :]).astype(BF16)

    r = _dot(mix(0), wrkv_ref[0])
    k = _dot(mix(1), wrkv_ref[1])
    v = _dot(mix(2), wrkv_ref[2])
    wl = _dot(jnp.tanh(_dot(mix(3), w1_ref[...])).astype(BF16), w2_ref[...])
    al = _dot(_dot(mix(4), a1_ref[...]).astype(BF16), a2_ref[...])
    g = _dot(jax.nn.sigmoid(_dot(mix(5), g1_ref[...])).astype(BF16), g2_ref[...])

    w0 = vec_ref[0:1, :]
    a0 = vec_ref[1:2, :]
    k_k = vec_ref[2:3, :]
    k_a = vec_ref[3:4, :]
    r_k = vec_ref[4:5, :]

    def headsum(z):
        s = _dot(z.astype(BF16), seg_ref[...])
        s_hi, s_lo = _split_bf16(s)
        return _dot(jnp.concatenate([s_hi, s_lo], axis=1), segt_ref[...])

    w_log = -_softplus(-(w0 + wl)) - 0.5
    lw = -jnp.exp(w_log)
    a = jax.nn.sigmoid(a0 + al)
    kk = k * k_k
    kk = kk / jnp.maximum(jnp.sqrt(headsum(kk * kk)), 1e-12)
    k2 = k * (1.0 + (a - 1.0) * k_a)
    bonus = headsum(r * k2 * r_k) * v

    r_out[0] = r
    lw_out[0] = lw
    k_out[0] = k2
    v_out[0] = v
    kk_out[0] = kk
    b_out[0] = kk * a
    g_out[0] = g
    bonus_out[0] = bonus


def _pad_to(x, axis, size):
    pad = [(0, 0)] * x.ndim
    pad[axis] = (0, size - x.shape[axis])
    return jnp.pad(x, pad)


def _rwkv_pre(x, sh, sc, ng, mu, w_rkv, w1, w2, a1, a2, g1, g2, w0, a0, k_k, k_a, r_k, tm):
    b, t, d = x.shape
    nh = d // HEAD
    lw_pad = LANES * pl.cdiv(w1.shape[1], LANES)
    la_pad = LANES * pl.cdiv(a1.shape[1], LANES)
    lg_pad = LANES * pl.cdiv(g1.shape[1], LANES)
    vecs = _pad_to(jnp.stack([w0, a0, k_k, k_a, r_k.reshape(d)]), 0, 8)
    head_of_lane = jnp.arange(d) // HEAD
    seg = (head_of_lane[:, None] == jnp.arange(LANES)[None, :]).astype(BF16)
    segt = jnp.concatenate([seg.T, seg.T], axis=0)
    del nh
    full = lambda *shape: pl.BlockSpec(shape, lambda bi, ti: (0,) * len(shape))
    tok = pl.BlockSpec((1, tm, d), lambda bi, ti: (bi, ti, 0))
    vec = pl.BlockSpec((1, 1, d), lambda bi, ti: (bi, 0, 0))
    outs = pl.pallas_call(
        _rwkv_pre_kernel,
        grid=(b, t // tm),
        in_specs=[tok, vec, vec, full(1, d), full(8, d), full(3, d, d),
                  full(d, lw_pad), full(lw_pad, d), full(d, la_pad), full(la_pad, d),
                  full(d, lg_pad), full(lg_pad, d), full(8, d), full(d, LANES), full(2 * LANES, d)],
        out_specs=[tok] * 8,
        out_shape=[jax.ShapeDtypeStruct((b, t, d), F32)] * 8,
        scratch_shapes=[pltpu.VMEM((tm + 8, d), F32)],
        compiler_params=pltpu.CompilerParams(
            dimension_semantics=("arbitrary", "arbitrary"), vmem_limit_bytes=VMEM_LIMIT),
        name="rwkv_pre",
    )(x, sh[:, None, :], sc[:, None, :], ng[None, :], _pad_to(mu, 0, 8), w_rkv.astype(BF16),
      _pad_to(w1, 1, lw_pad).astype(BF16), _pad_to(w2, 0, lw_pad).astype(BF16),
      _pad_to(a1, 1, la_pad).astype(BF16), _pad_to(a2, 0, la_pad).astype(BF16),
      _pad_to(g1, 1, lg_pad).astype(BF16), _pad_to(g2, 0, lg_pad).astype(BF16),
      vecs, seg, segt)
    return outs


def _wkv_scan_kernel(r_ref, lw_ref, k_ref, v_ref, kk_ref, b_ref, g_ref, bonus_ref,
                     gnw_ref, gnb_ref, o_ref, h_scr, *, chunks_per_step):
    L = CHUNK
    W = GROUP

    @pl.when(pl.program_id(2) == 0)
    def _():
        h_scr[...] = jnp.zeros_like(h_scr)

    row = lax.broadcasted_iota(I32, (L, W), 0)
    lane = lax.broadcasted_iota(I32, (L, W), 1)
    sidx = lane & (L - 1)
    lane_head = lane >> 6
    strict = sidx < row
    incl = sidx <= row
    eye = (sidx == row).astype(F32)
    same16 = (row >> 4) == (sidx >> 4)
    same32 = (row >> 5) == (sidx >> 5)
    m16 = strict & same16
    m32 = strict & same32 & jnp.logical_not(same16)
    m64 = strict & jnp.logical_not(same32)
    rb = lax.broadcasted_iota(I32, (W, W), 0)
    cb = lax.broadcasted_iota(I32, (W, W), 1)
    bmask = (rb >> 6) == (cb >> 6)
    diag = rb == cb
    ones_bd = bmask.astype(BF16)
    tri_r = lax.broadcasted_iota(I32, (L, 3 * L), 0)
    tri_c = lax.broadcasted_iota(I32, (L, 3 * L), 1)
    tri3 = ((tri_c & (L - 1)) <= tri_r).astype(BF16)

    def bd(y):
        yt = jnp.concatenate([y] * HEADS_PER_GROUP, axis=0)
        return jnp.where(bmask, yt, 0.0).astype(BF16)

    def hmm(x, ybd):
        return _dot(x.astype(BF16), ybd)

    for j in range(chunks_per_step):
        sl = pl.ds(j * L, L)
        r = r_ref[0, sl, :]
        lw = lw_ref[0, sl, :]
        k = k_ref[0, sl, :]
        v = v_ref[0, sl, :]
        kk = kk_ref[0, sl, :]
        bv = b_ref[0, sl, :]

        lw_hi = lw.astype(BF16)
        rem = lw - lw_hi.astype(F32)
        lw_mid = rem.astype(BF16)
        lw_lo = (rem - lw_mid.astype(F32)).astype(BF16)
        cl = _dot(tri3, jnp.concatenate([lw_hi, lw_mid, lw_lo], axis=0))
        cl_last = cl[L - 1:L, :]
        e_pos = jnp.exp(cl)
        e_neg = jnp.exp(-cl)
        rt = r * e_pos
        at = -kk * jnp.exp(cl - lw)
        bt = bv * e_neg
        kt = k * e_neg
        e_end = jnp.exp(cl_last - cl)
        p_last = jnp.exp(cl_last)

        x = jnp.concatenate([at, rt], axis=0).astype(BF16)
        ys = [jnp.where(lane_head == hh, bt, 0.0) for hh in range(HEADS_PER_GROUP)]
        ys += [jnp.where(lane_head == hh, kt, 0.0) for hh in range(HEADS_PER_GROUP)]
        y = jnp.concatenate(ys, axis=0).astype(BF16)
        gm = _dot_nt(x, y)
        a_ab = jnp.where(strict, gm[:L, :W], 0.0)
        a_ak = jnp.where(strict, gm[:L, W:], 0.0)
        a_rb = jnp.where(incl, gm[L:, :W], 0.0)
        a_rk = jnp.where(incl, gm[L:, W:], 0.0)

        a0 = jnp.where(m16, a_ab, 0.0)
        s = hmm(a0, bd(a0))
        tinv = eye + a0
        for _ in range(2):
            ts = _dot(jnp.concatenate([tinv, s], axis=0).astype(BF16), bd(s))
            tinv = tinv + ts[:L]
            s = ts[L:]
        tinv = tinv + hmm(tinv, bd(s))
        for msk in (m32, m64):
            off = jnp.where(msk, a_ab, 0.0)
            tinv = tinv + hmm(tinv, bd(hmm(off, bd(tinv))))

        vbd = bd(v)
        av = hmm(a_ak, vbd)
        tx = _dot(tinv.astype(BF16), jnp.concatenate([bd(at), bd(av)], axis=1))
        ahat = tx[:, :W]
        vp = tx[:, W:]
        ox = _dot(a_rb.astype(BF16), jnp.concatenate([bd(ahat), bd(vp)], axis=1))
        rhat = rt + ox[:, :W]
        o_intra = ox[:, W:] + hmm(a_rk, vbd)

        z = jnp.concatenate([bv * e_end, k * e_end], axis=0).astype(BF16)
        wm = jnp.concatenate(
            [jnp.concatenate([ahat, vp], axis=1),
             jnp.concatenate([jnp.zeros((L, W), F32), v], axis=1)], axis=0).astype(BF16)
        mn = _dot_tn(z, wm)
        m_mat = jnp.where(bmask, mn[:, :W], 0.0) + jnp.where(diag, p_last, 0.0)
        n_mat = jnp.where(bmask, mn[:, W:], 0.0)

        h0 = h_scr[...]
        h0b = h0.astype(BF16)
        o = _dot(rhat.astype(BF16), h0b) + o_intra
        m_hi, m_lo = _split_bf16(m_mat)
        h_scr[...] = _dot(m_hi, h0b) + _dot(m_lo, h0b) + n_mat

        o_hi, o_lo = _split_bf16(o)
        mean = (_dot(o_hi, ones_bd) + _dot(o_lo, ones_bd)) * (1.0 / HEAD)
        dlt = o - mean
        d_hi, d_lo = _split_bf16(dlt * dlt)
        var = (_dot(d_hi, ones_bd) + _dot(d_lo, ones_bd)) * (1.0 / HEAD)
        yn = dlt * lax.rsqrt(var + GN_EPS) * gnw_ref[...] + gnb_ref[...]
        o_ref[0, sl, :] = ((yn + bonus_ref[0, sl, :]) * g_ref[0, sl, :]).astype(o_ref.dtype)


def _wkv_scan(r, lw, k, v, kk, bv, g, bonus, gn_w, gn_b, chunks_per_step):
    b, t, d = r.shape
    lb = CHUNK * chunks_per_step
    tok = pl.BlockSpec((1, lb, GROUP), lambda bi, qi, ci: (bi, ci, qi))
    vec = pl.BlockSpec((1, GROUP), lambda bi, qi, ci: (0, qi))
    return pl.pallas_call(
        functools.partial(_wkv_scan_kernel, chunks_per_step=chunks_per_step),
        grid=(b, d // GROUP, t // lb),
        in_specs=[tok] * 8 + [vec, vec],
        out_specs=tok,
        out_shape=jax.ShapeDtypeStruct((b, t, d), BF16),
        scratch_shapes=[pltpu.VMEM((GROUP, GROUP), F32)],
        compiler_params=pltpu.CompilerParams(
            dimension_semantics=("arbitrary", "arbitrary", "arbitrary"),
            vmem_limit_bytes=VMEM_LIMIT),
        name="wkv_scan",
    )(r, lw, k, v, kk, bv, g, bonus, gn_w[None, :], gn_b[None, :])


def _proj_res_kernel(a_ref, w_ref, x_ref, gt_ref, o_ref):
    o_ref[0] = x_ref[0] + (1.0 + gt_ref[0]) * _dot(a_ref[0], w_ref[...])


def _proj_residual(a, w, x, gt, tm):
    b, t, d = x.shape
    tok = pl.BlockSpec((1, tm, d), lambda bi, ti: (bi, ti, 0))
    return pl.pallas_call(
        _proj_res_kernel,
        grid=(b, t // tm),
        in_specs=[tok, pl.BlockSpec((d, d), lambda bi, ti: (0, 0)), tok,
                  pl.BlockSpec((1, 1, d), lambda bi, ti: (bi, 0, 0))],
        out_specs=tok,
        out_shape=jax.ShapeDtypeStruct((b, t, d), F32),
        compiler_params=pltpu.CompilerParams(
            dimension_semantics=("arbitrary", "arbitrary"), vmem_limit_bytes=VMEM_LIMIT),
        name="proj_residual",
    )(a, w.astype(BF16), x, gt[:, None, :])


def _ffn_dense_kernel(x_ref, sh_ref, sc_ref, gt_ref, ng_ref, wg_ref, wu_ref, wd_ref,
                      o_ref, h_scr, acc_scr):
    j = pl.program_id(2)

    @pl.when(j == 0)
    def _():
        h_scr[...] = _modulate(x_ref[0], ng_ref[...], sh_ref[0], sc_ref[0]).astype(BF16)
        acc_scr[...] = jnp.zeros_like(acc_scr)

    h = h_scr[...]
    g = _dot(h, wg_ref[...])
    u = _dot(h, wu_ref[...])
    act = (g * jax.nn.sigmoid(g) * u).astype(BF16)
    acc_scr[...] += _dot(act, wd_ref[...])

    @pl.when(j == pl.num_programs(2) - 1)
    def _():
        o_ref[0] = x_ref[0] + (1.0 + gt_ref[0]) * acc_scr[...]


def _ffn_dense(x, sh, sc, gt, ng, w_gu, w_d, tm, tf):
    b, t, d = x.shape
    f = w_d.shape[0]
    nf = f // tf
    tok = pl.BlockSpec((1, tm, d), lambda bi, ti, j: (bi, ti, 0))
    vec = pl.BlockSpec((1, 1, d), lambda bi, ti, j: (bi, 0, 0))
    return pl.pallas_call(
        _ffn_dense_kernel,
        grid=(b, t // tm, nf),
        in_specs=[tok, vec, vec, vec, pl.BlockSpec((1, d), lambda bi, ti, j: (0, 0)),
                  pl.BlockSpec((d, tf), lambda bi, ti, j: (0, j)),
                  pl.BlockSpec((d, tf), lambda bi, ti, j: (0, nf + j)),
                  pl.BlockSpec((tf, d), lambda bi, ti, j: (j, 0))],
        out_specs=tok,
        out_shape=jax.ShapeDtypeStruct((b, t, d), F32),
        scratch_shapes=[pltpu.VMEM((tm, d), BF16), pltpu.VMEM((tm, d), F32)],
        compiler_params=pltpu.CompilerParams(
            dimension_semantics=("arbitrary", "arbitrary", "arbitrary"),
            vmem_limit_bytes=VMEM_LIMIT),
        name="ffn_dense",
    )(x, sh[:, None, :], sc[:, None, :], gt[:, None, :], ng[None, :],
      w_gu.astype(BF16), w_gu.astype(BF16), w_d.astype(BF16))


def _lru_kernel(x_ref, sh_ref, sc_ref, gt_ref, ng_ref, win_ref, cw_ref, cb_ref, wg_ref, bg_ref,
                lam_ref, wout_ref, o_ref, xbuf, abuf, bbuf, carry):
    t = pl.program_id(1)
    tm = x_ref.shape[1]
    w = win_ref.shape[1] // 2
    nblk = w // LRU_BLOCK
    pad = tm

    @pl.when(t == 0)
    def _():
        xbuf[0:8, :] = jnp.zeros((8, w), F32)
        carry[...] = jnp.zeros_like(carry)
        for s in range(2):
            abuf[s, 0:pad, :] = jnp.ones((pad, w), F32)
            bbuf[s, 0:pad, :] = jnp.zeros((pad, w), F32)

    x = x_ref[0]
    h = _modulate(x, ng_ref[...], sh_ref[0], sc_ref[0]).astype(BF16)
    xg = _dot(h, win_ref[...])
    xb = xg[:, :w]
    gb = xg[:, w:]
    gate = 0.5 * gb * (1.0 + jnp.tanh(0.7978845608028654 * (gb + 0.044715 * gb * gb * gb)))

    xbuf[8:8 + tm, :] = xb
    conv = cb_ref[...] + cw_ref[CONV_WIDTH - 1:CONV_WIDTH, :] * xb
    for jj in range(CONV_WIDTH - 1):
        shift = CONV_WIDTH - 1 - jj
        conv = conv + cw_ref[jj:jj + 1, :] * xbuf[8 - shift:8 - shift + tm, :]
    xbuf[0:8, :] = xb[tm - 8:tm, :]

    conv_b = conv.astype(BF16)
    rs, is_ = [], []
    for n in range(nblk):
        gts = _dot(conv_b[:, n * LRU_BLOCK:(n + 1) * LRU_BLOCK], wg_ref[n]) + bg_ref[n]
        gts = jax.nn.sigmoid(gts)
        rs.append(gts[:, :LRU_BLOCK])
        is_.append(gts[:, LRU_BLOCK:])
    r_t = jnp.concatenate(rs, axis=1)
    i_t = jnp.concatenate(is_, axis=1)

    log_a = -LRU_C * r_t * _softplus(-lam_ref[...])
    a_t = jnp.exp(log_a)
    b_t = jnp.sqrt(-jnp.tanh(log_a) * (a_t * a_t + 1.0)) * (i_t * conv)

    abuf[0, pad:pad + tm, :] = a_t
    bbuf[0, pad:pad + tm, :] = b_t
    step = 1
    src = 0
    while step < tm:
        a_cur = abuf[src, pad:pad + tm, :]
        b_cur = bbuf[src, pad:pad + tm, :]
        a_sh = abuf[src, pad - step:pad - step + tm, :]
        b_sh = bbuf[src, pad - step:pad - step + tm, :]
        abuf[1 - src, pad:pad + tm, :] = a_cur * a_sh
        bbuf[1 - src, pad:pad + tm, :] = a_cur * b_sh + b_cur
        src = 1 - src
        step *= 2
    hs = bbuf[src, pad:pad + tm, :] + abuf[src, pad:pad + tm, :] * carry[0:1, :]
    carry[...] = jnp.broadcast_to(hs[tm - 1:tm, :], carry.shape)

    y = _dot((hs * gate).astype(BF16), wout_ref[...])
    o_ref[0] = x + (1.0 + gt_ref[0]) * y


def _lru_block(x, sh, sc, gt, ng, w_in, conv_w, conv_b, w_gates, b_gates, lam, w_out, tm):
    b, t, d = x.shape
    w = w_out.shape[0]
    nblk = w // LRU_BLOCK
    full = lambda *shape: pl.BlockSpec(shape, lambda bi, ti: (0,) * len(shape))
    tok = pl.BlockSpec((1, tm, d), lambda bi, ti: (bi, ti, 0))
    vec = pl.BlockSpec((1, 1, d), lambda bi, ti: (bi, 0, 0))
    return pl.pallas_call(
        _lru_kernel,
        grid=(b, t // tm),
        in_specs=[tok, vec, vec, vec, full(1, d), full(d, 2 * w), full(CONV_WIDTH, w), full(1, w),
                  full(nblk, LRU_BLOCK, 2 * LRU_BLOCK), full(nblk, 1, 2 * LRU_BLOCK),
                  full(1, w), full(w, d)],
        out_specs=tok,
        out_shape=jax.ShapeDtypeStruct((b, t, d), F32),
        scratch_shapes=[pltpu.VMEM((tm + 8, w), F32), pltpu.VMEM((2, 2 * tm, w), F32),
                        pltpu.VMEM((2, 2 * tm, w), F32), pltpu.VMEM((8, w), F32)],
        compiler_params=pltpu.CompilerParams(
            dimension_semantics=("arbitrary", "arbitrary"), vmem_limit_bytes=VMEM_LIMIT),
        name="rglru_block",
    )(x, sh[:, None, :], sc[:, None, :], gt[:, None, :], ng[None, :], w_in.astype(BF16),
      conv_w, conv_b[None, :], w_gates.astype(BF16), b_gates[:, None, :], lam[None, :],
      w_out.astype(BF16))


def _router_kernel(x_ref, sh_ref, sc_ref, ng_ref, wr_ref, br_ref,
                   h_out, eid_out, rank_out, prob_out, cnt_out, cnt_scr):
    i = pl.program_id(0)
    tm = x_ref.shape[0]

    @pl.when(i == 0)
    def _():
        cnt_scr[...] = jnp.zeros_like(cnt_scr)

    h = _modulate(x_ref[...], ng_ref[...], sh_ref[0], sc_ref[0])
    h_out[...] = h
    h_hi, h_lo = _split_bf16(h)
    w_hi, w_lo = _split_bf16(wr_ref[...])
    logits = _dot(h_hi, w_hi) + _dot(h_lo, w_hi) + _dot(h_hi, w_lo) + br_ref[...]

    lane = lax.broadcasted_iota(I32, logits.shape, 1)
    m1 = jnp.max(logits, axis=-1, keepdims=True)
    i1 = jnp.min(jnp.where(logits == m1, lane, LANES), axis=-1, keepdims=True)
    l2 = jnp.where(lane == i1, -jnp.inf, logits)
    m2 = jnp.max(l2, axis=-1, keepdims=True)
    i2 = jnp.min(jnp.where(l2 == m2, lane, LANES), axis=-1, keepdims=True)
    e = jnp.exp(m2 - m1)
    p1 = 1.0 / (1.0 + e)
    p2 = e / (1.0 + e)

    oh1 = (lane == i1).astype(F32)
    oh2 = (lane == i2).astype(F32)
    oh = oh1 + oh2
    rr = lax.broadcasted_iota(I32, (tm, tm), 0)
    cc = lax.broadcasted_iota(I32, (tm, tm), 1)
    tri = (cc < rr).astype(BF16)
    before = _dot(tri, oh.astype(BF16)) + cnt_scr[0:1, :]
    rank1 = jnp.sum(before * oh1, axis=-1, keepdims=True)
    rank2 = jnp.sum(before * oh2, axis=-1, keepdims=True)
    cnt_scr[...] = cnt_scr[...] + jnp.sum(oh, axis=0, keepdims=True)

    eid_out[:, 0:1] = i1
    eid_out[:, 1:2] = i2
    rank_out[:, 0:1] = rank1.astype(I32)
    rank_out[:, 1:2] = rank2.astype(I32)
    prob_out[:, 0:1] = p1
    prob_out[:, 1:2] = p2
    cnt_out[...] = cnt_scr[...].astype(I32)


def _router(x2d, sh, sc, ng, w_router, b_router, tiles_per_batch, tm):
    n, d = x2d.shape
    e = w_router.shape[1]
    wr = _pad_to(w_router, 1, LANES)
    br = jnp.concatenate([b_router, jnp.full((LANES - e,), -1e30, F32)])[None, :]
    tok = pl.BlockSpec((tm, d), lambda i: (i, 0))
    vec = pl.BlockSpec((1, 1, d), lambda i: (i // tiles_per_batch, 0, 0))
    two = pl.BlockSpec((tm, TOP_K), lambda i: (i, 0))
    return pl.pallas_call(
        _router_kernel,
        grid=(n // tm,),
        in_specs=[tok, vec, vec, pl.BlockSpec((1, d), lambda i: (0, 0)),
                  pl.BlockSpec((d, LANES), lambda i: (0, 0)), pl.BlockSpec((1, LANES), lambda i: (0, 0))],
        out_specs=[tok, two, two, two, pl.BlockSpec((8, LANES), lambda i: (0, 0))],
        out_shape=[jax.ShapeDtypeStruct((n, d), F32), jax.ShapeDtypeStruct((n, TOP_K), I32),
                   jax.ShapeDtypeStruct((n, TOP_K), I32), jax.ShapeDtypeStruct((n, TOP_K), F32),
                   jax.ShapeDtypeStruct((8, LANES), I32)],
        scratch_shapes=[pltpu.VMEM((8, LANES), F32)],
        compiler_params=pltpu.CompilerParams(
            dimension_semantics=("arbitrary",), vmem_limit_bytes=VMEM_LIMIT),
        name="moe_router",
    )(x2d, sh[:, None, :], sc[:, None, :], ng[None, :], wr, br)


def _dispatch_kernel(pos_ref, h_hbm, xs_in, xs_out, sem, *, tb):
    del xs_in
    i = pl.program_id(0)

    def row_copy(tok, p):
        return pltpu.make_async_copy(h_hbm.at[pl.ds(tok, 1)], xs_out.at[pl.ds(p, 1)], sem)

    def issue(t, c):
        for kk in range(TOP_K):
            row_copy(i * tb + t, pos_ref[TOP_K * t + kk]).start()
        return c

    lax.fori_loop(0, tb, issue, 0)

    def drain(t, c):
        for kk in range(TOP_K):
            row_copy(i * tb + t, pos_ref[TOP_K * t + kk]).wait()
        return c

    lax.fori_loop(0, tb, drain, 0)


def _dispatch(pos_flat, h2d, rows, tb):
    n, d = h2d.shape
    xs0 = jnp.zeros((rows, d), h2d.dtype)
    return pl.pallas_call(
        functools.partial(_dispatch_kernel, tb=tb),
        grid=(n // tb,),
        in_specs=[pl.BlockSpec((TOP_K * tb,), lambda i: (i,), memory_space=pltpu.SMEM),
                  pl.BlockSpec(memory_space=pl.ANY), pl.BlockSpec(memory_space=pl.ANY)],
        out_specs=pl.BlockSpec(memory_space=pl.ANY),
        out_shape=jax.ShapeDtypeStruct((rows, d), h2d.dtype),
        scratch_shapes=[pltpu.SemaphoreType.DMA(())],
        input_output_aliases={2: 0},
        compiler_params=pltpu.CompilerParams(dimension_semantics=("arbitrary",)),
        name="moe_dispatch",
    )(pos_flat, h2d, xs0)


def _ffn_moe_kernel(gid_ref, valid_ref, xs_ref, wg_ref, wu_ref, wd_ref, o_ref, acc_scr):
    del gid_ref
    i = pl.program_id(0)
    j = pl.program_id(1)
    last = pl.num_programs(1) - 1
    valid = valid_ref[i] > 0

    @pl.when(valid)
    def _():
        @pl.when(j == 0)
        def _():
            acc_scr[...] = jnp.zeros_like(acc_scr)

        h = xs_ref[...].astype(BF16)
        g = _dot(h, wg_ref[0])
        u = _dot(h, wu_ref[0])
        act = (g * jax.nn.sigmoid(g) * u).astype(BF16)
        acc_scr[...] += _dot(act, wd_ref[0])

        @pl.when(j == last)
        def _():
            o_ref[...] = acc_scr[...]

    @pl.when(jnp.logical_not(valid) & (j == last))
    def _():
        o_ref[...] = jnp.zeros_like(o_ref)


def _ffn_moe(gid, valid, xs, w_gu, w_d, tm, tf):
    rows, d = xs.shape
    f = w_d.shape[1]
    nf = f // tf
    grid_spec = pltpu.PrefetchScalarGridSpec(
        num_scalar_prefetch=2,
        grid=(rows // tm, nf),
        in_specs=[pl.BlockSpec((tm, d), lambda i, j, gid, vld: (i, 0)),
                  pl.BlockSpec((1, d, tf), lambda i, j, gid, vld: (gid[i], 0, j)),
                  pl.BlockSpec((1, d, tf), lambda i, j, gid, vld: (gid[i], 0, nf + j)),
                  pl.BlockSpec((1, tf, d), lambda i, j, gid, vld: (gid[i], j, 0))],
        out_specs=pl.BlockSpec((tm, d), lambda i, j, gid, vld: (i, 0)),
        scratch_shapes=[pltpu.VMEM((tm, d), F32)],
    )
    w_gu_b = w_gu.astype(BF16)
    return pl.pallas_call(
        _ffn_moe_kernel,
        grid_spec=grid_spec,
        out_shape=jax.ShapeDtypeStruct((rows, d), F32),
        compiler_params=pltpu.CompilerParams(
            dimension_semantics=("arbitrary", "arbitrary"), vmem_limit_bytes=VMEM_LIMIT),
        name="ffn_moe",
    )(gid, valid, xs, w_gu_b, w_gu_b, w_d.astype(BF16))


def _combine_kernel(pos_ref, ys_hbm, x_ref, gt_ref, prob_ref, fg_ref, o_ref, ybuf, sem, *, tc):
    def row_copy(t, kk):
        return pltpu.make_async_copy(
            ys_hbm.at[pl.ds(pos_ref[TOP_K * t + kk], 1)], ybuf.at[kk, pl.ds(t, 1)], sem)

    def issue(t, c):
        for kk in range(TOP_K):
            row_copy(t, kk).start()
        return c

    lax.fori_loop(0, tc, issue, 0)

    def drain(t, c):
        for kk in range(TOP_K):
            row_copy(t, kk).wait()
        return c

    lax.fori_loop(0, tc, drain, 0)

    y = prob_ref[:, 0:1] * ybuf[0] + prob_ref[:, 1:2] * ybuf[1]
    x = x_ref[...] + (1.0 + gt_ref[0]) * y
    ms = jnp.mean(x * x, axis=-1, keepdims=True)
    o_ref[...] = x * lax.rsqrt(ms + NORM_EPS) * fg_ref[...]


def _combine(pos_flat, ys, x2d, gt, prob, final_g, tiles_per_batch, tc):
    n, d = x2d.shape
    tok = pl.BlockSpec((tc, d), lambda i: (i, 0))
    return pl.pallas_call(
        functools.partial(_combine_kernel, tc=tc),
        grid=(n // tc,),
        in_specs=[pl.BlockSpec((TOP_K * tc,), lambda i: (i,), memory_space=pltpu.SMEM),
                  pl.BlockSpec(memory_space=pl.ANY), tok,
                  pl.BlockSpec((1, 1, d), lambda i: (i // tiles_per_batch, 0, 0)),
                  pl.BlockSpec((tc, TOP_K), lambda i: (i, 0)),
                  pl.BlockSpec((1, d), lambda i: (0, 0))],
        out_specs=tok,
        out_shape=jax.ShapeDtypeStruct((n, d), F32),
        scratch_shapes=[pltpu.VMEM((TOP_K, tc, d), F32), pltpu.SemaphoreType.DMA(())],
        compiler_params=pltpu.CompilerParams(
            dimension_semantics=("arbitrary",), vmem_limit_bytes=VMEM_LIMIT),
        name="moe_combine",
    )(pos_flat, ys, x2d, gt[:, None, :], prob, final_g[None, :])


def _moe_layout(eid, rank, counts, n_experts, tm, n_tiles):
    cnt = counts[0, :n_experts]
    tiles = (cnt + tm - 1) // tm
    tile_end = jnp.cumsum(tiles)
    offsets = (tile_end - tiles) * tm
    pos = offsets[eid] + rank
    tile_ids = jnp.arange(n_tiles, dtype=I32)
    total = tile_end[-1]
    last_id = jnp.minimum(tile_ids, total - 1)
    gid = jnp.sum((last_id[:, None] >= tile_end[None, :]).astype(I32), axis=1)
    valid = (tile_ids < total).astype(I32)
    return pos.reshape(-1).astype(I32), gid, valid


def kernel(x, c, ada_w, ada_b, norm_g, final_g, rwkv_mu, rwkv_w_rkv, rwkv_w_o, rwkv_w0, rwkv_w1, rwkv_w2, rwkv_a0, rwkv_a1, rwkv_a2, rwkv_g1, rwkv_g2, rwkv_k_k, rwkv_k_a, rwkv_r_k, rwkv_gn_w, rwkv_gn_b, lru_w_in, lru_conv_w, lru_conv_b, lru_w_gates, lru_b_gates, lru_lam, lru_w_out, ffn_w_gu, ffn_w_d, moe_w_router, moe_b_router, moe_w_gu, moe_w_d):
    b, t, d = x.shape
    n = b * t
    n_experts = moe_w_router.shape[-1]
    mod = _ada_mod(c, ada_w, ada_b)

    def mods(i):
        return [mod[i, :, q * d:(q + 1) * d] for q in range(6)]

    sh1, sc1, gt1, sh2, sc2, gt2 = mods(0)
    r, lw, k2, v, kk, bv, g, bonus = _rwkv_pre(
        x, sh1, sc1, norm_g[0, 0], rwkv_mu[0], rwkv_w_rkv[0], rwkv_w1[0], rwkv_w2[0],
        rwkv_a1[0], rwkv_a2[0], rwkv_g1[0], rwkv_g2[0], rwkv_w0[0], rwkv_a0[0],
        rwkv_k_k[0], rwkv_k_a[0], rwkv_r_k[0], tm=min(256, t))
    yg = _wkv_scan(r, lw, k2, v, kk, bv, g, bonus, rwkv_gn_w[0], rwkv_gn_b[0],
                   chunks_per_step=min(4, t // CHUNK))
    x = _proj_residual(yg, rwkv_w_o[0], x, gt1, tm=min(512, t))
    x = _ffn_dense(x, sh2, sc2, gt2, norm_g[0, 1], ffn_w_gu[0], ffn_w_d[0], tm=min(1024, t), tf=512)

    sh1, sc1, gt1, sh2, sc2, gt2 = mods(1)
    x = _lru_block(x, sh1, sc1, gt1, norm_g[1, 0], lru_w_in[0], lru_conv_w[0], lru_conv_b[0],
                   lru_w_gates[0], lru_b_gates[0], lru_lam[0], lru_w_out[0], tm=min(256, t))

    x2d = x.reshape(n, d)
    tm_r = min(512, t)
    h2, eid, rank, prob, counts = _router(x2d, sh2, sc2, norm_g[1, 1], moe_w_router[0],
                                          moe_b_router[0], t // tm_r, tm_r)
    tm_g = min(512, t)
    n_tiles = (TOP_K * n) // tm_g + n_experts
    pos, gid, valid = _moe_layout(eid, rank, counts, n_experts, tm_g, n_tiles)
    xs = _dispatch(pos, h2, n_tiles * tm_g, tb=min(1024, n))
    ys = _ffn_moe(gid, valid, xs, moe_w_gu[0], moe_w_d[0], tm_g, tf=512)
    tc = min(512, t)
    out = _combine(pos, ys, x2d, gt2, prob, final_g, t // tc, tc)
    return out.reshape(b, t, d)
```

```python
import functools

import jax
import jax.numpy as jnp
from jax import lax
from jax.experimental import pallas as pl
from jax.experimental.pallas import tpu as pltpu

F32 = jnp.float32
BF16 = jnp.bfloat16
I32 = jnp.int32

HEAD = 64
CHUNK = 64
GROUP = 256
HEADS_PER_GROUP = GROUP // HEAD
GN_EPS = 64e-5
NORM_EPS = 1e-6
LRU_C = 8.0
CONV_WIDTH = 4
LRU_BLOCK = 256
TOP_K = 2
LANES = 128
VMEM_LIMIT = 56 * 1024 * 1024


def _dot(a, b):
    return jnp.dot(a, b, preferred_element_type=F32)


def _dot_nt(a, b):
    return lax.dot_general(a, b, (((1,), (1,)), ((), ())), preferred_element_type=F32)


def _dot_tn(a, b):
    return lax.dot_general(a, b, (((0,), (0,)), ((), ())), preferred_element_type=F32)


def _softplus(u):
    return jnp.maximum(u, 0.0) + jnp.log1p(jnp.exp(-jnp.abs(u)))


def _modulate(x, ng, sh, sc):
    ms = jnp.mean(x * x, axis=-1, keepdims=True)
    return x * lax.rsqrt(ms + NORM_EPS) * ng * (1.0 + sc) + sh


def _split_bf16(x):
    hi = x.astype(BF16)
    lo = (x - hi.astype(F32)).astype(BF16)
    return hi, lo


def _mod_kernel(c_ref, w_ref, b_ref, o_ref):
    c = c_ref[...]
    cond = c * jax.nn.sigmoid(c)
    o_ref[0] = _dot(cond.astype(BF16), w_ref[0].astype(BF16)) + b_ref[0]


def _ada_mod(c, ada_w, ada_b):
    depth, d, d6 = ada_w.shape
    b = c.shape[0]
    rows = 8
    c8 = jnp.pad(c, ((0, rows - b), (0, 0)))
    tn = 1024
    out = pl.pallas_call(
        _mod_kernel,
        grid=(depth, d6 // tn),
        in_specs=[
            pl.BlockSpec((rows, d), lambda i, j: (0, 0)),
            pl.BlockSpec((1, d, tn), lambda i, j: (i, 0, j)),
            pl.BlockSpec((1, 1, tn), lambda i, j: (i, 0, j)),
        ],
        out_specs=pl.BlockSpec((1, rows, tn), lambda i, j: (i, 0, j)),
        out_shape=jax.ShapeDtypeStruct((depth, rows, d6), F32),
        name="ada_mod",
    )(c8, ada_w, ada_b.reshape(depth, 1, d6))
    return out[:, :b]


def _rwkv_pre_kernel(x_ref, sh_ref, sc_ref, ng_ref, mu_ref, wrkv_ref, w1_ref, w2_ref,
                     a1_ref, a2_ref, g1_ref, g2_ref, vec_ref, seg_ref, segt_ref,
                     r_out, lw_out, k_out, v_out, kk_out, b_out, g_out, bonus_out,
                     hbuf):
    t = pl.program_id(1)
    tm = x_ref.shape[1]
    d = x_ref.shape[2]

    h = _modulate(x_ref[0], ng_ref[...], sh_ref[0], sc_ref[0])

    @pl.when(t == 0)
    def _():
        hbuf[0:8, :] = jnp.zeros((8, d), F32)

    hbuf[8:8 + tm, :] = h
    hprev = hbuf[7:7 + tm, :]
    hbuf[0:8, :] = h[tm - 8:tm, :]
    xx = hprev - h

    def mix(p):
        return (h + xx * mu_ref[p:p + 1, :]).astype(BF16)

    r = _dot(mix(0), wrkv_ref[0])
    k = _dot(mix(1), wrkv_ref[1])
    v = _dot(mix(2), wrkv_ref[2])
    wl = _dot(jnp.tanh(_dot(mix(3), w1_ref[...])).astype(BF16), w2_ref[...])
    al = _dot(_dot(mix(4), a1_ref[...]).astype(BF16), a2_ref[...])
    g = _dot(jax.nn.sigmoid(_dot(mix(5), g1_ref[...])).astype(BF16), g2_ref[...])

    w0 = vec_ref[0:1, :]
    a0 = vec_ref[1:2, :]
    k_k = vec_ref[2:3, :]
    k_a = vec_ref[3:4, :]
    r_k = vec_ref[4:5, :]

    def headsum(z):
        s = _dot(z.astype(BF16), seg_ref[...])
        s_hi, s_lo = _split_bf16(s)
        return _dot(jnp.concatenate([s_hi, s_lo], axis=1), segt_ref[...])

    w_log = -_softplus(-(w0 + wl)) - 0.5
    lw = -jnp.exp(w_log)
    a = jax.nn.sigmoid(a0 + al)
    kk = k * k_k
    kk = kk / jnp.maximum(jnp.sqrt(headsum(kk * kk)), 1e-12)
    k2 = k * (1.0 + (a - 1.0) * k_a)
    bonus = headsum(r * k2 * r_k) * v

    r_out[0] = r
    lw_out[0] = lw
    k_out[0] = k2
    v_out[0] = v
    kk_out[0] = kk
    b_out[0] = kk * a
    g_out[0] = g
    bonus_out[0] = bonus


def _pad_to(x, axis, size):
    pad = [(0, 0)] * x.ndim
    pad[axis] = (0, size - x.shape[axis])
    return jnp.pad(x, pad)


def _rwkv_pre(x, sh, sc, ng, mu, w_rkv, w1, w2, a1, a2, g1, g2, w0, a0, k_k, k_a, r_k, tm):
    b, t, d = x.shape
    nh = d // HEAD
    lw_pad = LANES * pl.cdiv(w1.shape[1], LANES)
    la_pad = LANES * pl.cdiv(a1.shape[1], LANES)
    lg_pad = LANES * pl.cdiv(g1.shape[1], LANES)
    vecs = _pad_to(jnp.stack([w0, a0, k_k, k_a, r_k.reshape(d)]), 0, 8)
    head_of_lane = jnp.arange(d) // HEAD
    seg = (head_of_lane[:, None] == jnp.arange(LANES)[None, :]).astype(BF16)
    segt = jnp.concatenate([seg.T, seg.T], axis=0)
    del nh
    full = lambda *shape: pl.BlockSpec(shape, lambda bi, ti: (0,) * len(shape))
    tok = pl.BlockSpec((1, tm, d), lambda bi, ti: (bi, ti, 0))
    vec = pl.BlockSpec((1, 1, d), lambda bi, ti: (bi, 0, 0))
    outs = pl.pallas_call(
        _rwkv_pre_kernel,
        grid=(b, t // tm),
        in_specs=[tok, vec, vec, full(1, d), full(8, d), full(3, d, d),
                  full(d, lw_pad), full(lw_pad, d), full(d, la_pad), full(la_pad, d),
                  full(d, lg_pad), full(lg_pad, d), full(8, d), full(d, LANES), full(2 * LANES, d)],
        out_specs=[tok] * 8,
        out_shape=[jax.ShapeDtypeStruct((b, t, d), F32)] * 8,
        scratch_shapes=[pltpu.VMEM((tm + 8, d), F32)],
        compiler_params=pltpu.CompilerParams(
            dimension_semantics=("arbitrary", "arbitrary"), vmem_limit_bytes=VMEM_LIMIT),
        name="rwkv_pre",
    )(x, sh[:, None, :], sc[:, None, :], ng[None, :], _pad_to(mu, 0, 8), w_rkv.astype(BF16),
      _pad_to(w1, 1, lw_pad).astype(BF16), _pad_to(w2, 0, lw_pad).astype(BF16),
      _pad_to(a1, 1, la_pad).astype(BF16), _pad_to(a2, 0, la_pad).astype(BF16),
      _pad_to(g1, 1, lg_pad).astype(BF16), _pad_to(g2, 0, lg_pad).astype(BF16),
      vecs, seg, segt)
    return outs


def _wkv_scan_kernel(r_ref, lw_ref, k_ref, v_ref, kk_ref, b_ref, g_ref, bonus_ref,
                     gnw_ref, gnb_ref, o_ref, h_scr, *, chunks_per_step):
    L = CHUNK
    W = GROUP
    ngroups = r_ref.shape[2] // W

    @pl.when(pl.program_id(1) == 0)
    def _():
        h_scr[...] = jnp.zeros_like(h_scr)

    row = lax.broadcasted_iota(I32, (L, W), 0)
    lane = lax.broadcasted_iota(I32, (L, W), 1)
    sidx = lane & (L - 1)
    lane_head = lane >> 6
    strict = sidx < row
    incl = sidx <= row
    eye = (sidx == row).astype(F32)
    same16 = (row >> 4) == (sidx >> 4)
    same32 = (row >> 5) == (sidx >> 5)
    m16 = strict & same16
    m32 = strict & same32 & jnp.logical_not(same16)
    m64 = strict & jnp.logical_not(same32)
    rb = lax.broadcasted_iota(I32, (W, W), 0)
    cb = lax.broadcasted_iota(I32, (W, W), 1)
    bmask = (rb >> 6) == (cb >> 6)
    diag = rb == cb
    ones_bd = bmask.astype(BF16)
    tri_r = lax.broadcasted_iota(I32, (L, 3 * L), 0)
    tri_c = lax.broadcasted_iota(I32, (L, 3 * L), 1)
    tri3 = ((tri_c & (L - 1)) <= tri_r).astype(BF16)

    def bd(y):
        yt = jnp.concatenate([y] * HEADS_PER_GROUP, axis=0)
        return jnp.where(bmask, yt, 0.0).astype(BF16)

    def hmm(x, ybd):
        return _dot(x.astype(BF16), ybd)

    streams = [(q, j) for j in range(chunks_per_step) for q in range(ngroups)]
    S = range(len(streams))

    def ld(ref, s):
        q, j = streams[s]
        return ref[0, j * L:(j + 1) * L, q * W:(q + 1) * W]

    r = [ld(r_ref, s) for s in S]
    lw = [ld(lw_ref, s) for s in S]
    k = [ld(k_ref, s) for s in S]
    v = [ld(v_ref, s) for s in S]
    kk = [ld(kk_ref, s) for s in S]
    bv = [ld(b_ref, s) for s in S]

    def cumlog(x):
        hi = x.astype(BF16)
        rem = x - hi.astype(F32)
        mid = rem.astype(BF16)
        lo = (rem - mid.astype(F32)).astype(BF16)
        return _dot(tri3, jnp.concatenate([hi, mid, lo], axis=0))

    cl = [cumlog(lw[s]) for s in S]
    cl_last = [cl[s][L - 1:L, :] for s in S]
    e_pos = [jnp.exp(cl[s]) for s in S]
    e_neg = [jnp.exp(-cl[s]) for s in S]
    e_end = [jnp.exp(cl_last[s] - cl[s]) for s in S]
    rt = [r[s] * e_pos[s] for s in S]
    at = [-kk[s] * jnp.exp(cl[s] - lw[s]) for s in S]
    bt = [bv[s] * e_neg[s] for s in S]
    kt = [k[s] * e_neg[s] for s in S]

    def gram(s):
        x = jnp.concatenate([at[s], rt[s]], axis=0).astype(BF16)
        ys = [jnp.where(lane_head == hh, bt[s], 0.0) for hh in range(HEADS_PER_GROUP)]
        ys += [jnp.where(lane_head == hh, kt[s], 0.0) for hh in range(HEADS_PER_GROUP)]
        return _dot_nt(x, jnp.concatenate(ys, axis=0).astype(BF16))

    gm = [gram(s) for s in S]
    a_ab = [jnp.where(strict, gm[s][:L, :W], 0.0) for s in S]
    a_ak = [jnp.where(strict, gm[s][:L, W:], 0.0) for s in S]
    a_rb = [jnp.where(incl, gm[s][L:, :W], 0.0) for s in S]
    a_rk = [jnp.where(incl, gm[s][L:, W:], 0.0) for s in S]

    a0 = [jnp.where(m16, a_ab[s], 0.0) for s in S]
    pw = [hmm(a0[s], bd(a0[s])) for s in S]
    tinv = [eye + a0[s] for s in S]
    for _ in range(2):
        ts = [_dot(jnp.concatenate([tinv[s], pw[s]], axis=0).astype(BF16), bd(pw[s])) for s in S]
        tinv = [tinv[s] + ts[s][:L] for s in S]
        pw = [ts[s][L:] for s in S]
    tinv = [tinv[s] + hmm(tinv[s], bd(pw[s])) for s in S]
    for msk in (m32, m64):
        inner = [hmm(jnp.where(msk, a_ab[s], 0.0), bd(tinv[s])) for s in S]
        tinv = [tinv[s] + hmm(tinv[s], bd(inner[s])) for s in S]

    vbd = [bd(v[s]) for s in S]
    av = [hmm(a_ak[s], vbd[s]) for s in S]
    tx = [_dot(tinv[s].astype(BF16), jnp.concatenate([bd(at[s]), bd(av[s])], axis=1)) for s in S]
    ahat = [tx[s][:, :W] for s in S]
    vp = [tx[s][:, W:] for s in S]
    ox = [_dot(a_rb[s].astype(BF16), jnp.concatenate([bd(ahat[s]), bd(vp[s])], axis=1)) for s in S]
    rhat = [rt[s] + ox[s][:, :W] for s in S]
    o_intra = [ox[s][:, W:] + hmm(a_rk[s], vbd[s]) for s in S]

    def state_terms(s):
        z = jnp.concatenate([bv[s] * e_end[s], k[s] * e_end[s]], axis=0).astype(BF16)
        wm = jnp.concatenate(
            [jnp.concatenate([ahat[s], vp[s]], axis=1),
             jnp.concatenate([jnp.zeros((L, W), F32), v[s]], axis=1)], axis=0).astype(BF16)
        mn = _dot_tn(z, wm)
        m_mat = jnp.where(bmask, mn[:, :W], 0.0) + jnp.where(diag, jnp.exp(cl_last[s]), 0.0)
        return m_mat, jnp.where(bmask, mn[:, W:], 0.0)

    mn = [state_terms(s) for s in S]

    o = [None] * len(streams)
    for q in range(ngroups):
        h = h_scr[q]
        for j in range(chunks_per_step):
            s = streams.index((q, j))
            m_hi, m_lo = _split_bf16(mn[s][0])
            lhs = jnp.concatenate([m_hi, m_lo, rhat[s].astype(BF16)], axis=0)
            res = _dot(lhs, h.astype(BF16))
            o[s] = res[2 * W:] + o_intra[s]
            h = res[:W] + res[W:2 * W] + mn[s][1]
        h_scr[q] = h

    def headmean(zs):
        parts = []
        for z in zs:
            parts += list(_split_bf16(z))
        red = _dot(jnp.concatenate(parts, axis=0), ones_bd) * (1.0 / HEAD)
        return [red[2 * L * s:2 * L * s + L] + red[2 * L * s + L:2 * L * (s + 1)] for s in S]

    mean = headmean(o)
    dlt = [o[s] - mean[s] for s in S]
    var = headmean([dlt[s] * dlt[s] for s in S])
    for s in S:
        q, j = streams[s]
        gsl = slice(q * W, (q + 1) * W)
        rsl = slice(j * L, (j + 1) * L)
        yn = dlt[s] * lax.rsqrt(var[s] + GN_EPS) * gnw_ref[:, gsl] + gnb_ref[:, gsl]
        o_ref[0, rsl, gsl] = ((yn + bonus_ref[0, rsl, gsl]) * g_ref[0, rsl, gsl]).astype(o_ref.dtype)


def _wkv_scan(r, lw, k, v, kk, bv, g, bonus, gn_w, gn_b, chunks_per_step):
    b, t, d = r.shape
    lb = CHUNK * chunks_per_step
    tok = pl.BlockSpec((1, lb, d), lambda bi, ci: (bi, ci, 0))
    vec = pl.BlockSpec((1, d), lambda bi, ci: (0, 0))
    return pl.pallas_call(
        functools.partial(_wkv_scan_kernel, chunks_per_step=chunks_per_step),
        grid=(b, t // lb),
        in_specs=[tok] * 8 + [vec, vec],
        out_specs=tok,
        out_shape=jax.ShapeDtypeStruct((b, t, d), BF16),
        scratch_shapes=[pltpu.VMEM((d // GROUP, GROUP, GROUP), F32)],
        compiler_params=pltpu.CompilerParams(
            dimension_semantics=("arbitrary", "arbitrary"), vmem_limit_bytes=VMEM_LIMIT),
        name="wkv_scan",
    )(r, lw, k, v, kk, bv, g, bonus, gn_w[None, :], gn_b[None, :])


def _proj_res_kernel(a_ref, w_ref, x_ref, gt_ref, o_ref):
    o_ref[0] = x_ref[0] + (1.0 + gt_ref[0]) * _dot(a_ref[0], w_ref[...])


def _proj_residual(a, w, x, gt, tm):
    b, t, d = x.shape
    tok = pl.BlockSpec((1, tm, d), lambda bi, ti: (bi, ti, 0))
    return pl.pallas_call(
        _proj_res_kernel,
        grid=(b, t // tm),
        in_specs=[tok, pl.BlockSpec((d, d), lambda bi, ti: (0, 0)), tok,
                  pl.BlockSpec((1, 1, d), lambda bi, ti: (bi, 0, 0))],
        out_specs=tok,
        out_shape=jax.ShapeDtypeStruct((b, t, d), F32),
        compiler_params=pltpu.CompilerParams(
            dimension_semantics=("arbitrary", "arbitrary"), vmem_limit_bytes=VMEM_LIMIT),
        name="proj_residual",
    )(a, w.astype(BF16), x, gt[:, None, :])


def _ffn_dense_kernel(x_ref, sh_ref, sc_ref, gt_ref, ng_ref, wg_ref, wu_ref, wd_ref,
                      o_ref, h_scr, acc_scr):
    j = pl.program_id(2)

    @pl.when(j == 0)
    def _():
        h_scr[...] = _modulate(x_ref[0], ng_ref[...], sh_ref[0], sc_ref[0]).astype(BF16)
        acc_scr[...] = jnp.zeros_like(acc_scr)

    h = h_scr[...]
    g = _dot(h, wg_ref[...])
    u = _dot(h, wu_ref[...])
    act = (g * jax.nn.sigmoid(g) * u).astype(BF16)
    acc_scr[...] += _dot(act, wd_ref[...])

    @pl.when(j == pl.num_programs(2) - 1)
    def _():
        o_ref[0] = x_ref[0] + (1.0 + gt_ref[0]) * acc_scr[...]


def _ffn_dense(x, sh, sc, gt, ng, w_gu, w_d, tm, tf):
    b, t, d = x.shape
    f = w_d.shape[0]
    nf = f // tf
    tok = pl.BlockSpec((1, tm, d), lambda bi, ti, j: (bi, ti, 0))
    vec = pl.BlockSpec((1, 1, d), lambda bi, ti, j: (bi, 0, 0))
    return pl.pallas_call(
        _ffn_dense_kernel,
        grid=(b, t // tm, nf),
        in_specs=[tok, vec, vec, vec, pl.BlockSpec((1, d), lambda bi, ti, j: (0, 0)),
                  pl.BlockSpec((d, tf), lambda bi, ti, j: (0, j)),
                  pl.BlockSpec((d, tf), lambda bi, ti, j: (0, nf + j)),
                  pl.BlockSpec((tf, d), lambda bi, ti, j: (j, 0))],
        out_specs=tok,
        out_shape=jax.ShapeDtypeStruct((b, t, d), F32),
        scratch_shapes=[pltpu.VMEM((tm, d), BF16), pltpu.VMEM((tm, d), F32)],
        compiler_params=pltpu.CompilerParams(
            dimension_semantics=("arbitrary", "arbitrary", "arbitrary"),
            vmem_limit_bytes=VMEM_LIMIT),
        name="ffn_dense",
    )(x, sh[:, None, :], sc[:, None, :], gt[:, None, :], ng[None, :],
      w_gu.astype(BF16), w_gu.astype(BF16), w_d.astype(BF16))


def _lru_kernel(x_ref, sh_ref, sc_ref, gt_ref, ng_ref, win_ref, cw_ref, cb_ref, wg_ref, bg_ref,
                lam_ref, wout_ref, o_ref, xbuf, abuf, bbuf, carry):
    t = pl.program_id(1)
    tm = x_ref.shape[1]
    w = win_ref.shape[1] // 2
    nblk = w // LRU_BLOCK
    pad = tm

    @pl.when(t == 0)
    def _():
        xbuf[0:8, :] = jnp.zeros((8, w), F32)
        carry[...] = jnp.zeros_like(carry)
        for s in range(2):
            abuf[s, 0:pad, :] = jnp.ones((pad, w), F32)
            bbuf[s, 0:pad, :] = jnp.zeros((pad, w), F32)

    x = x_ref[0]
    h = _modulate(x, ng_ref[...], sh_ref[0], sc_ref[0]).astype(BF16)
    xg = _dot(h, win_ref[...])
    xb = xg[:, :w]
    gb = xg[:, w:]
    gate = 0.5 * gb * (1.0 + jnp.tanh(0.7978845608028654 * (gb + 0.044715 * gb * gb * gb)))

    xbuf[8:8 + tm, :] = xb
    conv = cb_ref[...] + cw_ref[CONV_WIDTH - 1:CONV_WIDTH, :] * xb
    for jj in range(CONV_WIDTH - 1):
        shift = CONV_WIDTH - 1 - jj
        conv = conv + cw_ref[jj:jj + 1, :] * xbuf[8 - shift:8 - shift + tm, :]
    xbuf[0:8, :] = xb[tm - 8:tm, :]

    conv_b = conv.astype(BF16)
    rs, is_ = [], []
    for n in range(nblk):
        gts = _dot(conv_b[:, n * LRU_BLOCK:(n + 1) * LRU_BLOCK], wg_ref[n]) + bg_ref[n]
        gts = jax.nn.sigmoid(gts)
        rs.append(gts[:, :LRU_BLOCK])
        is_.append(gts[:, LRU_BLOCK:])
    r_t = jnp.concatenate(rs, axis=1)
    i_t = jnp.concatenate(is_, axis=1)

    log_a = -LRU_C * r_t * _softplus(-lam_ref[...])
    a_t = jnp.exp(log_a)
    b_t = jnp.sqrt(-jnp.tanh(log_a) * (a_t * a_t + 1.0)) * (i_t * conv)

    abuf[0, pad:pad + tm, :] = a_t
    bbuf[0, pad:pad + tm, :] = b_t
    step = 1
    src = 0
    while step < tm:
        a_cur = abuf[src, pad:pad + tm, :]
        b_cur = bbuf[src, pad:pad + tm, :]
        a_sh = abuf[src, pad - step:pad - step + tm, :]
        b_sh = bbuf[src, pad - step:pad - step + tm, :]
        abuf[1 - src, pad:pad + tm, :] = a_cur * a_sh
        bbuf[1 - src, pad:pad + tm, :] = a_cur * b_sh + b_cur
        src = 1 - src
        step *= 2
    hs = bbuf[src, pad:pad + tm, :] + abuf[src, pad:pad + tm, :] * carry[0:1, :]
    carry[...] = jnp.broadcast_to(hs[tm - 1:tm, :], carry.shape)

    y = _dot((hs * gate).astype(BF16), wout_ref[...])
    o_ref[0] = x + (1.0 + gt_ref[0]) * y


def _lru_block(x, sh, sc, gt, ng, w_in, conv_w, conv_b, w_gates, b_gates, lam, w_out, tm):
    b, t, d = x.shape
    w = w_out.shape[0]
    nblk = w // LRU_BLOCK
    full = lambda *shape: pl.BlockSpec(shape, lambda bi, ti: (0,) * len(shape))
    tok = pl.BlockSpec((1, tm, d), lambda bi, ti: (bi, ti, 0))
    vec = pl.BlockSpec((1, 1, d), lambda bi, ti: (bi, 0, 0))
    return pl.pallas_call(
        _lru_kernel,
        grid=(b, t // tm),
        in_specs=[tok, vec, vec, vec, full(1, d), full(d, 2 * w), full(CONV_WIDTH, w), full(1, w),
                  full(nblk, LRU_BLOCK, 2 * LRU_BLOCK), full(nblk, 1, 2 * LRU_BLOCK),
                  full(1, w), full(w, d)],
        out_specs=tok,
        out_shape=jax.ShapeDtypeStruct((b, t, d), F32),
        scratch_shapes=[pltpu.VMEM((tm + 8, w), F32), pltpu.VMEM((2, 2 * tm, w), F32),
                        pltpu.VMEM((2, 2 * tm, w), F32), pltpu.VMEM((8, w), F32)],
        compiler_params=pltpu.CompilerParams(
            dimension_semantics=("arbitrary", "arbitrary"), vmem_limit_bytes=VMEM_LIMIT),
        name="rglru_block",
    )(x, sh[:, None, :], sc[:, None, :], gt[:, None, :], ng[None, :], w_in.astype(BF16),
      conv_w, conv_b[None, :], w_gates.astype(BF16), b_gates[:, None, :], lam[None, :],
      w_out.astype(BF16))


def _router_kernel(x_ref, sh_ref, sc_ref, ng_ref, wr_ref, br_ref,
                   h_out, eid_out, rank_out, prob_out, cnt_out, cnt_scr):
    i = pl.program_id(0)
    tm = x_ref.shape[0]

    @pl.when(i == 0)
    def _():
        cnt_scr[...] = jnp.zeros_like(cnt_scr)

    h = _modulate(x_ref[...], ng_ref[...], sh_ref[0], sc_ref[0])
    h_out[...] = h
    h_hi, h_lo = _split_bf16(h)
    w_hi, w_lo = _split_bf16(wr_ref[...])
    logits = _dot(h_hi, w_hi) + _dot(h_lo, w_hi) + _dot(h_hi, w_lo) + br_ref[...]

    lane = lax.broadcasted_iota(I32, logits.shape, 1)
    m1 = jnp.max(logits, axis=-1, keepdims=True)
    i1 = jnp.min(jnp.where(logits == m1, lane, LANES), axis=-1, keepdims=True)
    l2 = jnp.where(lane == i1, -jnp.inf, logits)
    m2 = jnp.max(l2, axis=-1, keepdims=True)
    i2 = jnp.min(jnp.where(l2 == m2, lane, LANES), axis=-1, keepdims=True)
    e = jnp.exp(m2 - m1)
    p1 = 1.0 / (1.0 + e)
    p2 = e / (1.0 + e)

    oh1 = (lane == i1).astype(F32)
    oh2 = (lane == i2).astype(F32)
    oh = oh1 + oh2
    rr = lax.broadcasted_iota(I32, (tm, tm), 0)
    cc = lax.broadcasted_iota(I32, (tm, tm), 1)
    tri = (cc < rr).astype(BF16)
    before = _dot(tri, oh.astype(BF16)) + cnt_scr[0:1, :]
    rank1 = jnp.sum(before * oh1, axis=-1, keepdims=True)
    rank2 = jnp.sum(before * oh2, axis=-1, keepdims=True)
    cnt_scr[...] = cnt_scr[...] + jnp.sum(oh, axis=0, keepdims=True)

    eid_out[:, 0:1] = i1
    eid_out[:, 1:2] = i2
    rank_out[:, 0:1] = rank1.astype(I32)
    rank_out[:, 1:2] = rank2.astype(I32)
    prob_out[:, 0:1] = p1
    prob_out[:, 1:2] = p2
    cnt_out[...] = cnt_scr[...].astype(I32)


def _router(x2d, sh, sc, ng, w_router, b_router, tiles_per_batch, tm):
    n, d = x2d.shape
    e = w_router.shape[1]
    wr = _pad_to(w_router, 1, LANES)
    br = jnp.concatenate([b_router, jnp.full((LANES - e,), -1e30, F32)])[None, :]
    tok = pl.BlockSpec((tm, d), lambda i: (i, 0))
    vec = pl.BlockSpec((1, 1, d), lambda i: (i // tiles_per_batch, 0, 0))
    two = pl.BlockSpec((tm, TOP_K), lambda i: (i, 0))
    return pl.pallas_call(
        _router_kernel,
        grid=(n // tm,),
        in_specs=[tok, vec, vec, pl.BlockSpec((1, d), lambda i: (0, 0)),
                  pl.BlockSpec((d, LANES), lambda i: (0, 0)), pl.BlockSpec((1, LANES), lambda i: (0, 0))],
        out_specs=[tok, two, two, two, pl.BlockSpec((8, LANES), lambda i: (0, 0))],
        out_shape=[jax.ShapeDtypeStruct((n, d), F32), jax.ShapeDtypeStruct((n, TOP_K), I32),
                   jax.ShapeDtypeStruct((n, TOP_K), I32), jax.ShapeDtypeStruct((n, TOP_K), F32),
                   jax.ShapeDtypeStruct((8, LANES), I32)],
        scratch_shapes=[pltpu.VMEM((8, LANES), F32)],
        compiler_params=pltpu.CompilerParams(
            dimension_semantics=("arbitrary",), vmem_limit_bytes=VMEM_LIMIT),
        name="moe_router",
    )(x2d, sh[:, None, :], sc[:, None, :], ng[None, :], wr, br)


def _dispatch_kernel(pos_ref, h_ref, xs_in, xs_out, sem, *, tb):
    del xs_in

    def issue(t, c):
        for kk in range(TOP_K):
            pltpu.make_async_copy(h_ref.at[pl.ds(t, 1)],
                                  xs_out.at[pl.ds(pos_ref[TOP_K * t + kk], 1)], sem).start()
        return c

    lax.fori_loop(0, tb, issue, 0)
    for kk in range(TOP_K):
        pltpu.make_async_copy(h_ref, xs_out.at[pl.ds(0, tb)], sem).wait()


def _dispatch(pos_flat, h2d, rows, tb):
    n, d = h2d.shape
    xs0 = jnp.zeros((rows, d), h2d.dtype)
    return pl.pallas_call(
        functools.partial(_dispatch_kernel, tb=tb),
        grid=(n // tb,),
        in_specs=[pl.BlockSpec((TOP_K * tb,), lambda i: (i,), memory_space=pltpu.SMEM),
                  pl.BlockSpec((tb, d), lambda i: (i, 0)), pl.BlockSpec(memory_space=pl.ANY)],
        out_specs=pl.BlockSpec(memory_space=pl.ANY),
        out_shape=jax.ShapeDtypeStruct((rows, d), h2d.dtype),
        scratch_shapes=[pltpu.SemaphoreType.DMA(())],
        input_output_aliases={2: 0},
        compiler_params=pltpu.CompilerParams(dimension_semantics=("arbitrary",)),
        name="moe_dispatch",
    )(pos_flat, h2d, xs0)


def _ffn_moe_kernel(gid_ref, valid_ref, xs_ref, wg_ref, wu_ref, wd_ref, o_ref, acc_scr):
    del gid_ref
    i = pl.program_id(0)
    j = pl.program_id(1)
    last = pl.num_programs(1) - 1
    valid = valid_ref[i] > 0

    @pl.when(valid)
    def _():
        @pl.when(j == 0)
        def _():
            acc_scr[...] = jnp.zeros_like(acc_scr)

        h = xs_ref[...].astype(BF16)
        g = _dot(h, wg_ref[0])
        u = _dot(h, wu_ref[0])
        act = (g * jax.nn.sigmoid(g) * u).astype(BF16)
        acc_scr[...] += _dot(act, wd_ref[0])

        @pl.when(j == last)
        def _():
            o_ref[...] = acc_scr[...]

    @pl.when(jnp.logical_not(valid) & (j == last))
    def _():
        o_ref[...] = jnp.zeros_like(o_ref)


def _ffn_moe(gid, valid, xs, w_gu, w_d, tm, tf):
    rows, d = xs.shape
    f = w_d.shape[1]
    nf = f // tf
    grid_spec = pltpu.PrefetchScalarGridSpec(
        num_scalar_prefetch=2,
        grid=(rows // tm, nf),
        in_specs=[pl.BlockSpec((tm, d), lambda i, j, gid, vld: (i, 0)),
                  pl.BlockSpec((1, d, tf), lambda i, j, gid, vld: (gid[i], 0, j)),
                  pl.BlockSpec((1, d, tf), lambda i, j, gid, vld: (gid[i], 0, nf + j)),
                  pl.BlockSpec((1, tf, d), lambda i, j, gid, vld: (gid[i], j, 0))],
        out_specs=pl.BlockSpec((tm, d), lambda i, j, gid, vld: (i, 0)),
        scratch_shapes=[pltpu.VMEM((tm, d), F32)],
    )
    w_gu_b = w_gu.astype(BF16)
    return pl.pallas_call(
        _ffn_moe_kernel,
        grid_spec=grid_spec,
        out_shape=jax.ShapeDtypeStruct((rows, d), F32),
        compiler_params=pltpu.CompilerParams(
            dimension_semantics=("arbitrary", "arbitrary"), vmem_limit_bytes=VMEM_LIMIT),
        name="ffn_moe",
    )(gid, valid, xs, w_gu_b, w_gu_b, w_d.astype(BF16))


def _combine_kernel(pos_ref, ys_hbm, x_ref, gt_ref, prob_ref, fg_ref, o_ref, ybuf, sem, *, tc):
    def row_copy(t, kk):
        return pltpu.make_async_copy(
            ys_hbm.at[pl.ds(pos_ref[TOP_K * t + kk], 1)], ybuf.at[kk, pl.ds(t, 1)], sem)

    def issue(t, c):
        for kk in range(TOP_K):
            row_copy(t, kk).start()
        return c

    lax.fori_loop(0, tc, issue, 0)
    for kk in range(TOP_K):
        pltpu.make_async_copy(ys_hbm.at[pl.ds(0, tc)], ybuf.at[kk], sem).wait()

    y = prob_ref[:, 0:1] * ybuf[0] + prob_ref[:, 1:2] * ybuf[1]
    x = x_ref[...] + (1.0 + gt_ref[0]) * y
    ms = jnp.mean(x * x, axis=-1, keepdims=True)
    o_ref[...] = x * lax.rsqrt(ms + NORM_EPS) * fg_ref[...]


def _combine(pos_flat, ys, x2d, gt, prob, final_g, tiles_per_batch, tc):
    n, d = x2d.shape
    tok = pl.BlockSpec((tc, d), lambda i: (i, 0))
    return pl.pallas_call(
        functools.partial(_combine_kernel, tc=tc),
        grid=(n // tc,),
        in_specs=[pl.BlockSpec((TOP_K * tc,), lambda i: (i,), memory_space=pltpu.SMEM),
                  pl.BlockSpec(memory_space=pl.ANY), tok,
                  pl.BlockSpec((1, 1, d), lambda i: (i // tiles_per_batch, 0, 0)),
                  pl.BlockSpec((tc, TOP_K), lambda i: (i, 0)),
                  pl.BlockSpec((1, d), lambda i: (0, 0))],
        out_specs=tok,
        out_shape=jax.ShapeDtypeStruct((n, d), F32),
        scratch_shapes=[pltpu.VMEM((TOP_K, tc, d), F32), pltpu.SemaphoreType.DMA(())],
        compiler_params=pltpu.CompilerParams(
            dimension_semantics=("arbitrary",), vmem_limit_bytes=VMEM_LIMIT),
        name="moe_combine",
    )(pos_flat, ys, x2d, gt[:, None, :], prob, final_g[None, :])


def _moe_layout(eid, rank, counts, n_experts, tm, n_tiles):
    cnt = counts[0, :n_experts]
    tiles = (cnt + tm - 1) // tm
    tile_end = jnp.cumsum(tiles)
    offsets = (tile_end - tiles) * tm
    pos = offsets[eid] + rank
    tile_ids = jnp.arange(n_tiles, dtype=I32)
    total = tile_end[-1]
    last_id = jnp.minimum(tile_ids, total - 1)
    gid = jnp.sum((last_id[:, None] >= tile_end[None, :]).astype(I32), axis=1)
    valid = (tile_ids < total).astype(I32)
    return pos.reshape(-1).astype(I32), gid, valid


def kernel(x, c, ada_w, ada_b, norm_g, final_g, rwkv_mu, rwkv_w_rkv, rwkv_w_o, rwkv_w0, rwkv_w1, rwkv_w2, rwkv_a0, rwkv_a1, rwkv_a2, rwkv_g1, rwkv_g2, rwkv_k_k, rwkv_k_a, rwkv_r_k, rwkv_gn_w, rwkv_gn_b, lru_w_in, lru_conv_w, lru_conv_b, lru_w_gates, lru_b_gates, lru_lam, lru_w_out, ffn_w_gu, ffn_w_d, moe_w_router, moe_b_router, moe_w_gu, moe_w_d):
    b, t, d = x.shape
    n = b * t
    n_experts = moe_w_router.shape[-1]
    mod = _ada_mod(c, ada_w, ada_b)

    def mods(i):
        return [mod[i, :, q * d:(q + 1) * d] for q in range(6)]

    sh1, sc1, gt1, sh2, sc2, gt2 = mods(0)
    r, lw, k2, v, kk, bv, g, bonus = _rwkv_pre(
        x, sh1, sc1, norm_g[0, 0], rwkv_mu[0], rwkv_w_rkv[0], rwkv_w1[0], rwkv_w2[0],
        rwkv_a1[0], rwkv_a2[0], rwkv_g1[0], rwkv_g2[0], rwkv_w0[0], rwkv_a0[0],
        rwkv_k_k[0], rwkv_k_a[0], rwkv_r_k[0], tm=min(256, t))
    yg = _wkv_scan(r, lw, k2, v, kk, bv, g, bonus, rwkv_gn_w[0], rwkv_gn_b[0],
                   chunks_per_step=min(2, t // CHUNK))
    x = _proj_residual(yg, rwkv_w_o[0], x, gt1, tm=min(512, t))
    x = _ffn_dense(x, sh2, sc2, gt2, norm_g[0, 1], ffn_w_gu[0], ffn_w_d[0], tm=min(1024, t), tf=512)

    sh1, sc1, gt1, sh2, sc2, gt2 = mods(1)
    x = _lru_block(x, sh1, sc1, gt1, norm_g[1, 0], lru_w_in[0], lru_conv_w[0], lru_conv_b[0],
                   lru_w_gates[0], lru_b_gates[0], lru_lam[0], lru_w_out[0], tm=min(256, t))

    x2d = x.reshape(n, d)
    tm_r = min(512, t)
    h2, eid, rank, prob, counts = _router(x2d, sh2, sc2, norm_g[1, 1], moe_w_router[0],
                                          moe_b_router[0], t // tm_r, tm_r)
    tm_g = min(512, t)
    n_tiles = (TOP_K * n) // tm_g + n_experts
    pos, gid, valid = _moe_layout(eid, rank, counts, n_experts, tm_g, n_tiles)
    xs = _dispatch(pos, h2, n_tiles * tm_g, tb=min(512, n))
    ys = _ffn_moe(gid, valid, xs, moe_w_gu[0], moe_w_d[0], tm_g, tf=512)
    tc = min(512, t)
    out = _combine(pos, ys, x2d, gt2, prob, final_g, t // tc, tc)
    return out.reshape(b, t, d)
```

```python
import functools

import jax
import jax.numpy as jnp
from jax import lax
from jax.experimental import pallas as pl
from jax.experimental.pallas import tpu as pltpu

F32 = jnp.float32
BF16 = jnp.bfloat16
I32 = jnp.int32

HEAD = 64
CHUNK = 64
GROUP = 256
HEADS_PER_GROUP = GROUP // HEAD
GN_EPS = 64e-5
NORM_EPS = 1e-6
LRU_C = 8.0
CONV_WIDTH = 4
LRU_BLOCK = 256
TOP_K = 2
LANES = 128
VMEM_LIMIT = 56 * 1024 * 1024


def _dot(a, b):
    return jnp.dot(a, b, preferred_element_type=F32)


def _dot_nt(a, b):
    return lax.dot_general(a, b, (((1,), (1,)), ((), ())), preferred_element_type=F32)


def _dot_tn(a, b):
    return lax.dot_general(a, b, (((0,), (0,)), ((), ())), preferred_element_type=F32)


def _softplus(u):
    return jnp.maximum(u, 0.0) + jnp.log1p(jnp.exp(-jnp.abs(u)))


def _modulate(x, ng, sh, sc):
    ms = jnp.mean(x * x, axis=-1, keepdims=True)
    return x * lax.rsqrt(ms + NORM_EPS) * ng * (1.0 + sc) + sh


def _split_bf16(x):
    hi = x.astype(BF16)
    lo = (x - hi.astype(F32)).astype(BF16)
    return hi, lo


def _mod_kernel(c_ref, w_ref, b_ref, o_ref):
    c = c_ref[...]
    cond = c * jax.nn.sigmoid(c)
    o_ref[0] = _dot(cond.astype(BF16), w_ref[0].astype(BF16)) + b_ref[0]


def _ada_mod(c, ada_w, ada_b):
    depth, d, d6 = ada_w.shape
    b = c.shape[0]
    rows = 8
    c8 = jnp.pad(c, ((0, rows - b), (0, 0)))
    tn = 1024
    out = pl.pallas_call(
        _mod_kernel,
        grid=(depth, d6 // tn),
        in_specs=[
            pl.BlockSpec((rows, d), lambda i, j: (0, 0)),
            pl.BlockSpec((1, d, tn), lambda i, j: (i, 0, j)),
            pl.BlockSpec((1, 1, tn), lambda i, j: (i, 0, j)),
        ],
        out_specs=pl.BlockSpec((1, rows, tn), lambda i, j: (i, 0, j)),
        out_shape=jax.ShapeDtypeStruct((depth, rows, d6), F32),
        name="ada_mod",
    )(c8, ada_w, ada_b.reshape(depth, 1, d6))
    return out[:, :b]


def _rwkv_pre_kernel(x_ref, sh_ref, sc_ref, ng_ref, mu_ref, wrkv_ref, w1_ref, w2_ref,
                     a1_ref, a2_ref, g1_ref, g2_ref, vec_ref, seg_ref, segt_ref,
                     r_out, lw_out, k_out, v_out, kk_out, b_out, g_out, bonus_out,
                     hbuf):
    t = pl.program_id(1)
    tm = x_ref.shape[1]
    d = x_ref.shape[2]

    h = _modulate(x_ref[0], ng_ref[...], sh_ref[0], sc_ref[0])

    @pl.when(t == 0)
    def _():
        hbuf[...] = jnp.zeros((8, d), F32)

    rr = lax.broadcasted_iota(I32, (tm, tm), 0)
    cc = lax.broadcasted_iota(I32, (tm, tm), 1)
    hprev = _dot((cc == rr - 1).astype(BF16), h.astype(BF16))
    first = lax.broadcasted_iota(I32, (tm, d), 0) == 0
    hprev = jnp.where(first, hbuf[7:8, :], hprev)
    hbuf[...] = h[tm - 8:tm, :]
    xx = hprev - h

    def mix(p):
        return (h + xx * mu_ref[p:p + 1, :]).astype(BF16)

    r = _dot(mix(0), wrkv_ref[0])
    k = _dot(mix(1), wrkv_ref[1])
    v = _dot(mix(2), wrkv_ref[2])
    wl = _dot(jnp.tanh(_dot(mix(3), w1_ref[...])).astype(BF16), w2_ref[...])
    al = _dot(_dot(mix(4), a1_ref[...]).astype(BF16), a2_ref[...])
    g = _dot(jax.nn.sigmoid(_dot(mix(5), g1_ref[...])).astype(BF16), g2_ref[...])

    w0 = vec_ref[0:1, :]
    a0 = vec_ref[1:2, :]
    k_k = vec_ref[2:3, :]
    k_a = vec_ref[3:4, :]
    r_k = vec_ref[4:5, :]

    def headsum(z):
        s = _dot(z.astype(BF16), seg_ref[...])
        s_hi, s_lo = _split_bf16(s)
        return _dot(jnp.concatenate([s_hi, s_lo], axis=1), segt_ref[...])

    lw = -0.6065306597126334 * jax.nn.sigmoid(w0 + wl)
    a = jax.nn.sigmoid(a0 + al)
    kk = k * k_k
    kk = kk * lax.rsqrt(jnp.maximum(headsum(kk * kk), 1e-24))
    k2 = k * (1.0 + (a - 1.0) * k_a)
    bonus = headsum(r * k2 * r_k) * v

    r_out[0] = r
    lw_out[0] = lw
    k_out[0] = k2
    v_out[0] = v
    kk_out[0] = kk
    b_out[0] = kk * a
    g_out[0] = g
    bonus_out[0] = bonus


def _pad_to(x, axis, size):
    pad = [(0, 0)] * x.ndim
    pad[axis] = (0, size - x.shape[axis])
    return jnp.pad(x, pad)


def _rwkv_pre(x, sh, sc, ng, mu, w_rkv, w1, w2, a1, a2, g1, g2, w0, a0, k_k, k_a, r_k, tm):
    b, t, d = x.shape
    nh = d // HEAD
    lw_pad = LANES * pl.cdiv(w1.shape[1], LANES)
    la_pad = LANES * pl.cdiv(a1.shape[1], LANES)
    lg_pad = LANES * pl.cdiv(g1.shape[1], LANES)
    vecs = _pad_to(jnp.stack([w0, a0, k_k, k_a, r_k.reshape(d)]), 0, 8)
    head_of_lane = jnp.arange(d) // HEAD
    seg = (head_of_lane[:, None] == jnp.arange(LANES)[None, :]).astype(BF16)
    segt = jnp.concatenate([seg.T, seg.T], axis=0)
    del nh
    full = lambda *shape: pl.BlockSpec(shape, lambda bi, ti: (0,) * len(shape))
    tok = pl.BlockSpec((1, tm, d), lambda bi, ti: (bi, ti, 0))
    vec = pl.BlockSpec((1, 1, d), lambda bi, ti: (bi, 0, 0))
    outs = pl.pallas_call(
        _rwkv_pre_kernel,
        grid=(b, t // tm),
        in_specs=[tok, vec, vec, full(1, d), full(8, d), full(3, d, d),
                  full(d, lw_pad), full(lw_pad, d), full(d, la_pad), full(la_pad, d),
                  full(d, lg_pad), full(lg_pad, d), full(8, d), full(d, LANES), full(2 * LANES, d)],
        out_specs=[tok] * 8,
        out_shape=[jax.ShapeDtypeStruct((b, t, d), F32)] * 8,
        scratch_shapes=[pltpu.VMEM((8, d), F32)],
        compiler_params=pltpu.CompilerParams(
            dimension_semantics=("arbitrary", "arbitrary"), vmem_limit_bytes=VMEM_LIMIT),
        name="rwkv_pre",
    )(x, sh[:, None, :], sc[:, None, :], ng[None, :], _pad_to(mu, 0, 8), w_rkv.astype(BF16),
      _pad_to(w1, 1, lw_pad).astype(BF16), _pad_to(w2, 0, lw_pad).astype(BF16),
      _pad_to(a1, 1, la_pad).astype(BF16), _pad_to(a2, 0, la_pad).astype(BF16),
      _pad_to(g1, 1, lg_pad).astype(BF16), _pad_to(g2, 0, lg_pad).astype(BF16),
      vecs, seg, segt)
    return outs


def _wkv_scan_kernel(r_ref, lw_ref, k_ref, v_ref, kk_ref, b_ref, g_ref, bonus_ref,
                     gnw_ref, gnb_ref, o_ref, h_scr, *, chunks_per_step):
    L = CHUNK
    W = GROUP
    ngroups = r_ref.shape[2] // W

    @pl.when(pl.program_id(1) == 0)
    def _():
        h_scr[...] = jnp.zeros_like(h_scr)

    row = lax.broadcasted_iota(I32, (L, W), 0)
    lane = lax.broadcasted_iota(I32, (L, W), 1)
    sidx = lane & (L - 1)
    lane_head = lane >> 6
    strict = sidx < row
    incl = sidx <= row
    eye = (sidx == row).astype(F32)
    same16 = (row >> 4) == (sidx >> 4)
    same32 = (row >> 5) == (sidx >> 5)
    m16 = strict & same16
    m32 = strict & same32 & jnp.logical_not(same16)
    m64 = strict & jnp.logical_not(same32)
    rb = lax.broadcasted_iota(I32, (W, W), 0)
    cb = lax.broadcasted_iota(I32, (W, W), 1)
    bmask = (rb >> 6) == (cb >> 6)
    diag = rb == cb
    ones_bd = bmask.astype(BF16)
    tri_r = lax.broadcasted_iota(I32, (L, 3 * L), 0)
    tri_c = lax.broadcasted_iota(I32, (L, 3 * L), 1)
    tri3 = ((tri_c & (L - 1)) <= tri_r).astype(BF16)

    def bd(y):
        yt = jnp.concatenate([y] * HEADS_PER_GROUP, axis=0)
        return jnp.where(bmask, yt, 0.0).astype(BF16)

    def hmm(x, ybd):
        return _dot(x.astype(BF16), ybd)

    streams = [(q, j) for j in range(chunks_per_step) for q in range(ngroups)]
    S = range(len(streams))

    def ld(ref, s):
        q, j = streams[s]
        return ref[0, j * L:(j + 1) * L, q * W:(q + 1) * W]

    r = [ld(r_ref, s) for s in S]
    lw = [ld(lw_ref, s) for s in S]
    k = [ld(k_ref, s) for s in S]
    v = [ld(v_ref, s) for s in S]
    kk = [ld(kk_ref, s) for s in S]
    bv = [ld(b_ref, s) for s in S]

    def cumlog(x):
        hi = x.astype(BF16)
        rem = x - hi.astype(F32)
        mid = rem.astype(BF16)
        lo = (rem - mid.astype(F32)).astype(BF16)
        return _dot(tri3, jnp.concatenate([hi, mid, lo], axis=0))

    cl = [cumlog(lw[s]) for s in S]
    cl_last = [cl[s][L - 1:L, :] for s in S]
    e_pos = [jnp.exp(cl[s]) for s in S]
    e_neg = [jnp.exp(-cl[s]) for s in S]
    e_end = [jnp.exp(cl_last[s] - cl[s]) for s in S]
    rt = [r[s] * e_pos[s] for s in S]
    at = [-kk[s] * jnp.exp(cl[s] - lw[s]) for s in S]
    bt = [bv[s] * e_neg[s] for s in S]
    kt = [k[s] * e_neg[s] for s in S]

    def gram(s):
        x = jnp.concatenate([at[s], rt[s]], axis=0).astype(BF16)
        ys = [jnp.where(lane_head == hh, bt[s], 0.0) for hh in range(HEADS_PER_GROUP)]
        ys += [jnp.where(lane_head == hh, kt[s], 0.0) for hh in range(HEADS_PER_GROUP)]
        return _dot_nt(x, jnp.concatenate(ys, axis=0).astype(BF16))

    gm = [gram(s) for s in S]
    a_ab = [jnp.where(strict, gm[s][:L, :W], 0.0) for s in S]
    a_ak = [jnp.where(strict, gm[s][:L, W:], 0.0) for s in S]
    a_rb = [jnp.where(incl, gm[s][L:, :W], 0.0) for s in S]
    a_rk = [jnp.where(incl, gm[s][L:, W:], 0.0) for s in S]

    a0 = [jnp.where(m16, a_ab[s], 0.0) for s in S]
    pw = [hmm(a0[s], bd(a0[s])) for s in S]
    tinv = [eye + a0[s] for s in S]
    for _ in range(2):
        ts = [_dot(jnp.concatenate([tinv[s], pw[s]], axis=0).astype(BF16), bd(pw[s])) for s in S]
        tinv = [tinv[s] + ts[s][:L] for s in S]
        pw = [ts[s][L:] for s in S]
    tinv = [tinv[s] + hmm(tinv[s], bd(pw[s])) for s in S]
    for msk in (m32, m64):
        inner = [hmm(jnp.where(msk, a_ab[s], 0.0), bd(tinv[s])) for s in S]
        tinv = [tinv[s] + hmm(tinv[s], bd(inner[s])) for s in S]

    vbd = [bd(v[s]) for s in S]
    av = [hmm(a_ak[s], vbd[s]) for s in S]
    tx = [_dot(tinv[s].astype(BF16), jnp.concatenate([bd(at[s]), bd(av[s])], axis=1)) for s in S]
    ahat = [tx[s][:, :W] for s in S]
    vp = [tx[s][:, W:] for s in S]
    ox = [_dot(a_rb[s].astype(BF16), jnp.concatenate([bd(ahat[s]), bd(vp[s])], axis=1)) for s in S]
    rhat = [rt[s] + ox[s][:, :W] for s in S]
    o_intra = [ox[s][:, W:] + hmm(a_rk[s], vbd[s]) for s in S]

    def state_terms(s):
        z = jnp.concatenate([bv[s] * e_end[s], k[s] * e_end[s]], axis=0).astype(BF16)
        wm = jnp.concatenate(
            [jnp.concatenate([ahat[s], vp[s]], axis=1),
             jnp.concatenate([jnp.zeros((L, W), F32), v[s]], axis=1)], axis=0).astype(BF16)
        mn = _dot_tn(z, wm)
        m_mat = jnp.where(bmask, mn[:, :W], 0.0) + jnp.where(diag, jnp.exp(cl_last[s]), 0.0)
        return m_mat, jnp.where(bmask, mn[:, W:], 0.0)

    mn = [state_terms(s) for s in S]

    o = [None] * len(streams)
    for q in range(ngroups):
        h = h_scr[q]
        for j in range(chunks_per_step):
            s = streams.index((q, j))
            m_hi, m_lo = _split_bf16(mn[s][0])
            lhs = jnp.concatenate([m_hi, m_lo, rhat[s].astype(BF16)], axis=0)
            res = _dot(lhs, h.astype(BF16))
            o[s] = res[2 * W:] + o_intra[s]
            h = res[:W] + res[W:2 * W] + mn[s][1]
        h_scr[q] = h

    def headmean(zs):
        parts = []
        for z in zs:
            parts += list(_split_bf16(z))
        red = _dot(jnp.concatenate(parts, axis=0), ones_bd) * (1.0 / HEAD)
        return [red[2 * L * s:2 * L * s + L] + red[2 * L * s + L:2 * L * (s + 1)] for s in S]

    mean = headmean(o)
    dlt = [o[s] - mean[s] for s in S]
    var = headmean([dlt[s] * dlt[s] for s in S])
    for s in S:
        q, j = streams[s]
        gsl = slice(q * W, (q + 1) * W)
        rsl = slice(j * L, (j + 1) * L)
        yn = dlt[s] * lax.rsqrt(var[s] + GN_EPS) * gnw_ref[:, gsl] + gnb_ref[:, gsl]
        o_ref[0, rsl, gsl] = ((yn + bonus_ref[0, rsl, gsl]) * g_ref[0, rsl, gsl]).astype(o_ref.dtype)


def _wkv_scan(r, lw, k, v, kk, bv, g, bonus, gn_w, gn_b, chunks_per_step):
    b, t, d = r.shape
    lb = CHUNK * chunks_per_step
    tok = pl.BlockSpec((1, lb, d), lambda bi, ci: (bi, ci, 0))
    vec = pl.BlockSpec((1, d), lambda bi, ci: (0, 0))
    return pl.pallas_call(
        functools.partial(_wkv_scan_kernel, chunks_per_step=chunks_per_step),
        grid=(b, t // lb),
        in_specs=[tok] * 8 + [vec, vec],
        out_specs=tok,
        out_shape=jax.ShapeDtypeStruct((b, t, d), BF16),
        scratch_shapes=[pltpu.VMEM((d // GROUP, GROUP, GROUP), F32)],
        compiler_params=pltpu.CompilerParams(
            dimension_semantics=("arbitrary", "arbitrary"), vmem_limit_bytes=VMEM_LIMIT),
        name="wkv_scan",
    )(r, lw, k, v, kk, bv, g, bonus, gn_w[None, :], gn_b[None, :])


def _swiglu(h, wg, wu, wd):
    g = _dot(h, wg)
    u = _dot(h, wu)
    return _dot((g * jax.nn.sigmoid(g) * u).astype(BF16), wd)


def _ffn_dense_kernel(a_ref, x_ref, gt1_ref, sh_ref, sc_ref, gt2_ref, ng_ref, wo_ref,
                      wg_ref, wu_ref, wd_ref, o_ref):
    x1 = x_ref[0] + (1.0 + gt1_ref[0]) * _dot(a_ref[0], wo_ref[...])
    h = _modulate(x1, ng_ref[...], sh_ref[0], sc_ref[0]).astype(BF16)
    o_ref[0] = x1 + (1.0 + gt2_ref[0]) * _swiglu(h, wg_ref[...], wu_ref[...], wd_ref[...])


def _resident(shape, index_map):
    return pl.BlockSpec(shape, index_map, pipeline_mode=pl.Buffered(1))


def _ffn_dense(a, x, gt1, sh, sc, gt2, ng, w_o, w_gu, w_d, tm):
    b, t, d = x.shape
    f = w_d.shape[0]
    tok = pl.BlockSpec((1, tm, d), lambda bi, ti: (bi, ti, 0))
    vec = pl.BlockSpec((1, 1, d), lambda bi, ti: (bi, 0, 0))
    w_gu_b = w_gu.astype(BF16)
    return pl.pallas_call(
        _ffn_dense_kernel,
        grid=(b, t // tm),
        in_specs=[tok, tok, vec, vec, vec, vec, pl.BlockSpec((1, d), lambda bi, ti: (0, 0)),
                  _resident((d, d), lambda bi, ti: (0, 0)),
                  _resident((d, f), lambda bi, ti: (0, 0)),
                  _resident((d, f), lambda bi, ti: (0, 1)),
                  _resident((f, d), lambda bi, ti: (0, 0))],
        out_specs=tok,
        out_shape=jax.ShapeDtypeStruct((b, t, d), F32),
        compiler_params=pltpu.CompilerParams(
            dimension_semantics=("arbitrary", "arbitrary"), vmem_limit_bytes=VMEM_LIMIT),
        name="ffn_dense",
    )(a, x, gt1[:, None, :], sh[:, None, :], sc[:, None, :], gt2[:, None, :], ng[None, :],
      w_o.astype(BF16), w_gu_b, w_gu_b, w_d.astype(BF16))


def _lru_kernel(x_ref, sh_ref, sc_ref, gt_ref, ng_ref, win_ref, cw_ref, cb_ref, wg_ref, bg_ref,
                lam_ref, wout_ref, o_ref, xbuf, abuf, bbuf, carry):
    t = pl.program_id(1)
    tm = x_ref.shape[1]
    w = win_ref.shape[1] // 2
    nblk = w // LRU_BLOCK
    pad = tm

    @pl.when(t == 0)
    def _():
        xbuf[0:8, :] = jnp.zeros((8, w), F32)
        carry[...] = jnp.zeros_like(carry)
        for s in range(2):
            abuf[s, 0:pad, :] = jnp.ones((pad, w), F32)
            bbuf[s, 0:pad, :] = jnp.zeros((pad, w), F32)

    x = x_ref[0]
    h = _modulate(x, ng_ref[...], sh_ref[0], sc_ref[0]).astype(BF16)
    xg = _dot(h, win_ref[...])
    xb = xg[:, :w]
    gb = xg[:, w:]
    gate = 0.5 * gb * (1.0 + jnp.tanh(0.7978845608028654 * (gb + 0.044715 * gb * gb * gb)))

    xbuf[8:8 + tm, :] = xb
    conv = cb_ref[...] + cw_ref[CONV_WIDTH - 1:CONV_WIDTH, :] * xb
    for jj in range(CONV_WIDTH - 1):
        shift = CONV_WIDTH - 1 - jj
        conv = conv + cw_ref[jj:jj + 1, :] * xbuf[8 - shift:8 - shift + tm, :]
    xbuf[0:8, :] = xb[tm - 8:tm, :]

    conv_b = conv.astype(BF16)
    rs, is_ = [], []
    for n in range(nblk):
        gts = _dot(conv_b[:, n * LRU_BLOCK:(n + 1) * LRU_BLOCK], wg_ref[n]) + bg_ref[n]
        gts = jax.nn.sigmoid(gts)
        rs.append(gts[:, :LRU_BLOCK])
        is_.append(gts[:, LRU_BLOCK:])
    r_t = jnp.concatenate(rs, axis=1)
    i_t = jnp.concatenate(is_, axis=1)

    log_a = -LRU_C * r_t * _softplus(-lam_ref[...])
    a_t = jnp.exp(log_a)
    b_t = jnp.sqrt(-jnp.tanh(log_a) * (a_t * a_t + 1.0)) * (i_t * conv)

    abuf[0, pad:pad + tm, :] = a_t
    bbuf[0, pad:pad + tm, :] = b_t
    step = 1
    src = 0
    while step < tm:
        a_cur = abuf[src, pad:pad + tm, :]
        b_cur = bbuf[src, pad:pad + tm, :]
        a_sh = abuf[src, pad - step:pad - step + tm, :]
        b_sh = bbuf[src, pad - step:pad - step + tm, :]
        abuf[1 - src, pad:pad + tm, :] = a_cur * a_sh
        bbuf[1 - src, pad:pad + tm, :] = a_cur * b_sh + b_cur
        src = 1 - src
        step *= 2
    hs = bbuf[src, pad:pad + tm, :] + abuf[src, pad:pad + tm, :] * carry[0:1, :]
    carry[...] = jnp.broadcast_to(hs[tm - 1:tm, :], carry.shape)

    y = _dot((hs * gate).astype(BF16), wout_ref[...])
    o_ref[0] = x + (1.0 + gt_ref[0]) * y


def _lru_block(x, sh, sc, gt, ng, w_in, conv_w, conv_b, w_gates, b_gates, lam, w_out, tm):
    b, t, d = x.shape
    w = w_out.shape[0]
    nblk = w // LRU_BLOCK
    full = lambda *shape: pl.BlockSpec(shape, lambda bi, ti: (0,) * len(shape))
    tok = pl.BlockSpec((1, tm, d), lambda bi, ti: (bi, ti, 0))
    vec = pl.BlockSpec((1, 1, d), lambda bi, ti: (bi, 0, 0))
    return pl.pallas_call(
        _lru_kernel,
        grid=(b, t // tm),
        in_specs=[tok, vec, vec, vec, full(1, d), full(d, 2 * w), full(CONV_WIDTH, w), full(1, w),
                  full(nblk, LRU_BLOCK, 2 * LRU_BLOCK), full(nblk, 1, 2 * LRU_BLOCK),
                  full(1, w), full(w, d)],
        out_specs=tok,
        out_shape=jax.ShapeDtypeStruct((b, t, d), F32),
        scratch_shapes=[pltpu.VMEM((tm + 8, w), F32), pltpu.VMEM((2, 2 * tm, w), F32),
                        pltpu.VMEM((2, 2 * tm, w), F32), pltpu.VMEM((8, w), F32)],
        compiler_params=pltpu.CompilerParams(
            dimension_semantics=("arbitrary", "arbitrary"), vmem_limit_bytes=VMEM_LIMIT),
        name="rglru_block",
    )(x, sh[:, None, :], sc[:, None, :], gt[:, None, :], ng[None, :], w_in.astype(BF16),
      conv_w, conv_b[None, :], w_gates.astype(BF16), b_gates[:, None, :], lam[None, :],
      w_out.astype(BF16))


def _router_kernel(x_ref, sh_ref, sc_ref, ng_ref, wr_ref, br_ref,
                   h_out, eid_out, rank_out, prob_out, cnt_out, cnt_scr):
    i = pl.program_id(0)
    tm = x_ref.shape[0]

    @pl.when(i == 0)
    def _():
        cnt_scr[...] = jnp.zeros_like(cnt_scr)

    h = _modulate(x_ref[...], ng_ref[...], sh_ref[0], sc_ref[0])
    h_out[...] = h
    h_hi, h_lo = _split_bf16(h)
    w_hi, w_lo = _split_bf16(wr_ref[...])
    logits = _dot(h_hi, w_hi) + _dot(h_lo, w_hi) + _dot(h_hi, w_lo) + br_ref[...]

    lane = lax.broadcasted_iota(I32, logits.shape, 1)
    m1 = jnp.max(logits, axis=-1, keepdims=True)
    i1 = jnp.min(jnp.where(logits == m1, lane, LANES), axis=-1, keepdims=True)
    l2 = jnp.where(lane == i1, -jnp.inf, logits)
    m2 = jnp.max(l2, axis=-1, keepdims=True)
    i2 = jnp.min(jnp.where(l2 == m2, lane, LANES), axis=-1, keepdims=True)
    e = jnp.exp(m2 - m1)
    p1 = 1.0 / (1.0 + e)
    p2 = e / (1.0 + e)

    oh1 = (lane == i1).astype(F32)
    oh2 = (lane == i2).astype(F32)
    oh = oh1 + oh2
    rr = lax.broadcasted_iota(I32, (tm, tm), 0)
    cc = lax.broadcasted_iota(I32, (tm, tm), 1)
    tri = (cc < rr).astype(BF16)
    before = _dot(tri, oh.astype(BF16)) + cnt_scr[0:1, :]
    rank1 = jnp.sum(before * oh1, axis=-1, keepdims=True)
    rank2 = jnp.sum(before * oh2, axis=-1, keepdims=True)
    cnt_scr[...] = cnt_scr[...] + jnp.sum(oh, axis=0, keepdims=True)

    eid_out[:, 0:1] = i1
    eid_out[:, 1:2] = i2
    rank_out[:, 0:1] = rank1.astype(I32)
    rank_out[:, 1:2] = rank2.astype(I32)
    prob_out[:, 0:1] = p1
    prob_out[:, 1:2] = p2
    cnt_out[...] = cnt_scr[...].astype(I32)


def _router(x2d, sh, sc, ng, w_router, b_router, tiles_per_batch, tm):
    n, d = x2d.shape
    e = w_router.shape[1]
    wr = _pad_to(w_router, 1, LANES)
    br = jnp.concatenate([b_router, jnp.full((LANES - e,), -1e30, F32)])[None, :]
    tok = pl.BlockSpec((tm, d), lambda i: (i, 0))
    vec = pl.BlockSpec((1, 1, d), lambda i: (i // tiles_per_batch, 0, 0))
    two = pl.BlockSpec((tm, TOP_K), lambda i: (i, 0))
    return pl.pallas_call(
        _router_kernel,
        grid=(n // tm,),
        in_specs=[tok, vec, vec, pl.BlockSpec((1, d), lambda i: (0, 0)),
                  pl.BlockSpec((d, LANES), lambda i: (0, 0)), pl.BlockSpec((1, LANES), lambda i: (0, 0))],
        out_specs=[tok, two, two, two, pl.BlockSpec((8, LANES), lambda i: (0, 0))],
        out_shape=[jax.ShapeDtypeStruct((n, d), F32), jax.ShapeDtypeStruct((n, TOP_K), I32),
                   jax.ShapeDtypeStruct((n, TOP_K), I32), jax.ShapeDtypeStruct((n, TOP_K), F32),
                   jax.ShapeDtypeStruct((8, LANES), I32)],
        scratch_shapes=[pltpu.VMEM((8, LANES), F32)],
        compiler_params=pltpu.CompilerParams(
            dimension_semantics=("arbitrary",), vmem_limit_bytes=VMEM_LIMIT),
        name="moe_router",
    )(x2d, sh[:, None, :], sc[:, None, :], ng[None, :], wr, br)


def _dispatch_kernel(pos_ref, h_ref, xs_in, xs_out, sem, *, tb):
    del xs_in

    for t in range(tb):
        for kk in range(TOP_K):
            pltpu.make_async_copy(h_ref.at[pl.ds(t, 1)],
                                  xs_out.at[pl.ds(pos_ref[TOP_K * t + kk], 1)], sem).start()
    for kk in range(TOP_K):
        pltpu.make_async_copy(h_ref, xs_out.at[pl.ds(0, tb)], sem).wait()


def _dispatch(pos_flat, h2d, rows, tb):
    n, d = h2d.shape
    xs0 = jnp.zeros((rows, d), h2d.dtype)
    return pl.pallas_call(
        functools.partial(_dispatch_kernel, tb=tb),
        grid=(n // tb,),
        in_specs=[pl.BlockSpec((TOP_K * tb,), lambda i: (i,), memory_space=pltpu.SMEM),
                  pl.BlockSpec((tb, d), lambda i: (i, 0)), pl.BlockSpec(memory_space=pl.ANY)],
        out_specs=pl.BlockSpec(memory_space=pl.ANY),
        out_shape=jax.ShapeDtypeStruct((rows, d), h2d.dtype),
        scratch_shapes=[pltpu.SemaphoreType.DMA(())],
        input_output_aliases={2: 0},
        compiler_params=pltpu.CompilerParams(dimension_semantics=("arbitrary",)),
        name="moe_dispatch",
    )(pos_flat, h2d, xs0)


def _ffn_moe_kernel(gid_ref, valid_ref, xs_ref, wg_ref, wu_ref, wd_ref, o_ref):
    del gid_ref
    valid = valid_ref[pl.program_id(0)] > 0

    @pl.when(valid)
    def _():
        o_ref[...] = _swiglu(xs_ref[...].astype(BF16), wg_ref[0], wu_ref[0], wd_ref[0])

    @pl.when(jnp.logical_not(valid))
    def _():
        o_ref[...] = jnp.zeros_like(o_ref)


def _ffn_moe(gid, valid, xs, w_gu, w_d, tm):
    rows, d = xs.shape
    f = w_d.shape[1]
    grid_spec = pltpu.PrefetchScalarGridSpec(
        num_scalar_prefetch=2,
        grid=(rows // tm,),
        in_specs=[pl.BlockSpec((tm, d), lambda i, gid, vld: (i, 0)),
                  _resident((1, d, f), lambda i, gid, vld: (gid[i], 0, 0)),
                  _resident((1, d, f), lambda i, gid, vld: (gid[i], 0, 1)),
                  _resident((1, f, d), lambda i, gid, vld: (gid[i], 0, 0))],
        out_specs=pl.BlockSpec((tm, d), lambda i, gid, vld: (i, 0)),
    )
    w_gu_b = w_gu.astype(BF16)
    return pl.pallas_call(
        _ffn_moe_kernel,
        grid_spec=grid_spec,
        out_shape=jax.ShapeDtypeStruct((rows, d), F32),
        compiler_params=pltpu.CompilerParams(
            dimension_semantics=("arbitrary",), vmem_limit_bytes=VMEM_LIMIT),
        name="ffn_moe",
    )(gid, valid, xs, w_gu_b, w_gu_b, w_d.astype(BF16))


def _combine_kernel(pos_ref, ys_hbm, x_ref, gt_ref, prob_ref, fg_ref, o_ref, ybuf, sem, *, tc):
    def row_copy(t, kk):
        return pltpu.make_async_copy(
            ys_hbm.at[pl.ds(pos_ref[TOP_K * t + kk], 1)], ybuf.at[kk, pl.ds(t, 1)], sem)

    for t in range(tc):
        for kk in range(TOP_K):
            row_copy(t, kk).start()
    for kk in range(TOP_K):
        pltpu.make_async_copy(ys_hbm.at[pl.ds(0, tc)], ybuf.at[kk], sem).wait()

    y = prob_ref[:, 0:1] * ybuf[0] + prob_ref[:, 1:2] * ybuf[1]
    x = x_ref[...] + (1.0 + gt_ref[0]) * y
    ms = jnp.mean(x * x, axis=-1, keepdims=True)
    o_ref[...] = x * lax.rsqrt(ms + NORM_EPS) * fg_ref[...]


def _combine(pos_flat, ys, x2d, gt, prob, final_g, tiles_per_batch, tc):
    n, d = x2d.shape
    tok = pl.BlockSpec((tc, d), lambda i: (i, 0))
    return pl.pallas_call(
        functools.partial(_combine_kernel, tc=tc),
        grid=(n // tc,),
        in_specs=[pl.BlockSpec((TOP_K * tc,), lambda i: (i,), memory_space=pltpu.SMEM),
                  pl.BlockSpec(memory_space=pl.ANY), tok,
                  pl.BlockSpec((1, 1, d), lambda i: (i // tiles_per_batch, 0, 0)),
                  pl.BlockSpec((tc, TOP_K), lambda i: (i, 0)),
                  pl.BlockSpec((1, d), lambda i: (0, 0))],
        out_specs=tok,
        out_shape=jax.ShapeDtypeStruct((n, d), F32),
        scratch_shapes=[pltpu.VMEM((TOP_K, tc, d), F32), pltpu.SemaphoreType.DMA(())],
        compiler_params=pltpu.CompilerParams(
            dimension_semantics=("arbitrary",), vmem_limit_bytes=VMEM_LIMIT),
        name="moe_combine",
    )(pos_flat, ys, x2d, gt[:, None, :], prob, final_g[None, :])


def _moe_layout(eid, rank, counts, n_experts, tm, n_tiles):
    cnt = counts[0, :n_experts]
    tiles = (cnt + tm - 1) // tm
    tile_end = jnp.cumsum(tiles)
    offsets = (tile_end - tiles) * tm
    pos = offsets[eid] + rank
    tile_ids = jnp.arange(n_tiles, dtype=I32)
    total = tile_end[-1]
    last_id = jnp.minimum(tile_ids, total - 1)
    gid = jnp.sum((last_id[:, None] >= tile_end[None, :]).astype(I32), axis=1)
    valid = (tile_ids < total).astype(I32)
    return pos.reshape(-1).astype(I32), gid, valid


def kernel(x, c, ada_w, ada_b, norm_g, final_g, rwkv_mu, rwkv_w_rkv, rwkv_w_o, rwkv_w0, rwkv_w1, rwkv_w2, rwkv_a0, rwkv_a1, rwkv_a2, rwkv_g1, rwkv_g2, rwkv_k_k, rwkv_k_a, rwkv_r_k, rwkv_gn_w, rwkv_gn_b, lru_w_in, lru_conv_w, lru_conv_b, lru_w_gates, lru_b_gates, lru_lam, lru_w_out, ffn_w_gu, ffn_w_d, moe_w_router, moe_b_router, moe_w_gu, moe_w_d):
    b, t, d = x.shape
    n = b * t
    n_experts = moe_w_router.shape[-1]
    mod = _ada_mod(c, ada_w, ada_b)

    def mods(i):
        return [mod[i, :, q * d:(q + 1) * d] for q in range(6)]

    sh1, sc1, gt1, sh2, sc2, gt2 = mods(0)
    r, lw, k2, v, kk, bv, g, bonus = _rwkv_pre(
        x, sh1, sc1, norm_g[0, 0], rwkv_mu[0], rwkv_w_rkv[0], rwkv_w1[0], rwkv_w2[0],
        rwkv_a1[0], rwkv_a2[0], rwkv_g1[0], rwkv_g2[0], rwkv_w0[0], rwkv_a0[0],
        rwkv_k_k[0], rwkv_k_a[0], rwkv_r_k[0], tm=min(256, t))
    yg = _wkv_scan(r, lw, k2, v, kk, bv, g, bonus, rwkv_gn_w[0], rwkv_gn_b[0],
                   chunks_per_step=min(2, t // CHUNK))
    x = _ffn_dense(yg, x, gt1, sh2, sc2, gt2, norm_g[0, 1], rwkv_w_o[0], ffn_w_gu[0], ffn_w_d[0],
                   tm=min(256, t))

    sh1, sc1, gt1, sh2, sc2, gt2 = mods(1)
    x = _lru_block(x, sh1, sc1, gt1, norm_g[1, 0], lru_w_in[0], lru_conv_w[0], lru_conv_b[0],
                   lru_w_gates[0], lru_b_gates[0], lru_lam[0], lru_w_out[0], tm=min(256, t))

    x2d = x.reshape(n, d)
    tm_r = min(512, t)
    h2, eid, rank, prob, counts = _router(x2d, sh2, sc2, norm_g[1, 1], moe_w_router[0],
                                          moe_b_router[0], t // tm_r, tm_r)
    tm_g = min(256, t)
    n_tiles = (TOP_K * n) // tm_g + n_experts
    pos, gid, valid = _moe_layout(eid, rank, counts, n_experts, tm_g, n_tiles)
    xs = _dispatch(pos, h2, n_tiles * tm_g, tb=min(512, n))
    ys = _ffn_moe(gid, valid, xs, moe_w_gu[0], moe_w_d[0], tm_g)
    tc = min(512, t)
    out = _combine(pos, ys, x2d, gt2, prob, final_g, t // tc, tc)
    return out.reshape(b, t, d)
```

```python
import functools

import jax
import jax.numpy as jnp
from jax import lax
from jax.experimental import pallas as pl
from jax.experimental.pallas import tpu as pltpu

F32 = jnp.float32
BF16 = jnp.bfloat16
I32 = jnp.int32

HEAD = 64
CHUNK = 64
GROUP = 256
HEADS_PER_GROUP = GROUP // HEAD
GN_EPS = 64e-5
NORM_EPS = 1e-6
LRU_C = 8.0
CONV_WIDTH = 4
LRU_BLOCK = 256
TOP_K = 2
LANES = 128
VMEM_LIMIT = 56 * 1024 * 1024


def _dot(a, b):
    return jnp.dot(a, b, preferred_element_type=F32)


def _dot_nt(a, b):
    return lax.dot_general(a, b, (((1,), (1,)), ((), ())), preferred_element_type=F32)


def _dot_tn(a, b):
    return lax.dot_general(a, b, (((0,), (0,)), ((), ())), preferred_element_type=F32)


def _softplus(u):
    return jnp.maximum(u, 0.0) + jnp.log1p(jnp.exp(-jnp.abs(u)))


def _modulate(x, ng, sh, sc):
    ms = jnp.mean(x * x, axis=-1, keepdims=True)
    return x * lax.rsqrt(ms + NORM_EPS) * ng * (1.0 + sc) + sh


def _split_bf16(x):
    hi = x.astype(BF16)
    lo = (x - hi.astype(F32)).astype(BF16)
    return hi, lo


def _mod_kernel(c_ref, w_ref, b_ref, o_ref):
    c = c_ref[...]
    cond = c * jax.nn.sigmoid(c)
    o_ref[0] = _dot(cond.astype(BF16), w_ref[0].astype(BF16)) + b_ref[0]


def _ada_mod(c, ada_w, ada_b):
    depth, d, d6 = ada_w.shape
    b = c.shape[0]
    rows = 8
    c8 = jnp.pad(c, ((0, rows - b), (0, 0)))
    tn = 1024
    out = pl.pallas_call(
        _mod_kernel,
        grid=(depth, d6 // tn),
        in_specs=[
            pl.BlockSpec((rows, d), lambda i, j: (0, 0)),
            pl.BlockSpec((1, d, tn), lambda i, j: (i, 0, j)),
            pl.BlockSpec((1, 1, tn), lambda i, j: (i, 0, j)),
        ],
        out_specs=pl.BlockSpec((1, rows, tn), lambda i, j: (i, 0, j)),
        out_shape=jax.ShapeDtypeStruct((depth, rows, d6), F32),
        name="ada_mod",
    )(c8, ada_w, ada_b.reshape(depth, 1, d6))
    return out[:, :b]


def _rwkv_pre_kernel(x_ref, sh_ref, sc_ref, ng_ref, mu_ref, wrkv_ref, w1_ref, w2_ref,
                     a1_ref, a2_ref, g1_ref, g2_ref, vec_ref, seg_ref, segt_ref,
                     r_out, lw_out, k_out, v_out, kk_out, b_out, g_out, bonus_out,
                     hbuf):
    t = pl.program_id(1)
    tm = x_ref.shape[1]
    d = x_ref.shape[2]

    h = _modulate(x_ref[0], ng_ref[...], sh_ref[0], sc_ref[0])

    @pl.when(t == 0)
    def _():
        hbuf[...] = jnp.zeros((8, d), F32)

    rr = lax.broadcasted_iota(I32, (tm, tm), 0)
    cc = lax.broadcasted_iota(I32, (tm, tm), 1)
    hprev = _dot((cc == rr - 1).astype(BF16), h.astype(BF16))
    first = lax.broadcasted_iota(I32, (tm, d), 0) == 0
    hprev = jnp.where(first, hbuf[7:8, :], hprev)
    hbuf[...] = h[tm - 8:tm, :]
    xx = hprev - h

    def mix(p):
        return (h + xx * mu_ref[p:p + 1, :]).astype(BF16)

    r = _dot(mix(0), wrkv_ref[0])
    k = _dot(mix(1), wrkv_ref[1])
    v = _dot(mix(2), wrkv_ref[2])
    wl = _dot(jnp.tanh(_dot(mix(3), w1_ref[...])).astype(BF16), w2_ref[...])
    al = _dot(_dot(mix(4), a1_ref[...]).astype(BF16), a2_ref[...])
    g = _dot(jax.nn.sigmoid(_dot(mix(5), g1_ref[...])).astype(BF16), g2_ref[...])

    w0 = vec_ref[0:1, :]
    a0 = vec_ref[1:2, :]
    k_k = vec_ref[2:3, :]
    k_a = vec_ref[3:4, :]
    r_k = vec_ref[4:5, :]

    def headsum(z):
        s = _dot(z.astype(BF16), seg_ref[...])
        s_hi, s_lo = _split_bf16(s)
        return _dot(jnp.concatenate([s_hi, s_lo], axis=1), segt_ref[...])

    lw = -0.6065306597126334 * jax.nn.sigmoid(w0 + wl)
    a = jax.nn.sigmoid(a0 + al)
    kk = k * k_k
    kk = kk * lax.rsqrt(jnp.maximum(headsum(kk * kk), 1e-24))
    k2 = k * (1.0 + (a - 1.0) * k_a)
    bonus = headsum(r * k2 * r_k) * v

    r_out[0] = r
    lw_out[0] = lw
    k_out[0] = k2
    v_out[0] = v
    kk_out[0] = kk
    b_out[0] = kk * a
    g_out[0] = g
    bonus_out[0] = bonus


def _pad_to(x, axis, size):
    pad = [(0, 0)] * x.ndim
    pad[axis] = (0, size - x.shape[axis])
    return jnp.pad(x, pad)


def _rwkv_pre(x, sh, sc, ng, mu, w_rkv, w1, w2, a1, a2, g1, g2, w0, a0, k_k, k_a, r_k, tm):
    b, t, d = x.shape
    nh = d // HEAD
    lw_pad = LANES * pl.cdiv(w1.shape[1], LANES)
    la_pad = LANES * pl.cdiv(a1.shape[1], LANES)
    lg_pad = LANES * pl.cdiv(g1.shape[1], LANES)
    vecs = _pad_to(jnp.stack([w0, a0, k_k, k_a, r_k.reshape(d)]), 0, 8)
    head_of_lane = jnp.arange(d) // HEAD
    seg = (head_of_lane[:, None] == jnp.arange(LANES)[None, :]).astype(BF16)
    segt = jnp.concatenate([seg.T, seg.T], axis=0)
    del nh
    full = lambda *shape: pl.BlockSpec(shape, lambda bi, ti: (0,) * len(shape))
    tok = pl.BlockSpec((1, tm, d), lambda bi, ti: (bi, ti, 0))
    vec = pl.BlockSpec((1, 1, d), lambda bi, ti: (bi, 0, 0))
    outs = pl.pallas_call(
        _rwkv_pre_kernel,
        grid=(b, t // tm),
        in_specs=[tok, vec, vec, full(1, d), full(8, d), full(3, d, d),
                  full(d, lw_pad), full(lw_pad, d), full(d, la_pad), full(la_pad, d),
                  full(d, lg_pad), full(lg_pad, d), full(8, d), full(d, LANES), full(2 * LANES, d)],
        out_specs=[tok] * 8,
        out_shape=[jax.ShapeDtypeStruct((b, t, d), F32)] * 8,
        scratch_shapes=[pltpu.VMEM((8, d), F32)],
        compiler_params=pltpu.CompilerParams(
            dimension_semantics=("arbitrary", "arbitrary"), vmem_limit_bytes=VMEM_LIMIT),
        name="rwkv_pre",
    )(x, sh[:, None, :], sc[:, None, :], ng[None, :], _pad_to(mu, 0, 8), w_rkv.astype(BF16),
      _pad_to(w1, 1, lw_pad).astype(BF16), _pad_to(w2, 0, lw_pad).astype(BF16),
      _pad_to(a1, 1, la_pad).astype(BF16), _pad_to(a2, 0, la_pad).astype(BF16),
      _pad_to(g1, 1, lg_pad).astype(BF16), _pad_to(g2, 0, lg_pad).astype(BF16),
      vecs, seg, segt)
    return outs


def _wkv_scan_kernel(r_ref, lw_ref, k_ref, v_ref, kk_ref, b_ref, g_ref, bonus_ref,
                     gnw_ref, gnb_ref, o_ref, h_scr, *, chunks_per_step):
    L = CHUNK
    W = GROUP
    ngroups = r_ref.shape[2] // W

    @pl.when(pl.program_id(1) == 0)
    def _():
        h_scr[...] = jnp.zeros_like(h_scr)

    row = lax.broadcasted_iota(I32, (L, W), 0)
    lane = lax.broadcasted_iota(I32, (L, W), 1)
    sidx = lane & (L - 1)
    lane_head = lane >> 6
    strict = sidx < row
    incl = sidx <= row
    eye = (sidx == row).astype(F32)
    same16 = (row >> 4) == (sidx >> 4)
    same32 = (row >> 5) == (sidx >> 5)
    m16 = strict & same16
    m32 = strict & same32 & jnp.logical_not(same16)
    m64 = strict & jnp.logical_not(same32)
    rb = lax.broadcasted_iota(I32, (W, W), 0)
    cb = lax.broadcasted_iota(I32, (W, W), 1)
    bmask = (rb >> 6) == (cb >> 6)
    diag = rb == cb
    ones_bd = bmask.astype(BF16)
    tri_r = lax.broadcasted_iota(I32, (L, 3 * L), 0)
    tri_c = lax.broadcasted_iota(I32, (L, 3 * L), 1)
    tri3 = ((tri_c & (L - 1)) <= tri_r).astype(BF16)

    def bd(y):
        yt = jnp.concatenate([y] * HEADS_PER_GROUP, axis=0)
        return jnp.where(bmask, yt, 0.0).astype(BF16)

    def hmm(x, ybd):
        return _dot(x.astype(BF16), ybd)

    streams = [(q, j) for j in range(chunks_per_step) for q in range(ngroups)]
    S = range(len(streams))

    def ld(ref, s):
        q, j = streams[s]
        return ref[0, j * L:(j + 1) * L, q * W:(q + 1) * W]

    r = [ld(r_ref, s) for s in S]
    lw = [ld(lw_ref, s) for s in S]
    k = [ld(k_ref, s) for s in S]
    v = [ld(v_ref, s) for s in S]
    kk = [ld(kk_ref, s) for s in S]
    bv = [ld(b_ref, s) for s in S]

    def cumlog(x):
        hi = x.astype(BF16)
        rem = x - hi.astype(F32)
        mid = rem.astype(BF16)
        lo = (rem - mid.astype(F32)).astype(BF16)
        return _dot(tri3, jnp.concatenate([hi, mid, lo], axis=0))

    cl = [cumlog(lw[s]) for s in S]
    cl_last = [cl[s][L - 1:L, :] for s in S]
    e_pos = [jnp.exp(cl[s]) for s in S]
    e_neg = [jnp.exp(-cl[s]) for s in S]
    e_end = [jnp.exp(cl_last[s] - cl[s]) for s in S]
    rt = [r[s] * e_pos[s] for s in S]
    at = [-kk[s] * jnp.exp(cl[s] - lw[s]) for s in S]
    bt = [bv[s] * e_neg[s] for s in S]
    kt = [k[s] * e_neg[s] for s in S]

    def gram(s):
        x = jnp.concatenate([at[s], rt[s]], axis=0).astype(BF16)
        ys = [jnp.where(lane_head == hh, bt[s], 0.0) for hh in range(HEADS_PER_GROUP)]
        ys += [jnp.where(lane_head == hh, kt[s], 0.0) for hh in range(HEADS_PER_GROUP)]
        return _dot_nt(x, jnp.concatenate(ys, axis=0).astype(BF16))

    gm = [gram(s) for s in S]
    a_ab = [jnp.where(strict, gm[s][:L, :W], 0.0) for s in S]
    a_ak = [jnp.where(strict, gm[s][:L, W:], 0.0) for s in S]
    a_rb = [jnp.where(incl, gm[s][L:, :W], 0.0) for s in S]
    a_rk = [jnp.where(incl, gm[s][L:, W:], 0.0) for s in S]

    a0 = [jnp.where(m16, a_ab[s], 0.0) for s in S]
    pw = [hmm(a0[s], bd(a0[s])) for s in S]
    tinv = [eye + a0[s] for s in S]
    for _ in range(2):
        ts = [_dot(jnp.concatenate([tinv[s], pw[s]], axis=0).astype(BF16), bd(pw[s])) for s in S]
        tinv = [tinv[s] + ts[s][:L] for s in S]
        pw = [ts[s][L:] for s in S]
    tinv = [tinv[s] + hmm(tinv[s], bd(pw[s])) for s in S]
    for msk in (m32, m64):
        inner = [hmm(jnp.where(msk, a_ab[s], 0.0), bd(tinv[s])) for s in S]
        tinv = [tinv[s] + hmm(tinv[s], bd(inner[s])) for s in S]

    vbd = [bd(v[s]) for s in S]
    av = [hmm(a_ak[s], vbd[s]) for s in S]
    tx = [_dot(tinv[s].astype(BF16), jnp.concatenate([bd(at[s]), bd(av[s])], axis=1)) for s in S]
    ahat = [tx[s][:, :W] for s in S]
    vp = [tx[s][:, W:] for s in S]
    ox = [_dot(a_rb[s].astype(BF16), jnp.concatenate([bd(ahat[s]), bd(vp[s])], axis=1)) for s in S]
    rhat = [rt[s] + ox[s][:, :W] for s in S]
    o_intra = [ox[s][:, W:] + hmm(a_rk[s], vbd[s]) for s in S]

    def state_terms(s):
        z = jnp.concatenate([bv[s] * e_end[s], k[s] * e_end[s]], axis=0).astype(BF16)
        wm = jnp.concatenate(
            [jnp.concatenate([ahat[s], vp[s]], axis=1),
             jnp.concatenate([jnp.zeros((L, W), F32), v[s]], axis=1)], axis=0).astype(BF16)
        mn = _dot_tn(z, wm)
        m_mat = jnp.where(bmask, mn[:, :W], 0.0) + jnp.where(diag, jnp.exp(cl_last[s]), 0.0)
        return m_mat, jnp.where(bmask, mn[:, W:], 0.0)

    mn = [state_terms(s) for s in S]

    o = [None] * len(streams)
    for q in range(ngroups):
        h = h_scr[q]
        for j in range(chunks_per_step):
            s = streams.index((q, j))
            m_hi, m_lo = _split_bf16(mn[s][0])
            lhs = jnp.concatenate([m_hi, m_lo, rhat[s].astype(BF16)], axis=0)
            res = _dot(lhs, h.astype(BF16))
            o[s] = res[2 * W:] + o_intra[s]
            h = res[:W] + res[W:2 * W] + mn[s][1]
        h_scr[q] = h

    def headmean(zs):
        parts = []
        for z in zs:
            parts += list(_split_bf16(z))
        red = _dot(jnp.concatenate(parts, axis=0), ones_bd) * (1.0 / HEAD)
        return [red[2 * L * s:2 * L * s + L] + red[2 * L * s + L:2 * L * (s + 1)] for s in S]

    mean = headmean(o)
    dlt = [o[s] - mean[s] for s in S]
    var = headmean([dlt[s] * dlt[s] for s in S])
    for s in S:
        q, j = streams[s]
        gsl = slice(q * W, (q + 1) * W)
        rsl = slice(j * L, (j + 1) * L)
        yn = dlt[s] * lax.rsqrt(var[s] + GN_EPS) * gnw_ref[:, gsl] + gnb_ref[:, gsl]
        o_ref[0, rsl, gsl] = ((yn + bonus_ref[0, rsl, gsl]) * g_ref[0, rsl, gsl]).astype(o_ref.dtype)


def _wkv_scan(r, lw, k, v, kk, bv, g, bonus, gn_w, gn_b, chunks_per_step):
    b, t, d = r.shape
    lb = CHUNK * chunks_per_step
    tok = pl.BlockSpec((1, lb, d), lambda bi, ci: (bi, ci, 0))
    vec = pl.BlockSpec((1, d), lambda bi, ci: (0, 0))
    return pl.pallas_call(
        functools.partial(_wkv_scan_kernel, chunks_per_step=chunks_per_step),
        grid=(b, t // lb),
        in_specs=[tok] * 8 + [vec, vec],
        out_specs=tok,
        out_shape=jax.ShapeDtypeStruct((b, t, d), BF16),
        scratch_shapes=[pltpu.VMEM((d // GROUP, GROUP, GROUP), F32)],
        compiler_params=pltpu.CompilerParams(
            dimension_semantics=("arbitrary", "arbitrary"), vmem_limit_bytes=VMEM_LIMIT),
        name="wkv_scan",
    )(r, lw, k, v, kk, bv, g, bonus, gn_w[None, :], gn_b[None, :])


def _swiglu(h, wg, wu, wd):
    g = _dot(h, wg)
    u = _dot(h, wu)
    return _dot((g * jax.nn.sigmoid(g) * u).astype(BF16), wd)


def _ffn_dense_kernel(a_ref, x_ref, gt1_ref, sh_ref, sc_ref, gt2_ref, ng_ref, wo_ref,
                      wg_ref, wu_ref, wd_ref, o_ref):
    x1 = x_ref[0] + (1.0 + gt1_ref[0]) * _dot(a_ref[0], wo_ref[...])
    h = _modulate(x1, ng_ref[...], sh_ref[0], sc_ref[0]).astype(BF16)
    o_ref[0] = x1 + (1.0 + gt2_ref[0]) * _swiglu(h, wg_ref[...], wu_ref[...], wd_ref[...])


def _resident(shape, index_map):
    return pl.BlockSpec(shape, index_map, pipeline_mode=pl.Buffered(1))


def _ffn_dense(a, x, gt1, sh, sc, gt2, ng, w_o, w_gu, w_d, tm):
    b, t, d = x.shape
    f = w_d.shape[0]
    tok = pl.BlockSpec((1, tm, d), lambda bi, ti: (bi, ti, 0))
    vec = pl.BlockSpec((1, 1, d), lambda bi, ti: (bi, 0, 0))
    w_gu_b = w_gu.astype(BF16)
    return pl.pallas_call(
        _ffn_dense_kernel,
        grid=(b, t // tm),
        in_specs=[tok, tok, vec, vec, vec, vec, pl.BlockSpec((1, d), lambda bi, ti: (0, 0)),
                  _resident((d, d), lambda bi, ti: (0, 0)),
                  _resident((d, f), lambda bi, ti: (0, 0)),
                  _resident((d, f), lambda bi, ti: (0, 1)),
                  _resident((f, d), lambda bi, ti: (0, 0))],
        out_specs=tok,
        out_shape=jax.ShapeDtypeStruct((b, t, d), F32),
        compiler_params=pltpu.CompilerParams(
            dimension_semantics=("arbitrary", "arbitrary"), vmem_limit_bytes=VMEM_LIMIT),
        name="ffn_dense",
    )(a, x, gt1[:, None, :], sh[:, None, :], sc[:, None, :], gt2[:, None, :], ng[None, :],
      w_o.astype(BF16), w_gu_b, w_gu_b, w_d.astype(BF16))


def _lru_kernel(x_ref, sh_ref, sc_ref, gt_ref, ng_ref, win_ref, cw_ref, cb_ref, wg_ref, bg_ref,
                lam_ref, wout_ref, o_ref, xbuf, abuf, bbuf, carry):
    t = pl.program_id(1)
    tm = x_ref.shape[1]
    w = win_ref.shape[1] // 2
    nblk = w // LRU_BLOCK
    pad = tm

    @pl.when(t == 0)
    def _():
        xbuf[0:8, :] = jnp.zeros((8, w), F32)
        carry[...] = jnp.zeros_like(carry)
        for s in range(2):
            abuf[s, 0:pad, :] = jnp.ones((pad, w), F32)
            bbuf[s, 0:pad, :] = jnp.zeros((pad, w), F32)

    x = x_ref[0]
    h = _modulate(x, ng_ref[...], sh_ref[0], sc_ref[0]).astype(BF16)
    xg = _dot(h, win_ref[...])
    xb = xg[:, :w]
    gb = xg[:, w:]
    gate = 0.5 * gb * (1.0 + jnp.tanh(0.7978845608028654 * (gb + 0.044715 * gb * gb * gb)))

    xbuf[8:8 + tm, :] = xb
    conv = cb_ref[...] + cw_ref[CONV_WIDTH - 1:CONV_WIDTH, :] * xb
    for jj in range(CONV_WIDTH - 1):
        shift = CONV_WIDTH - 1 - jj
        conv = conv + cw_ref[jj:jj + 1, :] * xbuf[8 - shift:8 - shift + tm, :]
    xbuf[0:8, :] = xb[tm - 8:tm, :]

    conv_b = conv.astype(BF16)
    rs, is_ = [], []
    for n in range(nblk):
        gts = _dot(conv_b[:, n * LRU_BLOCK:(n + 1) * LRU_BLOCK], wg_ref[n]) + bg_ref[n]
        gts = jax.nn.sigmoid(gts)
        rs.append(gts[:, :LRU_BLOCK])
        is_.append(gts[:, LRU_BLOCK:])
    r_t = jnp.concatenate(rs, axis=1)
    i_t = jnp.concatenate(is_, axis=1)

    log_a = -LRU_C * r_t * _softplus(-lam_ref[...])
    a_t = jnp.exp(log_a)
    b_t = jnp.sqrt(-jnp.tanh(log_a) * (a_t * a_t + 1.0)) * (i_t * conv)

    abuf[0, pad:pad + tm, :] = a_t
    bbuf[0, pad:pad + tm, :] = b_t
    step = 1
    src = 0
    while step < tm:
        a_cur = abuf[src, pad:pad + tm, :]
        b_cur = bbuf[src, pad:pad + tm, :]
        a_sh = abuf[src, pad - step:pad - step + tm, :]
        b_sh = bbuf[src, pad - step:pad - step + tm, :]
        abuf[1 - src, pad:pad + tm, :] = a_cur * a_sh
        bbuf[1 - src, pad:pad + tm, :] = a_cur * b_sh + b_cur
        src = 1 - src
        step *= 2
    hs = bbuf[src, pad:pad + tm, :] + abuf[src, pad:pad + tm, :] * carry[0:1, :]
    carry[...] = jnp.broadcast_to(hs[tm - 1:tm, :], carry.shape)

    y = _dot((hs * gate).astype(BF16), wout_ref[...])
    o_ref[0] = x + (1.0 + gt_ref[0]) * y


def _lru_block(x, sh, sc, gt, ng, w_in, conv_w, conv_b, w_gates, b_gates, lam, w_out, tm):
    b, t, d = x.shape
    w = w_out.shape[0]
    nblk = w // LRU_BLOCK
    full = lambda *shape: pl.BlockSpec(shape, lambda bi, ti: (0,) * len(shape))
    tok = pl.BlockSpec((1, tm, d), lambda bi, ti: (bi, ti, 0))
    vec = pl.BlockSpec((1, 1, d), lambda bi, ti: (bi, 0, 0))
    return pl.pallas_call(
        _lru_kernel,
        grid=(b, t // tm),
        in_specs=[tok, vec, vec, vec, full(1, d), full(d, 2 * w), full(CONV_WIDTH, w), full(1, w),
                  full(nblk, LRU_BLOCK, 2 * LRU_BLOCK), full(nblk, 1, 2 * LRU_BLOCK),
                  full(1, w), full(w, d)],
        out_specs=tok,
        out_shape=jax.ShapeDtypeStruct((b, t, d), F32),
        scratch_shapes=[pltpu.VMEM((tm + 8, w), F32), pltpu.VMEM((2, 2 * tm, w), F32),
                        pltpu.VMEM((2, 2 * tm, w), F32), pltpu.VMEM((8, w), F32)],
        compiler_params=pltpu.CompilerParams(
            dimension_semantics=("arbitrary", "arbitrary"), vmem_limit_bytes=VMEM_LIMIT),
        name="rglru_block",
    )(x, sh[:, None, :], sc[:, None, :], gt[:, None, :], ng[None, :], w_in.astype(BF16),
      conv_w, conv_b[None, :], w_gates.astype(BF16), b_gates[:, None, :], lam[None, :],
      w_out.astype(BF16))


def _router_kernel(x_ref, sh_ref, sc_ref, ng_ref, wr_ref, br_ref,
                   h_out, eid_out, rank_out, prob_out, cnt_out, cnt_scr):
    i = pl.program_id(0)
    tm = x_ref.shape[0]

    @pl.when(i == 0)
    def _():
        cnt_scr[...] = jnp.zeros_like(cnt_scr)

    h = _modulate(x_ref[...], ng_ref[...], sh_ref[0], sc_ref[0])
    h_out[...] = h
    h_hi, h_lo = _split_bf16(h)
    w_hi, w_lo = _split_bf16(wr_ref[...])
    logits = _dot(h_hi, w_hi) + _dot(h_lo, w_hi) + _dot(h_hi, w_lo) + br_ref[...]

    lane = lax.broadcasted_iota(I32, logits.shape, 1)
    m1 = jnp.max(logits, axis=-1, keepdims=True)
    i1 = jnp.min(jnp.where(logits == m1, lane, LANES), axis=-1, keepdims=True)
    l2 = jnp.where(lane == i1, -jnp.inf, logits)
    m2 = jnp.max(l2, axis=-1, keepdims=True)
    i2 = jnp.min(jnp.where(l2 == m2, lane, LANES), axis=-1, keepdims=True)
    e = jnp.exp(m2 - m1)
    p1 = 1.0 / (1.0 + e)
    p2 = e / (1.0 + e)

    oh1 = (lane == i1).astype(F32)
    oh2 = (lane == i2).astype(F32)
    oh = oh1 + oh2
    rr = lax.broadcasted_iota(I32, (tm, tm), 0)
    cc = lax.broadcasted_iota(I32, (tm, tm), 1)
    tri = (cc < rr).astype(BF16)
    before = _dot(tri, oh.astype(BF16)) + cnt_scr[0:1, :]
    rank1 = jnp.sum(before * oh1, axis=-1, keepdims=True)
    rank2 = jnp.sum(before * oh2, axis=-1, keepdims=True)
    cnt_scr[...] = cnt_scr[...] + jnp.sum(oh, axis=0, keepdims=True)

    eid_out[:, 0:1] = i1
    eid_out[:, 1:2] = i2
    rank_out[:, 0:1] = rank1.astype(I32)
    rank_out[:, 1:2] = rank2.astype(I32)
    prob_out[:, 0:1] = p1
    prob_out[:, 1:2] = p2
    cnt_out[...] = cnt_scr[...].astype(I32)


def _router(x2d, sh, sc, ng, w_router, b_router, tiles_per_batch, tm):
    n, d = x2d.shape
    e = w_router.shape[1]
    wr = _pad_to(w_router, 1, LANES)
    br = jnp.concatenate([b_router, jnp.full((LANES - e,), -1e30, F32)])[None, :]
    tok = pl.BlockSpec((tm, d), lambda i: (i, 0))
    vec = pl.BlockSpec((1, 1, d), lambda i: (i // tiles_per_batch, 0, 0))
    two = pl.BlockSpec((tm, TOP_K), lambda i: (i, 0))
    return pl.pallas_call(
        _router_kernel,
        grid=(n // tm,),
        in_specs=[tok, vec, vec, pl.BlockSpec((1, d), lambda i: (0, 0)),
                  pl.BlockSpec((d, LANES), lambda i: (0, 0)), pl.BlockSpec((1, LANES), lambda i: (0, 0))],
        out_specs=[tok, two, two, two, pl.BlockSpec((8, LANES), lambda i: (0, 0))],
        out_shape=[jax.ShapeDtypeStruct((n, d), F32), jax.ShapeDtypeStruct((n, TOP_K), I32),
                   jax.ShapeDtypeStruct((n, TOP_K), I32), jax.ShapeDtypeStruct((n, TOP_K), F32),
                   jax.ShapeDtypeStruct((8, LANES), I32)],
        scratch_shapes=[pltpu.VMEM((8, LANES), F32)],
        compiler_params=pltpu.CompilerParams(
            dimension_semantics=("arbitrary",), vmem_limit_bytes=VMEM_LIMIT),
        name="moe_router",
    )(x2d, sh[:, None, :], sc[:, None, :], ng[None, :], wr, br)


def _moe_ffn_kernel(gid_ref, src0_ref, src_next_ref, dst_prev_ref, h_hbm, wg_ref, wu_ref, wd_ref,
                    ys_hbm, x0, x1, y0, y1, gsem, ssem, *, n_tiles, tm, spare_row):
    del gid_ref
    i = pl.program_id(0)
    xbufs = (x0, x1)
    ybufs = (y0, y1)

    def gather(src_ref, xbuf):
        for t in range(tm):
            pltpu.make_async_copy(h_hbm.at[pl.ds(src_ref[0, 0, t], 1)], xbuf.at[pl.ds(t, 1)],
                                  gsem).start()

    def scatter(ybuf):
        for t in range(tm):
            pltpu.make_async_copy(ybuf.at[pl.ds(t, 1)],
                                  ys_hbm.at[pl.ds(dst_prev_ref[0, 0, t], 1)], ssem).start()

    def wait_gather(xbuf):
        pltpu.make_async_copy(h_hbm.at[pl.ds(0, tm)], xbuf, gsem).wait()

    def wait_scatter(ybuf):
        pltpu.make_async_copy(ybuf, ys_hbm.at[pl.ds(0, tm)], ssem).wait()

    @pl.when(i == 0)
    def _():
        y0[...] = jnp.zeros_like(y0)
        y1[...] = jnp.zeros_like(y1)
        pltpu.make_async_copy(y0, ys_hbm.at[pl.ds(spare_row, tm)], ssem).start()
        gather(src0_ref, x0)

    for par in range(2):
        @pl.when((i < n_tiles) & (i % 2 == par))
        def _(par=par):
            xa, xb, ya, yb = xbufs[par], xbufs[1 - par], ybufs[par], ybufs[1 - par]
            wait_gather(xa)
            wait_scatter(ya)
            gather(src_next_ref, xb)
            scatter(yb)
            ya[...] = _swiglu(xa[...].astype(BF16), wg_ref[0], wu_ref[0], wd_ref[0])

    @pl.when(i == n_tiles)
    def _():
        par = n_tiles % 2
        wait_gather(xbufs[par])
        wait_scatter(ybufs[par])
        scatter(ybufs[1 - par])
        wait_scatter(ybufs[1 - par])


def _moe_ffn(gid, src, dst, h2d, w_gu, w_d, out_rows, spare_row, tm):
    n_tiles = gid.shape[0]
    _, d = h2d.shape
    f = w_d.shape[1]
    idx = lambda fn: pl.BlockSpec((1, 1, tm), fn, memory_space=pltpu.SMEM)
    last = n_tiles - 1
    grid_spec = pltpu.PrefetchScalarGridSpec(
        num_scalar_prefetch=1,
        grid=(n_tiles + 1,),
        in_specs=[idx(lambda i, gid: (0, 0, 0)),
                  idx(lambda i, gid: (jnp.minimum(i + 1, n_tiles), 0, 0)),
                  idx(lambda i, gid: (i, 0, 0)),
                  pl.BlockSpec(memory_space=pl.ANY),
                  _resident((1, d, f), lambda i, gid: (gid[jnp.minimum(i, last)], 0, 0)),
                  _resident((1, d, f), lambda i, gid: (gid[jnp.minimum(i, last)], 0, 1)),
                  _resident((1, f, d), lambda i, gid: (gid[jnp.minimum(i, last)], 0, 0))],
        out_specs=pl.BlockSpec(memory_space=pl.ANY),
        scratch_shapes=[pltpu.VMEM((tm, d), F32)] * 4 + [pltpu.SemaphoreType.DMA(())] * 2,
    )
    w_gu_b = w_gu.astype(BF16)
    return pl.pallas_call(
        functools.partial(_moe_ffn_kernel, n_tiles=n_tiles, tm=tm, spare_row=spare_row),
        grid_spec=grid_spec,
        out_shape=jax.ShapeDtypeStruct((out_rows, d), F32),
        compiler_params=pltpu.CompilerParams(
            dimension_semantics=("arbitrary",), vmem_limit_bytes=VMEM_LIMIT),
        name="moe_ffn",
    )(gid, src, src, dst, h2d, w_gu_b, w_gu_b, w_d.astype(BF16))


def _combine_kernel(y0_ref, y1_ref, x_ref, gt_ref, prob_ref, fg_ref, o_ref):
    y = prob_ref[:, 0:1] * y0_ref[...] + prob_ref[:, 1:2] * y1_ref[...]
    x = x_ref[...] + (1.0 + gt_ref[0]) * y
    ms = jnp.mean(x * x, axis=-1, keepdims=True)
    o_ref[...] = x * lax.rsqrt(ms + NORM_EPS) * fg_ref[...]


def _combine(ys, x2d, gt, prob, final_g, tiles_per_batch, tc):
    n, d = x2d.shape
    nt = n // tc
    tok = pl.BlockSpec((tc, d), lambda i: (i, 0))
    return pl.pallas_call(
        _combine_kernel,
        grid=(nt,),
        in_specs=[tok, pl.BlockSpec((tc, d), lambda i: (nt + i, 0)), tok,
                  pl.BlockSpec((1, 1, d), lambda i: (i // tiles_per_batch, 0, 0)),
                  pl.BlockSpec((tc, TOP_K), lambda i: (i, 0)),
                  pl.BlockSpec((1, d), lambda i: (0, 0))],
        out_specs=tok,
        out_shape=jax.ShapeDtypeStruct((n, d), F32),
        compiler_params=pltpu.CompilerParams(
            dimension_semantics=("arbitrary",), vmem_limit_bytes=VMEM_LIMIT),
        name="moe_combine",
    )(ys, ys, x2d, gt[:, None, :], prob, final_g[None, :])


def _moe_layout(eid, rank, counts, n_experts, tm, n_tiles):
    n = eid.shape[0]
    rows = n_tiles * tm
    cnt = counts[0, :n_experts]
    tiles = (cnt + tm - 1) // tm
    tile_end = jnp.cumsum(tiles)
    offsets = (tile_end - tiles) * tm
    first_rank = jnp.cumsum(cnt) - cnt
    pos = (offsets[eid] + rank).reshape(-1)
    order = jnp.argsort(pos).astype(I32)
    tile_ids = jnp.arange(n_tiles, dtype=I32)
    last_id = jnp.minimum(tile_ids, tile_end[-1] - 1)
    gid = jnp.sum((last_id[:, None] >= tile_end[None, :]).astype(I32), axis=1)
    p = jnp.arange(rows, dtype=I32)
    e_of_p = jnp.repeat(gid, tm)
    within = p - offsets[e_of_p]
    valid = within < cnt[e_of_p]
    flat = order[jnp.clip(first_rank[e_of_p] + within, 0, TOP_K * n - 1)]
    tok = flat // TOP_K
    src = jnp.where(valid, tok, 0)
    pad_id = jnp.cumsum(jnp.logical_not(valid).astype(I32)) - 1
    dst = jnp.where(valid, (flat % TOP_K) * n + tok, TOP_K * n + pad_id)
    spare = rows
    src = jnp.concatenate([src, jnp.zeros((tm,), I32)]).reshape(n_tiles + 1, 1, tm)
    dst = jnp.concatenate([spare + jnp.arange(tm, dtype=I32), dst]).reshape(n_tiles + 1, 1, tm)
    return gid, src.astype(I32), dst.astype(I32), spare + 2 * tm, spare + tm


def kernel(x, c, ada_w, ada_b, norm_g, final_g, rwkv_mu, rwkv_w_rkv, rwkv_w_o, rwkv_w0, rwkv_w1, rwkv_w2, rwkv_a0, rwkv_a1, rwkv_a2, rwkv_g1, rwkv_g2, rwkv_k_k, rwkv_k_a, rwkv_r_k, rwkv_gn_w, rwkv_gn_b, lru_w_in, lru_conv_w, lru_conv_b, lru_w_gates, lru_b_gates, lru_lam, lru_w_out, ffn_w_gu, ffn_w_d, moe_w_router, moe_b_router, moe_w_gu, moe_w_d):
    b, t, d = x.shape
    n = b * t
    n_experts = moe_w_router.shape[-1]
    mod = _ada_mod(c, ada_w, ada_b)

    def mods(i):
        return [mod[i, :, q * d:(q + 1) * d] for q in range(6)]

    sh1, sc1, gt1, sh2, sc2, gt2 = mods(0)
    r, lw, k2, v, kk, bv, g, bonus = _rwkv_pre(
        x, sh1, sc1, norm_g[0, 0], rwkv_mu[0], rwkv_w_rkv[0], rwkv_w1[0], rwkv_w2[0],
        rwkv_a1[0], rwkv_a2[0], rwkv_g1[0], rwkv_g2[0], rwkv_w0[0], rwkv_a0[0],
        rwkv_k_k[0], rwkv_k_a[0], rwkv_r_k[0], tm=min(256, t))
    yg = _wkv_scan(r, lw, k2, v, kk, bv, g, bonus, rwkv_gn_w[0], rwkv_gn_b[0],
                   chunks_per_step=min(2, t // CHUNK))
    x = _ffn_dense(yg, x, gt1, sh2, sc2, gt2, norm_g[0, 1], rwkv_w_o[0], ffn_w_gu[0], ffn_w_d[0],
                   tm=min(256, t))

    sh1, sc1, gt1, sh2, sc2, gt2 = mods(1)
    x = _lru_block(x, sh1, sc1, gt1, norm_g[1, 0], lru_w_in[0], lru_conv_w[0], lru_conv_b[0],
                   lru_w_gates[0], lru_b_gates[0], lru_lam[0], lru_w_out[0], tm=min(256, t))

    x2d = x.reshape(n, d)
    tm_r = min(512, t)
    h2, eid, rank, prob, counts = _router(x2d, sh2, sc2, norm_g[1, 1], moe_w_router[0],
                                          moe_b_router[0], t // tm_r, tm_r)
    tm_g = min(256, t)
    n_tiles = (TOP_K * n) // tm_g + n_experts
    gid, src, dst, out_rows, spare_row = _moe_layout(eid, rank, counts, n_experts, tm_g, n_tiles)
    ys = _moe_ffn(gid, src, dst, h2, moe_w_gu[0], moe_w_d[0], out_rows, spare_row, tm_g)
    tc = min(512, t)
    out = _combine(ys, x2d, gt2, prob, final_g, t // tc, tc)
    return out.reshape(b, t, d)
```

```python
import functools

import jax
import jax.numpy as jnp
from jax import lax
from jax.experimental import pallas as pl
from jax.experimental.pallas import tpu as pltpu

F32 = jnp.float32
BF16 = jnp.bfloat16
I32 = jnp.int32

HEAD = 64
CHUNK = 64
GROUP = 256
HEADS_PER_GROUP = GROUP // HEAD
GN_EPS = 64e-5
NORM_EPS = 1e-6
LRU_C = 8.0
CONV_WIDTH = 4
LRU_BLOCK = 256
TOP_K = 2
LANES = 128
MOE_WEIGHT_CHUNK = 512
VMEM_LIMIT = 56 * 1024 * 1024


def _dot(a, b):
    return jnp.dot(a, b, preferred_element_type=F32)


def _dot_nt(a, b):
    return lax.dot_general(a, b, (((1,), (1,)), ((), ())), preferred_element_type=F32)


def _dot_tn(a, b):
    return lax.dot_general(a, b, (((0,), (0,)), ((), ())), preferred_element_type=F32)


def _softplus(u):
    return jnp.maximum(u, 0.0) + jnp.log1p(jnp.exp(-jnp.abs(u)))


def _modulate(x, ng, sh, sc):
    ms = jnp.mean(x * x, axis=-1, keepdims=True)
    return x * lax.rsqrt(ms + NORM_EPS) * ng * (1.0 + sc) + sh


def _split_bf16(x):
    hi = x.astype(BF16)
    lo = (x - hi.astype(F32)).astype(BF16)
    return hi, lo


def _mod_kernel(c_ref, w_ref, b_ref, o_ref):
    c = c_ref[...]
    cond = c * jax.nn.sigmoid(c)
    o_ref[0] = _dot(cond.astype(BF16), w_ref[0].astype(BF16)) + b_ref[0]


def _ada_mod(c, ada_w, ada_b):
    depth, d, d6 = ada_w.shape
    b = c.shape[0]
    rows = 8
    c8 = jnp.pad(c, ((0, rows - b), (0, 0)))
    tn = 1024
    out = pl.pallas_call(
        _mod_kernel,
        grid=(depth, d6 // tn),
        in_specs=[
            pl.BlockSpec((rows, d), lambda i, j: (0, 0)),
            pl.BlockSpec((1, d, tn), lambda i, j: (i, 0, j)),
            pl.BlockSpec((1, 1, tn), lambda i, j: (i, 0, j)),
        ],
        out_specs=pl.BlockSpec((1, rows, tn), lambda i, j: (i, 0, j)),
        out_shape=jax.ShapeDtypeStruct((depth, rows, d6), F32),
        name="ada_mod",
    )(c8, ada_w, ada_b.reshape(depth, 1, d6))
    return out[:, :b]


def _rwkv_pre_kernel(x_ref, sh_ref, sc_ref, ng_ref, mu_ref, wrkv_ref, w1_ref, w2_ref,
                     a1_ref, a2_ref, g1_ref, g2_ref, vec_ref, seg_ref, segt_ref,
                     r_out, lw_out, k_out, v_out, kk_out, b_out, g_out, bonus_out,
                     hbuf):
    t = pl.program_id(1)
    tm = x_ref.shape[1]
    d = x_ref.shape[2]

    h = _modulate(x_ref[0], ng_ref[...], sh_ref[0], sc_ref[0])

    @pl.when(t == 0)
    def _():
        hbuf[...] = jnp.zeros((8, d), F32)

    rr = lax.broadcasted_iota(I32, (tm, tm), 0)
    cc = lax.broadcasted_iota(I32, (tm, tm), 1)
    hprev = _dot((cc == rr - 1).astype(BF16), h.astype(BF16))
    first = lax.broadcasted_iota(I32, (tm, d), 0) == 0
    hprev = jnp.where(first, hbuf[7:8, :], hprev)
    hbuf[...] = h[tm - 8:tm, :]
    xx = hprev - h

    def mix(p):
        return (h + xx * mu_ref[p:p + 1, :]).astype(BF16)

    r = _dot(mix(0), wrkv_ref[0])
    k = _dot(mix(1), wrkv_ref[1])
    v = _dot(mix(2), wrkv_ref[2])
    wl = _dot(jnp.tanh(_dot(mix(3), w1_ref[...])).astype(BF16), w2_ref[...])
    al = _dot(_dot(mix(4), a1_ref[...]).astype(BF16), a2_ref[...])
    g = _dot(jax.nn.sigmoid(_dot(mix(5), g1_ref[...])).astype(BF16), g2_ref[...])

    w0 = vec_ref[0:1, :]
    a0 = vec_ref[1:2, :]
    k_k = vec_ref[2:3, :]
    k_a = vec_ref[3:4, :]
    r_k = vec_ref[4:5, :]

    def headsum(z):
        s = _dot(z.astype(BF16), seg_ref[...])
        s_hi, s_lo = _split_bf16(s)
        return _dot(jnp.concatenate([s_hi, s_lo], axis=1), segt_ref[...])

    lw = -0.6065306597126334 * jax.nn.sigmoid(w0 + wl)
    a = jax.nn.sigmoid(a0 + al)
    kk = k * k_k
    kk = kk * lax.rsqrt(jnp.maximum(headsum(kk * kk), 1e-24))
    k2 = k * (1.0 + (a - 1.0) * k_a)
    bonus = headsum(r * k2 * r_k) * v

    r_out[0] = r
    lw_out[0] = lw
    k_out[0] = k2
    v_out[0] = v
    kk_out[0] = kk
    b_out[0] = kk * a
    g_out[0] = g
    bonus_out[0] = bonus


def _pad_to(x, axis, size):
    pad = [(0, 0)] * x.ndim
    pad[axis] = (0, size - x.shape[axis])
    return jnp.pad(x, pad)


def _rwkv_pre(x, sh, sc, ng, mu, w_rkv, w1, w2, a1, a2, g1, g2, w0, a0, k_k, k_a, r_k, tm):
    b, t, d = x.shape
    nh = d // HEAD
    lw_pad = LANES * pl.cdiv(w1.shape[1], LANES)
    la_pad = LANES * pl.cdiv(a1.shape[1], LANES)
    lg_pad = LANES * pl.cdiv(g1.shape[1], LANES)
    vecs = _pad_to(jnp.stack([w0, a0, k_k, k_a, r_k.reshape(d)]), 0, 8)
    head_of_lane = jnp.arange(d) // HEAD
    seg = (head_of_lane[:, None] == jnp.arange(LANES)[None, :]).astype(BF16)
    segt = jnp.concatenate([seg.T, seg.T], axis=0)
    del nh
    full = lambda *shape: pl.BlockSpec(shape, lambda bi, ti: (0,) * len(shape))
    tok = pl.BlockSpec((1, tm, d), lambda bi, ti: (bi, ti, 0))
    vec = pl.BlockSpec((1, 1, d), lambda bi, ti: (bi, 0, 0))
    outs = pl.pallas_call(
        _rwkv_pre_kernel,
        grid=(b, t // tm),
        in_specs=[tok, vec, vec, full(1, d), full(8, d), full(3, d, d),
                  full(d, lw_pad), full(lw_pad, d), full(d, la_pad), full(la_pad, d),
                  full(d, lg_pad), full(lg_pad, d), full(8, d), full(d, LANES), full(2 * LANES, d)],
        out_specs=[tok] * 8,
        out_shape=[jax.ShapeDtypeStruct((b, t, d), F32)] * 8,
        scratch_shapes=[pltpu.VMEM((8, d), F32)],
        compiler_params=pltpu.CompilerParams(
            dimension_semantics=("arbitrary", "arbitrary"), vmem_limit_bytes=VMEM_LIMIT),
        name="rwkv_pre",
    )(x, sh[:, None, :], sc[:, None, :], ng[None, :], _pad_to(mu, 0, 8), w_rkv.astype(BF16),
      _pad_to(w1, 1, lw_pad).astype(BF16), _pad_to(w2, 0, lw_pad).astype(BF16),
      _pad_to(a1, 1, la_pad).astype(BF16), _pad_to(a2, 0, la_pad).astype(BF16),
      _pad_to(g1, 1, lg_pad).astype(BF16), _pad_to(g2, 0, lg_pad).astype(BF16),
      vecs, seg, segt)
    return outs


def _wkv_scan_kernel(r_ref, lw_ref, k_ref, v_ref, kk_ref, b_ref, g_ref, bonus_ref,
                     gnw_ref, gnb_ref, o_ref, h_scr, *, chunks_per_step):
    L = CHUNK
    W = GROUP
    ngroups = r_ref.shape[2] // W

    @pl.when(pl.program_id(1) == 0)
    def _():
        h_scr[...] = jnp.zeros_like(h_scr)

    row = lax.broadcasted_iota(I32, (L, W), 0)
    lane = lax.broadcasted_iota(I32, (L, W), 1)
    sidx = lane & (L - 1)
    lane_head = lane >> 6
    strict = sidx < row
    incl = sidx <= row
    eye = (sidx == row).astype(F32)
    same16 = (row >> 4) == (sidx >> 4)
    same32 = (row >> 5) == (sidx >> 5)
    m16 = strict & same16
    m32 = strict & same32 & jnp.logical_not(same16)
    m64 = strict & jnp.logical_not(same32)
    rb = lax.broadcasted_iota(I32, (W, W), 0)
    cb = lax.broadcasted_iota(I32, (W, W), 1)
    bmask = (rb >> 6) == (cb >> 6)
    diag = rb == cb
    ones_bd = bmask.astype(BF16)
    tri_r = lax.broadcasted_iota(I32, (L, 3 * L), 0)
    tri_c = lax.broadcasted_iota(I32, (L, 3 * L), 1)
    tri3 = ((tri_c & (L - 1)) <= tri_r).astype(BF16)

    def bd(y):
        yt = jnp.concatenate([y] * HEADS_PER_GROUP, axis=0)
        return jnp.where(bmask, yt, 0.0).astype(BF16)

    def hmm(x, ybd):
        return _dot(x.astype(BF16), ybd)

    streams = [(q, j) for j in range(chunks_per_step) for q in range(ngroups)]
    S = range(len(streams))

    def ld(ref, s):
        q, j = streams[s]
        return ref[0, j * L:(j + 1) * L, q * W:(q + 1) * W]

    r = [ld(r_ref, s) for s in S]
    lw = [ld(lw_ref, s) for s in S]
    k = [ld(k_ref, s) for s in S]
    v = [ld(v_ref, s) for s in S]
    kk = [ld(kk_ref, s) for s in S]
    bv = [ld(b_ref, s) for s in S]

    def cumlog(x):
        hi = x.astype(BF16)
        rem = x - hi.astype(F32)
        mid = rem.astype(BF16)
        lo = (rem - mid.astype(F32)).astype(BF16)
        return _dot(tri3, jnp.concatenate([hi, mid, lo], axis=0))

    cl = [cumlog(lw[s]) for s in S]
    cl_last = [cl[s][L - 1:L, :] for s in S]
    e_pos = [jnp.exp(cl[s]) for s in S]
    e_neg = [jnp.exp(-cl[s]) for s in S]
    e_end = [jnp.exp(cl_last[s] - cl[s]) for s in S]
    rt = [r[s] * e_pos[s] for s in S]
    at = [-kk[s] * jnp.exp(cl[s] - lw[s]) for s in S]
    bt = [bv[s] * e_neg[s] for s in S]
    kt = [k[s] * e_neg[s] for s in S]

    def gram(s):
        x = jnp.concatenate([at[s], rt[s]], axis=0).astype(BF16)
        ys = [jnp.where(lane_head == hh, bt[s], 0.0) for hh in range(HEADS_PER_GROUP)]
        ys += [jnp.where(lane_head == hh, kt[s], 0.0) for hh in range(HEADS_PER_GROUP)]
        return _dot_nt(x, jnp.concatenate(ys, axis=0).astype(BF16))

    gm = [gram(s) for s in S]
    a_ab = [jnp.where(strict, gm[s][:L, :W], 0.0) for s in S]
    a_ak = [jnp.where(strict, gm[s][:L, W:], 0.0) for s in S]
    a_rb = [jnp.where(incl, gm[s][L:, :W], 0.0) for s in S]
    a_rk = [jnp.where(incl, gm[s][L:, W:], 0.0) for s in S]

    a0 = [jnp.where(m16, a_ab[s], 0.0) for s in S]
    pw = [hmm(a0[s], bd(a0[s])) for s in S]
    tinv = [eye + a0[s] for s in S]
    for _ in range(2):
        ts = [_dot(jnp.concatenate([tinv[s], pw[s]], axis=0).astype(BF16), bd(pw[s])) for s in S]
        tinv = [tinv[s] + ts[s][:L] for s in S]
        pw = [ts[s][L:] for s in S]
    tinv = [tinv[s] + hmm(tinv[s], bd(pw[s])) for s in S]
    for msk in (m32, m64):
        inner = [hmm(jnp.where(msk, a_ab[s], 0.0), bd(tinv[s])) for s in S]
        tinv = [tinv[s] + hmm(tinv[s], bd(inner[s])) for s in S]

    vbd = [bd(v[s]) for s in S]
    av = [hmm(a_ak[s], vbd[s]) for s in S]
    tx = [_dot(tinv[s].astype(BF16), jnp.concatenate([bd(at[s]), bd(av[s])], axis=1)) for s in S]
    ahat = [tx[s][:, :W] for s in S]
    vp = [tx[s][:, W:] for s in S]
    ox = [_dot(a_rb[s].astype(BF16), jnp.concatenate([bd(ahat[s]), bd(vp[s])], axis=1)) for s in S]
    rhat = [rt[s] + ox[s][:, :W] for s in S]
    o_intra = [ox[s][:, W:] + hmm(a_rk[s], vbd[s]) for s in S]

    def state_terms(s):
        z = jnp.concatenate([bv[s] * e_end[s], k[s] * e_end[s]], axis=0).astype(BF16)
        wm = jnp.concatenate(
            [jnp.concatenate([ahat[s], vp[s]], axis=1),
             jnp.concatenate([jnp.zeros((L, W), F32), v[s]], axis=1)], axis=0).astype(BF16)
        mn = _dot_tn(z, wm)
        m_mat = jnp.where(bmask, mn[:, :W], 0.0) + jnp.where(diag, jnp.exp(cl_last[s]), 0.0)
        return m_mat, jnp.where(bmask, mn[:, W:], 0.0)

    mn = [state_terms(s) for s in S]

    o = [None] * len(streams)
    for q in range(ngroups):
        h = h_scr[q]
        for j in range(chunks_per_step):
            s = streams.index((q, j))
            m_hi, m_lo = _split_bf16(mn[s][0])
            lhs = jnp.concatenate([m_hi, m_lo, rhat[s].astype(BF16)], axis=0)
            res = _dot(lhs, h.astype(BF16))
            o[s] = res[2 * W:] + o_intra[s]
            h = res[:W] + res[W:2 * W] + mn[s][1]
        h_scr[q] = h

    def headmean(zs):
        parts = []
        for z in zs:
            parts += list(_split_bf16(z))
        red = _dot(jnp.concatenate(parts, axis=0), ones_bd) * (1.0 / HEAD)
        return [red[2 * L * s:2 * L * s + L] + red[2 * L * s + L:2 * L * (s + 1)] for s in S]

    mean = headmean(o)
    dlt = [o[s] - mean[s] for s in S]
    var = headmean([dlt[s] * dlt[s] for s in S])
    for s in S:
        q, j = streams[s]
        gsl = slice(q * W, (q + 1) * W)
        rsl = slice(j * L, (j + 1) * L)
        yn = dlt[s] * lax.rsqrt(var[s] + GN_EPS) * gnw_ref[:, gsl] + gnb_ref[:, gsl]
        o_ref[0, rsl, gsl] = ((yn + bonus_ref[0, rsl, gsl]) * g_ref[0, rsl, gsl]).astype(o_ref.dtype)


def _wkv_scan(r, lw, k, v, kk, bv, g, bonus, gn_w, gn_b, chunks_per_step):
    b, t, d = r.shape
    lb = CHUNK * chunks_per_step
    tok = pl.BlockSpec((1, lb, d), lambda bi, ci: (bi, ci, 0))
    vec = pl.BlockSpec((1, d), lambda bi, ci: (0, 0))
    return pl.pallas_call(
        functools.partial(_wkv_scan_kernel, chunks_per_step=chunks_per_step),
        grid=(b, t // lb),
        in_specs=[tok] * 8 + [vec, vec],
        out_specs=tok,
        out_shape=jax.ShapeDtypeStruct((b, t, d), BF16),
        scratch_shapes=[pltpu.VMEM((d // GROUP, GROUP, GROUP), F32)],
        compiler_params=pltpu.CompilerParams(
            dimension_semantics=("arbitrary", "arbitrary"), vmem_limit_bytes=VMEM_LIMIT),
        name="wkv_scan",
    )(r, lw, k, v, kk, bv, g, bonus, gn_w[None, :], gn_b[None, :])


def _swiglu(h, wg, wu, wd):
    g = _dot(h, wg)
    u = _dot(h, wu)
    return _dot((g * jax.nn.sigmoid(g) * u).astype(BF16), wd)


def _ffn_dense_kernel(a_ref, x_ref, gt1_ref, sh_ref, sc_ref, gt2_ref, ng_ref, wo_ref,
                      wg_ref, wu_ref, wd_ref, o_ref):
    x1 = x_ref[0] + (1.0 + gt1_ref[0]) * _dot(a_ref[0], wo_ref[...])
    h = _modulate(x1, ng_ref[...], sh_ref[0], sc_ref[0]).astype(BF16)
    o_ref[0] = x1 + (1.0 + gt2_ref[0]) * _swiglu(h, wg_ref[...], wu_ref[...], wd_ref[...])


def _resident(shape, index_map):
    return pl.BlockSpec(shape, index_map, pipeline_mode=pl.Buffered(1))


def _ffn_dense(a, x, gt1, sh, sc, gt2, ng, w_o, w_gu, w_d, tm):
    b, t, d = x.shape
    f = w_d.shape[0]
    tok = pl.BlockSpec((1, tm, d), lambda bi, ti: (bi, ti, 0))
    vec = pl.BlockSpec((1, 1, d), lambda bi, ti: (bi, 0, 0))
    w_gu_b = w_gu.astype(BF16)
    return pl.pallas_call(
        _ffn_dense_kernel,
        grid=(b, t // tm),
        in_specs=[tok, tok, vec, vec, vec, vec, pl.BlockSpec((1, d), lambda bi, ti: (0, 0)),
                  _resident((d, d), lambda bi, ti: (0, 0)),
                  _resident((d, f), lambda bi, ti: (0, 0)),
                  _resident((d, f), lambda bi, ti: (0, 1)),
                  _resident((f, d), lambda bi, ti: (0, 0))],
        out_specs=tok,
        out_shape=jax.ShapeDtypeStruct((b, t, d), F32),
        compiler_params=pltpu.CompilerParams(
            dimension_semantics=("arbitrary", "arbitrary"), vmem_limit_bytes=VMEM_LIMIT),
        name="ffn_dense",
    )(a, x, gt1[:, None, :], sh[:, None, :], sc[:, None, :], gt2[:, None, :], ng[None, :],
      w_o.astype(BF16), w_gu_b, w_gu_b, w_d.astype(BF16))


def _lru_kernel(x_ref, sh_ref, sc_ref, gt_ref, ng_ref, win_ref, cw_ref, cb_ref, wg_ref, bg_ref,
                lam_ref, wout_ref, o_ref, xbuf, abuf, bbuf, hbuf, ga, gb, carry):
    t = pl.program_id(1)
    tm = x_ref.shape[1]
    w = win_ref.shape[1] // 2
    nblk = w // LRU_BLOCK
    nslab = w // LANES
    ng = tm // 8

    @pl.when(t == 0)
    def _():
        xbuf[0:8, :] = jnp.zeros((8, w), F32)
        carry[...] = jnp.zeros_like(carry)
        ga[:, :, 0:ng, :] = jnp.ones((2, nslab, ng, LANES), F32)
        gb[:, :, 0:ng, :] = jnp.zeros((2, nslab, ng, LANES), F32)

    x = x_ref[0]
    h = _modulate(x, ng_ref[...], sh_ref[0], sc_ref[0]).astype(BF16)
    xg = _dot(h, win_ref[...])
    xb = xg[:, :w]
    gx = xg[:, w:]
    gate = 0.5 * gx * (1.0 + jnp.tanh(0.7978845608028654 * (gx + 0.044715 * gx * gx * gx)))

    xbuf[8:8 + tm, :] = xb
    conv = cb_ref[...] + cw_ref[CONV_WIDTH - 1:CONV_WIDTH, :] * xb
    for jj in range(CONV_WIDTH - 1):
        shift = CONV_WIDTH - 1 - jj
        conv = conv + cw_ref[jj:jj + 1, :] * xbuf[8 - shift:8 - shift + tm, :]
    xbuf[0:8, :] = xb[tm - 8:tm, :]

    conv_b = conv.astype(BF16)
    rs, is_ = [], []
    for n in range(nblk):
        gts = _dot(conv_b[:, n * LRU_BLOCK:(n + 1) * LRU_BLOCK], wg_ref[n]) + bg_ref[n]
        gts = jax.nn.sigmoid(gts)
        rs.append(gts[:, :LRU_BLOCK])
        is_.append(gts[:, LRU_BLOCK:])
    r_t = jnp.concatenate(rs, axis=1)
    i_t = jnp.concatenate(is_, axis=1)

    log_a = -LRU_C * r_t * _softplus(-lam_ref[...])
    a_t = jnp.exp(log_a)
    b_t = jnp.sqrt(-jnp.tanh(log_a) * (a_t * a_t + 1.0)) * (i_t * conv)

    for j in range(nslab):
        abuf[j] = a_t[:, j * LANES:(j + 1) * LANES]
        bbuf[j] = b_t[:, j * LANES:(j + 1) * LANES]
    hs_slabs = []
    for j in range(nslab):
        a_loc = [abuf[j, pl.ds(0, ng, stride=8), :]]
        b_loc = [bbuf[j, pl.ds(0, ng, stride=8), :]]
        for s in range(1, 8):
            a_s = abuf[j, pl.ds(s, ng, stride=8), :]
            b_s = bbuf[j, pl.ds(s, ng, stride=8), :]
            b_loc.append(a_s * b_loc[-1] + b_s)
            a_loc.append(a_s * a_loc[-1])
        ga[0, j, ng:2 * ng, :] = a_loc[-1]
        gb[0, j, ng:2 * ng, :] = b_loc[-1]
        step = 1
        src = 0
        while step < ng:
            a_cur = ga[src, j, ng:2 * ng, :]
            b_cur = gb[src, j, ng:2 * ng, :]
            a_sh = ga[src, j, ng - step:2 * ng - step, :]
            b_sh = gb[src, j, ng - step:2 * ng - step, :]
            ga[1 - src, j, ng:2 * ng, :] = a_cur * a_sh
            gb[1 - src, j, ng:2 * ng, :] = a_cur * b_sh + b_cur
            src = 1 - src
            step *= 2
        c_in = carry[0:1, j * LANES:(j + 1) * LANES]
        h_in = gb[src, j, ng - 1:2 * ng - 1, :] + ga[src, j, ng - 1:2 * ng - 1, :] * c_in
        for s in range(8):
            h_s = b_loc[s] + a_loc[s] * h_in
            hbuf[j, pl.ds(s, ng, stride=8), :] = h_s
        carry[:, j * LANES:(j + 1) * LANES] = jnp.broadcast_to(h_s[ng - 1:ng, :], (8, LANES))
        hs_slabs.append(hbuf[j])
    hs = jnp.concatenate(hs_slabs, axis=1)

    y = _dot((hs * gate).astype(BF16), wout_ref[...])
    o_ref[0] = x + (1.0 + gt_ref[0]) * y


def _lru_block(x, sh, sc, gt, ng, w_in, conv_w, conv_b, w_gates, b_gates, lam, w_out, tm):
    b, t, d = x.shape
    w = w_out.shape[0]
    nblk = w // LRU_BLOCK
    full = lambda *shape: pl.BlockSpec(shape, lambda bi, ti: (0,) * len(shape))
    tok = pl.BlockSpec((1, tm, d), lambda bi, ti: (bi, ti, 0))
    vec = pl.BlockSpec((1, 1, d), lambda bi, ti: (bi, 0, 0))
    return pl.pallas_call(
        _lru_kernel,
        grid=(b, t // tm),
        in_specs=[tok, vec, vec, vec, full(1, d), full(d, 2 * w), full(CONV_WIDTH, w), full(1, w),
                  full(nblk, LRU_BLOCK, 2 * LRU_BLOCK), full(nblk, 1, 2 * LRU_BLOCK),
                  full(1, w), full(w, d)],
        out_specs=tok,
        out_shape=jax.ShapeDtypeStruct((b, t, d), F32),
        scratch_shapes=[pltpu.VMEM((tm + 8, w), F32)]
        + [pltpu.VMEM((w // LANES, tm, LANES), F32)] * 3
        + [pltpu.VMEM((2, w // LANES, tm // 4, LANES), F32)] * 2
        + [pltpu.VMEM((8, w), F32)],
        compiler_params=pltpu.CompilerParams(
            dimension_semantics=("arbitrary", "arbitrary"), vmem_limit_bytes=VMEM_LIMIT),
        name="rglru_block",
    )(x, sh[:, None, :], sc[:, None, :], gt[:, None, :], ng[None, :], w_in.astype(BF16),
      conv_w, conv_b[None, :], w_gates.astype(BF16), b_gates[:, None, :], lam[None, :],
      w_out.astype(BF16))


def _router_kernel(x_ref, sh_ref, sc_ref, ng_ref, wr_ref, br_ref,
                   h_out, eid_out, rank_out, prob_out, cnt_out, cnt_scr):
    i = pl.program_id(0)
    tm = x_ref.shape[0]

    @pl.when(i == 0)
    def _():
        cnt_scr[...] = jnp.zeros_like(cnt_scr)

    h = _modulate(x_ref[...], ng_ref[...], sh_ref[0], sc_ref[0])
    h_out[...] = h
    h_hi, h_lo = _split_bf16(h)
    w_hi, w_lo = _split_bf16(wr_ref[...])
    logits = _dot(h_hi, w_hi) + _dot(h_lo, w_hi) + _dot(h_hi, w_lo) + br_ref[...]

    lane = lax.broadcasted_iota(I32, logits.shape, 1)
    m1 = jnp.max(logits, axis=-1, keepdims=True)
    i1 = jnp.min(jnp.where(logits == m1, lane, LANES), axis=-1, keepdims=True)
    l2 = jnp.where(lane == i1, -jnp.inf, logits)
    m2 = jnp.max(l2, axis=-1, keepdims=True)
    i2 = jnp.min(jnp.where(l2 == m2, lane, LANES), axis=-1, keepdims=True)
    e = jnp.exp(m2 - m1)
    p1 = 1.0 / (1.0 + e)
    p2 = e / (1.0 + e)

    oh1 = (lane == i1).astype(F32)
    oh2 = (lane == i2).astype(F32)
    oh = oh1 + oh2
    rr = lax.broadcasted_iota(I32, (tm, tm), 0)
    cc = lax.broadcasted_iota(I32, (tm, tm), 1)
    tri = (cc < rr).astype(BF16)
    before = _dot(tri, oh.astype(BF16)) + cnt_scr[0:1, :]
    rank1 = jnp.sum(before * oh1, axis=-1, keepdims=True)
    rank2 = jnp.sum(before * oh2, axis=-1, keepdims=True)
    cnt_scr[...] = cnt_scr[...] + jnp.sum(oh, axis=0, keepdims=True)

    eid_out[:, 0:1] = i1
    eid_out[:, 1:2] = i2
    rank_out[:, 0:1] = rank1.astype(I32)
    rank_out[:, 1:2] = rank2.astype(I32)
    prob_out[:, 0:1] = p1
    prob_out[:, 1:2] = p2
    cnt_out[...] = cnt_scr[...].astype(I32)


def _router(x2d, sh, sc, ng, w_router, b_router, tiles_per_batch, tm):
    n, d = x2d.shape
    e = w_router.shape[1]
    wr = _pad_to(w_router, 1, LANES)
    br = jnp.concatenate([b_router, jnp.full((LANES - e,), -1e30, F32)])[None, :]
    tok = pl.BlockSpec((tm, d), lambda i: (i, 0))
    vec = pl.BlockSpec((1, 1, d), lambda i: (i // tiles_per_batch, 0, 0))
    two = pl.BlockSpec((tm, TOP_K), lambda i: (i, 0))
    return pl.pallas_call(
        _router_kernel,
        grid=(n // tm,),
        in_specs=[tok, vec, vec, pl.BlockSpec((1, d), lambda i: (0, 0)),
                  pl.BlockSpec((d, LANES), lambda i: (0, 0)), pl.BlockSpec((1, LANES), lambda i: (0, 0))],
        out_specs=[tok, two, two, two, pl.BlockSpec((8, LANES), lambda i: (0, 0))],
        out_shape=[jax.ShapeDtypeStruct((n, d), F32), jax.ShapeDtypeStruct((n, TOP_K), I32),
                   jax.ShapeDtypeStruct((n, TOP_K), I32), jax.ShapeDtypeStruct((n, TOP_K), F32),
                   jax.ShapeDtypeStruct((8, LANES), I32)],
        scratch_shapes=[pltpu.VMEM((8, LANES), F32)],
        compiler_params=pltpu.CompilerParams(
            dimension_semantics=("arbitrary",), vmem_limit_bytes=VMEM_LIMIT),
        name="moe_router",
    )(x2d, sh[:, None, :], sc[:, None, :], ng[None, :], wr, br)


def _moe_ffn_kernel(gid_ref, src0_ref, src_next_ref, dst_prev_ref, h_hbm, wgu_hbm, wd_hbm,
                    ys_hbm, x0, x1, y0, y1, wg_b, wu_b, wd_b, stage_gu, stage_d, gsem, ssem, wsem,
                    *, n_tiles, tm, spare_row):
    i = pl.program_id(0)
    xbufs = (x0, x1)
    ybufs = (y0, y1)
    f = wd_b.shape[0]
    wc = stage_gu.shape[2]

    def load_expert(e):
        chunks = []
        for c in range(f // wc):
            chunks.append((wgu_hbm.at[e, :, pl.ds(c * wc, wc)], stage_gu, wg_b, (slice(None), slice(c * wc, (c + 1) * wc))))
            chunks.append((wgu_hbm.at[e, :, pl.ds(f + c * wc, wc)], stage_gu, wu_b, (slice(None), slice(c * wc, (c + 1) * wc))))
        for c in range(f // wc):
            chunks.append((wd_hbm.at[e, pl.ds(c * wc, wc), :], stage_d, wd_b, (slice(c * wc, (c + 1) * wc), slice(None))))
        next_slot = {}
        copies = []
        for n_c, (src, stage, _, _) in enumerate(chunks):
            slot = next_slot.get(id(stage), 0)
            next_slot[id(stage)] = 1 - slot
            copies.append((pltpu.make_async_copy(src, stage.at[slot], wsem.at[n_c % 2]), stage, slot))
        copies[0][0].start()
        for n_c, (_, _, dest, where) in enumerate(chunks):
            if n_c + 1 < len(chunks):
                copies[n_c + 1][0].start()
            copy, stage, slot = copies[n_c]
            copy.wait()
            dest[where] = stage[slot].astype(BF16)

    def gather(src_ref, xbuf):
        for t in range(tm):
            pltpu.make_async_copy(h_hbm.at[pl.ds(src_ref[0, 0, t], 1)], xbuf.at[pl.ds(t, 1)],
                                  gsem).start()

    def scatter(ybuf):
        for t in range(tm):
            pltpu.make_async_copy(ybuf.at[pl.ds(t, 1)],
                                  ys_hbm.at[pl.ds(dst_prev_ref[0, 0, t], 1)], ssem).start()

    def wait_gather(xbuf):
        pltpu.make_async_copy(h_hbm.at[pl.ds(0, tm)], xbuf, gsem).wait()

    def wait_scatter(ybuf):
        pltpu.make_async_copy(ybuf, ys_hbm.at[pl.ds(0, tm)], ssem).wait()

    @pl.when(i == 0)
    def _():
        y0[...] = jnp.zeros_like(y0)
        y1[...] = jnp.zeros_like(y1)
        pltpu.make_async_copy(y0, ys_hbm.at[pl.ds(spare_row, tm)], ssem).start()
        gather(src0_ref, x0)

    for par in range(2):
        @pl.when((i < n_tiles) & (i % 2 == par))
        def _(par=par):
            xa, xb, ya, yb = xbufs[par], xbufs[1 - par], ybufs[par], ybufs[1 - par]
            wait_gather(xa)
            wait_scatter(ya)
            expert = gid_ref[i]

            @pl.when((i == 0) | (expert != gid_ref[jnp.maximum(i - 1, 0)]))
            def _():
                load_expert(expert)

            gather(src_next_ref, xb)
            scatter(yb)
            ya[...] = _swiglu(xa[...].astype(BF16), wg_b[...], wu_b[...], wd_b[...])

    @pl.when(i == n_tiles)
    def _():
        par = n_tiles % 2
        wait_gather(xbufs[par])
        wait_scatter(ybufs[par])
        scatter(ybufs[1 - par])
        wait_scatter(ybufs[1 - par])


def _moe_ffn(gid, src, dst, h2d, w_gu, w_d, out_rows, spare_row, tm):
    n_tiles = gid.shape[0]
    _, d = h2d.shape
    f = w_d.shape[1]
    idx = lambda fn: pl.BlockSpec((1, 1, tm), fn, memory_space=pltpu.SMEM)
    wc = MOE_WEIGHT_CHUNK
    grid_spec = pltpu.PrefetchScalarGridSpec(
        num_scalar_prefetch=1,
        grid=(n_tiles + 1,),
        in_specs=[idx(lambda i, gid: (0, 0, 0)),
                  idx(lambda i, gid: (jnp.minimum(i + 1, n_tiles), 0, 0)),
                  idx(lambda i, gid: (i, 0, 0)),
                  pl.BlockSpec(memory_space=pl.ANY), pl.BlockSpec(memory_space=pl.ANY),
                  pl.BlockSpec(memory_space=pl.ANY)],
        out_specs=pl.BlockSpec(memory_space=pl.ANY),
        scratch_shapes=[pltpu.VMEM((tm, d), F32)] * 4
        + [pltpu.VMEM((d, f), BF16), pltpu.VMEM((d, f), BF16), pltpu.VMEM((f, d), BF16),
           pltpu.VMEM((2, d, wc), F32), pltpu.VMEM((2, wc, d), F32),
           pltpu.SemaphoreType.DMA(()), pltpu.SemaphoreType.DMA(()), pltpu.SemaphoreType.DMA((2,))],
    )
    return pl.pallas_call(
        functools.partial(_moe_ffn_kernel, n_tiles=n_tiles, tm=tm, spare_row=spare_row),
        grid_spec=grid_spec,
        out_shape=jax.ShapeDtypeStruct((out_rows, d), F32),
        compiler_params=pltpu.CompilerParams(
            dimension_semantics=("arbitrary",), vmem_limit_bytes=VMEM_LIMIT),
        name="moe_ffn",
    )(gid, src, src, dst, h2d, w_gu, w_d)


def _combine_kernel(y0_ref, y1_ref, x_ref, gt_ref, prob_ref, fg_ref, o_ref):
    y = prob_ref[:, 0:1] * y0_ref[...] + prob_ref[:, 1:2] * y1_ref[...]
    x = x_ref[...] + (1.0 + gt_ref[0]) * y
    ms = jnp.mean(x * x, axis=-1, keepdims=True)
    o_ref[...] = x * lax.rsqrt(ms + NORM_EPS) * fg_ref[...]


def _combine(ys, x2d, gt, prob, final_g, tiles_per_batch, tc):
    n, d = x2d.shape
    nt = n // tc
    tok = pl.BlockSpec((tc, d), lambda i: (i, 0))
    return pl.pallas_call(
        _combine_kernel,
        grid=(nt,),
        in_specs=[tok, pl.BlockSpec((tc, d), lambda i: (nt + i, 0)), tok,
                  pl.BlockSpec((1, 1, d), lambda i: (i // tiles_per_batch, 0, 0)),
                  pl.BlockSpec((tc, TOP_K), lambda i: (i, 0)),
                  pl.BlockSpec((1, d), lambda i: (0, 0))],
        out_specs=tok,
        out_shape=jax.ShapeDtypeStruct((n, d), F32),
        compiler_params=pltpu.CompilerParams(
            dimension_semantics=("arbitrary",), vmem_limit_bytes=VMEM_LIMIT),
        name="moe_combine",
    )(ys, ys, x2d, gt[:, None, :], prob, final_g[None, :])


def _moe_layout(eid, rank, counts, n_experts, tm, n_tiles):
    n = eid.shape[0]
    rows = n_tiles * tm
    cnt = counts[0, :n_experts]
    tiles = (cnt + tm - 1) // tm
    tile_end = jnp.cumsum(tiles)
    offsets = (tile_end - tiles) * tm
    first_rank = jnp.cumsum(cnt) - cnt
    pos = (offsets[eid] + rank).reshape(-1)
    order = jnp.argsort(pos).astype(I32)
    tile_ids = jnp.arange(n_tiles, dtype=I32)
    last_id = jnp.minimum(tile_ids, tile_end[-1] - 1)
    gid = jnp.sum((last_id[:, None] >= tile_end[None, :]).astype(I32), axis=1)
    p = jnp.arange(rows, dtype=I32)
    e_of_p = jnp.repeat(gid, tm)
    within = p - offsets[e_of_p]
    valid = within < cnt[e_of_p]
    flat = order[jnp.clip(first_rank[e_of_p] + within, 0, TOP_K * n - 1)]
    tok = flat // TOP_K
    src = jnp.where(valid, tok, 0)
    pad_id = jnp.cumsum(jnp.logical_not(valid).astype(I32)) - 1
    dst = jnp.where(valid, (flat % TOP_K) * n + tok, TOP_K * n + pad_id)
    spare = rows
    src = jnp.concatenate([src, jnp.zeros((tm,), I32)]).reshape(n_tiles + 1, 1, tm)
    dst = jnp.concatenate([spare + jnp.arange(tm, dtype=I32), dst]).reshape(n_tiles + 1, 1, tm)
    return gid, src.astype(I32), dst.astype(I32), spare + 2 * tm, spare + tm


def kernel(x, c, ada_w, ada_b, norm_g, final_g, rwkv_mu, rwkv_w_rkv, rwkv_w_o, rwkv_w0, rwkv_w1, rwkv_w2, rwkv_a0, rwkv_a1, rwkv_a2, rwkv_g1, rwkv_g2, rwkv_k_k, rwkv_k_a, rwkv_r_k, rwkv_gn_w, rwkv_gn_b, lru_w_in, lru_conv_w, lru_conv_b, lru_w_gates, lru_b_gates, lru_lam, lru_w_out, ffn_w_gu, ffn_w_d, moe_w_router, moe_b_router, moe_w_gu, moe_w_d):
    b, t, d = x.shape
    n = b * t
    n_experts = moe_w_router.shape[-1]
    mod = _ada_mod(c, ada_w, ada_b)

    def mods(i):
        return [mod[i, :, q * d:(q + 1) * d] for q in range(6)]

    sh1, sc1, gt1, sh2, sc2, gt2 = mods(0)
    r, lw, k2, v, kk, bv, g, bonus = _rwkv_pre(
        x, sh1, sc1, norm_g[0, 0], rwkv_mu[0], rwkv_w_rkv[0], rwkv_w1[0], rwkv_w2[0],
        rwkv_a1[0], rwkv_a2[0], rwkv_g1[0], rwkv_g2[0], rwkv_w0[0], rwkv_a0[0],
        rwkv_k_k[0], rwkv_k_a[0], rwkv_r_k[0], tm=min(256, t))
    yg = _wkv_scan(r, lw, k2, v, kk, bv, g, bonus, rwkv_gn_w[0], rwkv_gn_b[0],
                   chunks_per_step=min(4, t // CHUNK))
    x = _ffn_dense(yg, x, gt1, sh2, sc2, gt2, norm_g[0, 1], rwkv_w_o[0], ffn_w_gu[0], ffn_w_d[0],
                   tm=min(256, t))

    sh1, sc1, gt1, sh2, sc2, gt2 = mods(1)
    x = _lru_block(x, sh1, sc1, gt1, norm_g[1, 0], lru_w_in[0], lru_conv_w[0], lru_conv_b[0],
                   lru_w_gates[0], lru_b_gates[0], lru_lam[0], lru_w_out[0], tm=min(256, t))

    x2d = x.reshape(n, d)
    tm_r = min(512, t)
    h2, eid, rank, prob, counts = _router(x2d, sh2, sc2, norm_g[1, 1], moe_w_router[0],
                                          moe_b_router[0], t // tm_r, tm_r)
    tm_g = min(256, t)
    n_tiles = (TOP_K * n) // tm_g + n_experts
    gid, src, dst, out_rows, spare_row = _moe_layout(eid, rank, counts, n_experts, tm_g, n_tiles)
    ys = _moe_ffn(gid, src, dst, h2, moe_w_gu[0], moe_w_d[0], out_rows, spare_row, tm_g)
    tc = min(512, t)
    out = _combine(ys, x2d, gt2, prob, final_g, t // tc, tc)
    return out.reshape(b, t, d)
```

```python
import functools

import jax
import jax.numpy as jnp
from jax import lax
from jax.experimental import pallas as pl
from jax.experimental.pallas import tpu as pltpu

F32 = jnp.float32
BF16 = jnp.bfloat16
I32 = jnp.int32

HEAD = 64
CHUNK = 64
GROUP = 256
HEADS_PER_GROUP = GROUP // HEAD
GN_EPS = 64e-5
NORM_EPS = 1e-6
LRU_C = 8.0
CONV_WIDTH = 4
LRU_BLOCK = 256
TOP_K = 2
LANES = 128
MOE_WEIGHT_CHUNK = 512
VMEM_LIMIT = 56 * 1024 * 1024


def _dot(a, b):
    return jnp.dot(a, b, preferred_element_type=F32)


def _dot_nt(a, b):
    return lax.dot_general(a, b, (((1,), (1,)), ((), ())), preferred_element_type=F32)


def _dot_tn(a, b):
    return lax.dot_general(a, b, (((0,), (0,)), ((), ())), preferred_element_type=F32)


def _softplus(u):
    return jnp.maximum(u, 0.0) + jnp.log1p(jnp.exp(-jnp.abs(u)))


def _modulate(x, ng, sh, sc):
    ms = jnp.mean(x * x, axis=-1, keepdims=True)
    return x * lax.rsqrt(ms + NORM_EPS) * ng * (1.0 + sc) + sh


def _split_bf16(x):
    hi = x.astype(BF16)
    lo = (x - hi.astype(F32)).astype(BF16)
    return hi, lo


def _shift_select(tm, shifts):
    rr = jnp.arange(tm)[:, None]
    cc = jnp.arange(tm)[None, :]
    return jnp.concatenate([(cc == rr - s) for s in shifts], axis=0).astype(BF16)


def _mod_kernel(c_ref, w_ref, b_ref, o_ref):
    c = c_ref[...]
    cond = c * jax.nn.sigmoid(c)
    o_ref[0] = _dot(cond.astype(BF16), w_ref[0].astype(BF16)) + b_ref[0]


def _ada_mod(c, ada_w, ada_b):
    depth, d, d6 = ada_w.shape
    b = c.shape[0]
    rows = 8
    c8 = jnp.pad(c, ((0, rows - b), (0, 0)))
    tn = 1024
    out = pl.pallas_call(
        _mod_kernel,
        grid=(depth, d6 // tn),
        in_specs=[
            pl.BlockSpec((rows, d), lambda i, j: (0, 0)),
            pl.BlockSpec((1, d, tn), lambda i, j: (i, 0, j)),
            pl.BlockSpec((1, 1, tn), lambda i, j: (i, 0, j)),
        ],
        out_specs=pl.BlockSpec((1, rows, tn), lambda i, j: (i, 0, j)),
        out_shape=jax.ShapeDtypeStruct((depth, rows, d6), F32),
        name="ada_mod",
    )(c8, ada_w, ada_b.reshape(depth, 1, d6))
    return out[:, :b]


def _rwkv_pre_kernel(x_ref, sh_ref, sc_ref, ng_ref, mu_ref, wrkv_ref, w1_ref, w2_ref,
                     a1_ref, a2_ref, g1_ref, g2_ref, vec_ref, seg_ref, segt_ref, sel_ref,
                     r_out, lw_out, k_out, v_out, kk_out, b_out, g_out, bonus_out,
                     hbuf):
    t = pl.program_id(1)
    tm = x_ref.shape[1]
    d = x_ref.shape[2]

    h = _modulate(x_ref[0], ng_ref[...], sh_ref[0], sc_ref[0])

    @pl.when(t == 0)
    def _():
        hbuf[...] = jnp.zeros((8, d), F32)

    hprev = _dot(sel_ref[...], h.astype(BF16))
    first = lax.broadcasted_iota(I32, (tm, d), 0) == 0
    hprev = jnp.where(first, hbuf[7:8, :], hprev)
    hbuf[...] = h[tm - 8:tm, :]
    xx = hprev - h

    def mix(p):
        return (h + xx * mu_ref[p:p + 1, :]).astype(BF16)

    r = _dot(mix(0), wrkv_ref[0])
    k = _dot(mix(1), wrkv_ref[1])
    v = _dot(mix(2), wrkv_ref[2])
    wl = _dot(jnp.tanh(_dot(mix(3), w1_ref[...])).astype(BF16), w2_ref[...])
    al = _dot(_dot(mix(4), a1_ref[...]).astype(BF16), a2_ref[...])
    g = _dot(jax.nn.sigmoid(_dot(mix(5), g1_ref[...])).astype(BF16), g2_ref[...])

    w0 = vec_ref[0:1, :]
    a0 = vec_ref[1:2, :]
    k_k = vec_ref[2:3, :]
    k_a = vec_ref[3:4, :]
    r_k = vec_ref[4:5, :]

    def headsum(z):
        s = _dot(z.astype(BF16), seg_ref[...])
        s_hi, s_lo = _split_bf16(s)
        return _dot(jnp.concatenate([s_hi, s_lo], axis=1), segt_ref[...])

    lw = -0.6065306597126334 * jax.nn.sigmoid(w0 + wl)
    a = jax.nn.sigmoid(a0 + al)
    kk = k * k_k
    kk = kk * lax.rsqrt(jnp.maximum(headsum(kk * kk), 1e-24))
    k2 = k * (1.0 + (a - 1.0) * k_a)
    bonus = headsum(r * k2 * r_k) * v

    r_out[0] = r
    lw_out[0] = lw
    k_out[0] = k2
    v_out[0] = v
    kk_out[0] = kk
    b_out[0] = kk * a
    g_out[0] = g
    bonus_out[0] = bonus


def _pad_to(x, axis, size):
    pad = [(0, 0)] * x.ndim
    pad[axis] = (0, size - x.shape[axis])
    return jnp.pad(x, pad)


def _rwkv_pre(x, sh, sc, ng, mu, w_rkv, w1, w2, a1, a2, g1, g2, w0, a0, k_k, k_a, r_k, tm):
    b, t, d = x.shape
    nh = d // HEAD
    lw_pad = LANES * pl.cdiv(w1.shape[1], LANES)
    la_pad = LANES * pl.cdiv(a1.shape[1], LANES)
    lg_pad = LANES * pl.cdiv(g1.shape[1], LANES)
    vecs = _pad_to(jnp.stack([w0, a0, k_k, k_a, r_k.reshape(d)]), 0, 8)
    head_of_lane = jnp.arange(d) // HEAD
    seg = (head_of_lane[:, None] == jnp.arange(LANES)[None, :]).astype(BF16)
    segt = jnp.concatenate([seg.T, seg.T], axis=0)
    del nh
    full = lambda *shape: pl.BlockSpec(shape, lambda bi, ti: (0,) * len(shape))
    tok = pl.BlockSpec((1, tm, d), lambda bi, ti: (bi, ti, 0))
    vec = pl.BlockSpec((1, 1, d), lambda bi, ti: (bi, 0, 0))
    outs = pl.pallas_call(
        _rwkv_pre_kernel,
        grid=(b, t // tm),
        in_specs=[tok, vec, vec, full(1, d), full(8, d), full(3, d, d),
                  full(d, lw_pad), full(lw_pad, d), full(d, la_pad), full(la_pad, d),
                  full(d, lg_pad), full(lg_pad, d), full(8, d), full(d, LANES), full(2 * LANES, d),
                  full(tm, tm)],
        out_specs=[tok] * 8,
        out_shape=[jax.ShapeDtypeStruct((b, t, d), F32)] * 8,
        scratch_shapes=[pltpu.VMEM((8, d), F32)],
        compiler_params=pltpu.CompilerParams(
            dimension_semantics=("arbitrary", "arbitrary"), vmem_limit_bytes=VMEM_LIMIT),
        name="rwkv_pre",
    )(x, sh[:, None, :], sc[:, None, :], ng[None, :], _pad_to(mu, 0, 8), w_rkv.astype(BF16),
      _pad_to(w1, 1, lw_pad).astype(BF16), _pad_to(w2, 0, lw_pad).astype(BF16),
      _pad_to(a1, 1, la_pad).astype(BF16), _pad_to(a2, 0, la_pad).astype(BF16),
      _pad_to(g1, 1, lg_pad).astype(BF16), _pad_to(g2, 0, lg_pad).astype(BF16),
      vecs, seg, segt, _shift_select(tm, [1]))
    return outs


def _wkv_scan_kernel(r_ref, lw_ref, k_ref, v_ref, kk_ref, b_ref, g_ref, bonus_ref,
                     gnw_ref, gnb_ref, o_ref, h_scr, *, chunks_per_step):
    L = CHUNK
    W = GROUP
    ngroups = r_ref.shape[2] // W

    @pl.when(pl.program_id(1) == 0)
    def _():
        h_scr[...] = jnp.zeros_like(h_scr)

    row = lax.broadcasted_iota(I32, (L, W), 0)
    lane = lax.broadcasted_iota(I32, (L, W), 1)
    sidx = lane & (L - 1)
    lane_head = lane >> 6
    strict = sidx < row
    incl = sidx <= row
    eye = (sidx == row).astype(F32)
    same16 = (row >> 4) == (sidx >> 4)
    same32 = (row >> 5) == (sidx >> 5)
    m16 = strict & same16
    m32 = strict & same32 & jnp.logical_not(same16)
    m64 = strict & jnp.logical_not(same32)
    rb = lax.broadcasted_iota(I32, (W, W), 0)
    cb = lax.broadcasted_iota(I32, (W, W), 1)
    bmask = (rb >> 6) == (cb >> 6)
    diag = rb == cb
    ones_bd = bmask.astype(BF16)
    tri_r = lax.broadcasted_iota(I32, (L, 3 * L), 0)
    tri_c = lax.broadcasted_iota(I32, (L, 3 * L), 1)
    tri3 = ((tri_c & (L - 1)) <= tri_r).astype(BF16)

    def bd(y):
        yt = jnp.concatenate([y] * HEADS_PER_GROUP, axis=0)
        return jnp.where(bmask, yt, 0.0).astype(BF16)

    def hmm(x, ybd):
        return _dot(x.astype(BF16), ybd)

    streams = [(q, j) for j in range(chunks_per_step) for q in range(ngroups)]
    S = range(len(streams))

    def ld(ref, s):
        q, j = streams[s]
        return ref[0, j * L:(j + 1) * L, q * W:(q + 1) * W]

    r = [ld(r_ref, s) for s in S]
    lw = [ld(lw_ref, s) for s in S]
    k = [ld(k_ref, s) for s in S]
    v = [ld(v_ref, s) for s in S]
    kk = [ld(kk_ref, s) for s in S]
    bv = [ld(b_ref, s) for s in S]

    def cumlog(x):
        hi = x.astype(BF16)
        rem = x - hi.astype(F32)
        mid = rem.astype(BF16)
        lo = (rem - mid.astype(F32)).astype(BF16)
        return _dot(tri3, jnp.concatenate([hi, mid, lo], axis=0))

    cl = [cumlog(lw[s]) for s in S]
    cl_last = [cl[s][L - 1:L, :] for s in S]
    e_pos = [jnp.exp(cl[s]) for s in S]
    e_neg = [jnp.exp(-cl[s]) for s in S]
    e_end = [jnp.exp(cl_last[s] - cl[s]) for s in S]
    rt = [r[s] * e_pos[s] for s in S]
    at = [-kk[s] * jnp.exp(cl[s] - lw[s]) for s in S]
    bt = [bv[s] * e_neg[s] for s in S]
    kt = [k[s] * e_neg[s] for s in S]

    def gram(s):
        x = jnp.concatenate([at[s], rt[s]], axis=0).astype(BF16)
        ys = [jnp.where(lane_head == hh, bt[s], 0.0) for hh in range(HEADS_PER_GROUP)]
        ys += [jnp.where(lane_head == hh, kt[s], 0.0) for hh in range(HEADS_PER_GROUP)]
        return _dot_nt(x, jnp.concatenate(ys, axis=0).astype(BF16))

    gm = [gram(s) for s in S]
    a_ab = [jnp.where(strict, gm[s][:L, :W], 0.0) for s in S]
    a_ak = [jnp.where(strict, gm[s][:L, W:], 0.0) for s in S]
    a_rb = [jnp.where(incl, gm[s][L:, :W], 0.0) for s in S]
    a_rk = [jnp.where(incl, gm[s][L:, W:], 0.0) for s in S]

    a0 = [jnp.where(m16, a_ab[s], 0.0) for s in S]
    pw = [hmm(a0[s], bd(a0[s])) for s in S]
    tinv = [eye + a0[s] for s in S]
    for _ in range(2):
        ts = [_dot(jnp.concatenate([tinv[s], pw[s]], axis=0).astype(BF16), bd(pw[s])) for s in S]
        tinv = [tinv[s] + ts[s][:L] for s in S]
        pw = [ts[s][L:] for s in S]
    tinv = [tinv[s] + hmm(tinv[s], bd(pw[s])) for s in S]
    for msk in (m32, m64):
        inner = [hmm(jnp.where(msk, a_ab[s], 0.0), bd(tinv[s])) for s in S]
        tinv = [tinv[s] + hmm(tinv[s], bd(inner[s])) for s in S]

    vbd = [bd(v[s]) for s in S]
    avs = [_dot(jnp.concatenate([a_ak[s], a_rk[s]], axis=0).astype(BF16), vbd[s]) for s in S]
    av = [avs[s][:L] for s in S]
    tx = [_dot(tinv[s].astype(BF16), jnp.concatenate([bd(at[s]), bd(av[s])], axis=1)) for s in S]
    ahat = [tx[s][:, :W] for s in S]
    vp = [tx[s][:, W:] for s in S]
    ox = [_dot(a_rb[s].astype(BF16), jnp.concatenate([bd(ahat[s]), bd(vp[s])], axis=1)) for s in S]
    rhat = [rt[s] + ox[s][:, :W] for s in S]
    o_intra = [ox[s][:, W:] + avs[s][L:] for s in S]

    def state_terms(s):
        z = jnp.concatenate([bv[s] * e_end[s], k[s] * e_end[s]], axis=0).astype(BF16)
        wm = jnp.concatenate(
            [jnp.concatenate([ahat[s], vp[s]], axis=1),
             jnp.concatenate([jnp.zeros((L, W), F32), v[s]], axis=1)], axis=0).astype(BF16)
        mn = _dot_tn(z, wm)
        m_mat = jnp.where(bmask, mn[:, :W], 0.0) + jnp.where(diag, jnp.exp(cl_last[s]), 0.0)
        return m_mat, jnp.where(bmask, mn[:, W:], 0.0)

    mn = [state_terms(s) for s in S]

    o = [None] * len(streams)
    hq = [h_scr[q] for q in range(ngroups)]
    for j in range(chunks_per_step):
        for q in range(ngroups):
            s = streams.index((q, j))
            m_hi, m_lo = _split_bf16(mn[s][0])
            lhs = jnp.concatenate([m_hi, m_lo, rhat[s].astype(BF16)], axis=0)
            res = _dot(lhs, hq[q].astype(BF16))
            o[s] = res[2 * W:] + o_intra[s]
            hq[q] = res[:W] + res[W:2 * W] + mn[s][1]
    for q in range(ngroups):
        h_scr[q] = hq[q]

    def headmean(zs):
        parts = []
        for z in zs:
            parts += list(_split_bf16(z))
        red = _dot(jnp.concatenate(parts, axis=0), ones_bd) * (1.0 / HEAD)
        return [red[2 * L * s:2 * L * s + L] + red[2 * L * s + L:2 * L * (s + 1)] for s in S]

    mean = headmean(o)
    dlt = [o[s] - mean[s] for s in S]
    var = headmean([dlt[s] * dlt[s] for s in S])
    for s in S:
        q, j = streams[s]
        gsl = slice(q * W, (q + 1) * W)
        rsl = slice(j * L, (j + 1) * L)
        yn = dlt[s] * lax.rsqrt(var[s] + GN_EPS) * gnw_ref[:, gsl] + gnb_ref[:, gsl]
        o_ref[0, rsl, gsl] = ((yn + bonus_ref[0, rsl, gsl]) * g_ref[0, rsl, gsl]).astype(o_ref.dtype)


def _wkv_scan(r, lw, k, v, kk, bv, g, bonus, gn_w, gn_b, chunks_per_step):
    b, t, d = r.shape
    lb = CHUNK * chunks_per_step
    tok = pl.BlockSpec((1, lb, d), lambda bi, ci: (bi, ci, 0))
    vec = pl.BlockSpec((1, d), lambda bi, ci: (0, 0))
    return pl.pallas_call(
        functools.partial(_wkv_scan_kernel, chunks_per_step=chunks_per_step),
        grid=(b, t // lb),
        in_specs=[tok] * 8 + [vec, vec],
        out_specs=tok,
        out_shape=jax.ShapeDtypeStruct((b, t, d), BF16),
        scratch_shapes=[pltpu.VMEM((d // GROUP, GROUP, GROUP), F32)],
        compiler_params=pltpu.CompilerParams(
            dimension_semantics=("arbitrary", "arbitrary"), vmem_limit_bytes=VMEM_LIMIT),
        name="wkv_scan",
    )(r, lw, k, v, kk, bv, g, bonus, gn_w[None, :], gn_b[None, :])


def _swiglu(h, wg, wu, wd):
    g = _dot(h, wg)
    u = _dot(h, wu)
    return _dot((g * jax.nn.sigmoid(g) * u).astype(BF16), wd)


def _ffn_dense_kernel(a_ref, x_ref, gt1_ref, sh_ref, sc_ref, gt2_ref, ng_ref, wo_ref,
                      wg_ref, wu_ref, wd_ref, o_ref):
    x1 = x_ref[0] + (1.0 + gt1_ref[0]) * _dot(a_ref[0], wo_ref[...])
    h = _modulate(x1, ng_ref[...], sh_ref[0], sc_ref[0]).astype(BF16)
    o_ref[0] = x1 + (1.0 + gt2_ref[0]) * _swiglu(h, wg_ref[...], wu_ref[...], wd_ref[...])


def _resident(shape, index_map):
    return pl.BlockSpec(shape, index_map, pipeline_mode=pl.Buffered(1))


def _ffn_dense(a, x, gt1, sh, sc, gt2, ng, w_o, w_gu, w_d, tm):
    b, t, d = x.shape
    f = w_d.shape[0]
    tok = pl.BlockSpec((1, tm, d), lambda bi, ti: (bi, ti, 0))
    vec = pl.BlockSpec((1, 1, d), lambda bi, ti: (bi, 0, 0))
    w_gu_b = w_gu.astype(BF16)
    return pl.pallas_call(
        _ffn_dense_kernel,
        grid=(b, t // tm),
        in_specs=[tok, tok, vec, vec, vec, vec, pl.BlockSpec((1, d), lambda bi, ti: (0, 0)),
                  _resident((d, d), lambda bi, ti: (0, 0)),
                  _resident((d, f), lambda bi, ti: (0, 0)),
                  _resident((d, f), lambda bi, ti: (0, 1)),
                  _resident((f, d), lambda bi, ti: (0, 0))],
        out_specs=tok,
        out_shape=jax.ShapeDtypeStruct((b, t, d), F32),
        compiler_params=pltpu.CompilerParams(
            dimension_semantics=("arbitrary", "arbitrary"), vmem_limit_bytes=VMEM_LIMIT),
        name="ffn_dense",
    )(a, x, gt1[:, None, :], sh[:, None, :], sc[:, None, :], gt2[:, None, :], ng[None, :],
      w_o.astype(BF16), w_gu_b, w_gu_b, w_d.astype(BF16))


def _lru_kernel(x_ref, sh_ref, sc_ref, gt_ref, ng_ref, win_ref, cw_ref, cb_ref, wg_ref, bg_ref,
                lam_ref, wout_ref, o_ref, xbuf, abuf, bbuf, hbuf, ga, gb, carry):
    t = pl.program_id(1)
    tm = x_ref.shape[1]
    w = win_ref.shape[1] // 2
    nblk = w // LRU_BLOCK
    nslab = w // LANES
    ng = tm // 8

    @pl.when(t == 0)
    def _():
        xbuf[0:8, :] = jnp.zeros((8, w), F32)
        carry[...] = jnp.zeros_like(carry)
        ga[:, :, 0:ng, :] = jnp.ones((2, nslab, ng, LANES), F32)
        gb[:, :, 0:ng, :] = jnp.zeros((2, nslab, ng, LANES), F32)

    x = x_ref[0]
    h = _modulate(x, ng_ref[...], sh_ref[0], sc_ref[0]).astype(BF16)
    xg = _dot(h, win_ref[...])
    xb = xg[:, :w]
    gx = xg[:, w:]
    gate = 0.5 * gx * (1.0 + jnp.tanh(0.7978845608028654 * (gx + 0.044715 * gx * gx * gx)))

    xbuf[8:8 + tm, :] = xb
    conv = cb_ref[...] + cw_ref[CONV_WIDTH - 1:CONV_WIDTH, :] * xb
    for jj in range(CONV_WIDTH - 1):
        shift = CONV_WIDTH - 1 - jj
        conv = conv + cw_ref[jj:jj + 1, :] * xbuf[8 - shift:8 - shift + tm, :]
    xbuf[0:8, :] = xb[tm - 8:tm, :]

    conv_b = conv.astype(BF16)
    rs, is_ = [], []
    for n in range(nblk):
        gts = _dot(conv_b[:, n * LRU_BLOCK:(n + 1) * LRU_BLOCK], wg_ref[n]) + bg_ref[n]
        gts = jax.nn.sigmoid(gts)
        rs.append(gts[:, :LRU_BLOCK])
        is_.append(gts[:, LRU_BLOCK:])
    r_t = jnp.concatenate(rs, axis=1)
    i_t = jnp.concatenate(is_, axis=1)

    log_a = -LRU_C * r_t * _softplus(-lam_ref[...])
    a_t = jnp.exp(log_a)
    b_t = jnp.sqrt(-jnp.tanh(log_a) * (a_t * a_t + 1.0)) * (i_t * conv)

    for j in range(nslab):
        abuf[j] = a_t[:, j * LANES:(j + 1) * LANES]
        bbuf[j] = b_t[:, j * LANES:(j + 1) * LANES]
    hs_slabs = []
    for j in range(nslab):
        a_loc = [abuf[j, pl.ds(0, ng, stride=8), :]]
        b_loc = [bbuf[j, pl.ds(0, ng, stride=8), :]]
        for s in range(1, 8):
            a_s = abuf[j, pl.ds(s, ng, stride=8), :]
            b_s = bbuf[j, pl.ds(s, ng, stride=8), :]
            b_loc.append(a_s * b_loc[-1] + b_s)
            a_loc.append(a_s * a_loc[-1])
        ga[0, j, ng:2 * ng, :] = a_loc[-1]
        gb[0, j, ng:2 * ng, :] = b_loc[-1]
        step = 1
        src = 0
        while step < ng:
            a_cur = ga[src, j, ng:2 * ng, :]
            b_cur = gb[src, j, ng:2 * ng, :]
            a_sh = ga[src, j, ng - step:2 * ng - step, :]
            b_sh = gb[src, j, ng - step:2 * ng - step, :]
            ga[1 - src, j, ng:2 * ng, :] = a_cur * a_sh
            gb[1 - src, j, ng:2 * ng, :] = a_cur * b_sh + b_cur
            src = 1 - src
            step *= 2
        c_in = carry[0:1, j * LANES:(j + 1) * LANES]
        h_in = gb[src, j, ng - 1:2 * ng - 1, :] + ga[src, j, ng - 1:2 * ng - 1, :] * c_in
        for s in range(8):
            h_s = b_loc[s] + a_loc[s] * h_in
            hbuf[j, pl.ds(s, ng, stride=8), :] = h_s
        carry[:, j * LANES:(j + 1) * LANES] = jnp.broadcast_to(h_s[ng - 1:ng, :], (8, LANES))
        hs_slabs.append(hbuf[j])
    hs = jnp.concatenate(hs_slabs, axis=1)

    y = _dot((hs * gate).astype(BF16), wout_ref[...])
    o_ref[0] = x + (1.0 + gt_ref[0]) * y


def _lru_block(x, sh, sc, gt, ng, w_in, conv_w, conv_b, w_gates, b_gates, lam, w_out, tm):
    b, t, d = x.shape
    w = w_out.shape[0]
    nblk = w // LRU_BLOCK
    full = lambda *shape: pl.BlockSpec(shape, lambda bi, ti: (0,) * len(shape))
    tok = pl.BlockSpec((1, tm, d), lambda bi, ti: (bi, ti, 0))
    vec = pl.BlockSpec((1, 1, d), lambda bi, ti: (bi, 0, 0))
    return pl.pallas_call(
        _lru_kernel,
        grid=(b, t // tm),
        in_specs=[tok, vec, vec, vec, full(1, d), full(d, 2 * w), full(CONV_WIDTH, w), full(1, w),
                  full(nblk, LRU_BLOCK, 2 * LRU_BLOCK), full(nblk, 1, 2 * LRU_BLOCK),
                  full(1, w), full(w, d)],
        out_specs=tok,
        out_shape=jax.ShapeDtypeStruct((b, t, d), F32),
        scratch_shapes=[pltpu.VMEM((tm + 8, w), F32)]
        + [pltpu.VMEM((w // LANES, tm, LANES), F32)] * 3
        + [pltpu.VMEM((2, w // LANES, tm // 4, LANES), F32)] * 2
        + [pltpu.VMEM((8, w), F32)],
        compiler_params=pltpu.CompilerParams(
            dimension_semantics=("arbitrary", "arbitrary"), vmem_limit_bytes=VMEM_LIMIT),
        name="rglru_block",
    )(x, sh[:, None, :], sc[:, None, :], gt[:, None, :], ng[None, :], w_in.astype(BF16),
      conv_w, conv_b[None, :], w_gates.astype(BF16), b_gates[:, None, :], lam[None, :],
      w_out.astype(BF16))


def _router_kernel(x_ref, sh_ref, sc_ref, ng_ref, wr_ref, br_ref,
                   h_out, eid_out, rank_out, prob_out, cnt_out, cnt_scr):
    i = pl.program_id(0)
    tm = x_ref.shape[0]

    @pl.when(i == 0)
    def _():
        cnt_scr[...] = jnp.zeros_like(cnt_scr)

    h = _modulate(x_ref[...], ng_ref[...], sh_ref[0], sc_ref[0])
    h_out[...] = h
    h_hi, h_lo = _split_bf16(h)
    w_hi, w_lo = _split_bf16(wr_ref[...])
    logits = _dot(h_hi, w_hi) + _dot(h_lo, w_hi) + _dot(h_hi, w_lo) + br_ref[...]

    lane = lax.broadcasted_iota(I32, logits.shape, 1)
    m1 = jnp.max(logits, axis=-1, keepdims=True)
    i1 = jnp.min(jnp.where(logits == m1, lane, LANES), axis=-1, keepdims=True)
    l2 = jnp.where(lane == i1, -jnp.inf, logits)
    m2 = jnp.max(l2, axis=-1, keepdims=True)
    i2 = jnp.min(jnp.where(l2 == m2, lane, LANES), axis=-1, keepdims=True)
    e = jnp.exp(m2 - m1)
    p1 = 1.0 / (1.0 + e)
    p2 = e / (1.0 + e)

    oh1 = (lane == i1).astype(F32)
    oh2 = (lane == i2).astype(F32)
    oh = oh1 + oh2
    rr = lax.broadcasted_iota(I32, (tm, tm), 0)
    cc = lax.broadcasted_iota(I32, (tm, tm), 1)
    tri = (cc < rr).astype(BF16)
    before = _dot(tri, oh.astype(BF16)) + cnt_scr[0:1, :]
    rank1 = jnp.sum(before * oh1, axis=-1, keepdims=True)
    rank2 = jnp.sum(before * oh2, axis=-1, keepdims=True)
    cnt_scr[...] = cnt_scr[...] + jnp.sum(oh, axis=0, keepdims=True)

    eid_out[:, 0:1] = i1
    eid_out[:, 1:2] = i2
    rank_out[:, 0:1] = rank1.astype(I32)
    rank_out[:, 1:2] = rank2.astype(I32)
    prob_out[:, 0:1] = p1
    prob_out[:, 1:2] = p2
    cnt_out[...] = cnt_scr[...].astype(I32)


def _router(x2d, sh, sc, ng, w_router, b_router, tiles_per_batch, tm):
    n, d = x2d.shape
    e = w_router.shape[1]
    wr = _pad_to(w_router, 1, LANES)
    br = jnp.concatenate([b_router, jnp.full((LANES - e,), -1e30, F32)])[None, :]
    tok = pl.BlockSpec((tm, d), lambda i: (i, 0))
    vec = pl.BlockSpec((1, 1, d), lambda i: (i // tiles_per_batch, 0, 0))
    two = pl.BlockSpec((tm, TOP_K), lambda i: (i, 0))
    return pl.pallas_call(
        _router_kernel,
        grid=(n // tm,),
        in_specs=[tok, vec, vec, pl.BlockSpec((1, d), lambda i: (0, 0)),
                  pl.BlockSpec((d, LANES), lambda i: (0, 0)), pl.BlockSpec((1, LANES), lambda i: (0, 0))],
        out_specs=[tok, two, two, two, pl.BlockSpec((8, LANES), lambda i: (0, 0))],
        out_shape=[jax.ShapeDtypeStruct((n, d), F32), jax.ShapeDtypeStruct((n, TOP_K), I32),
                   jax.ShapeDtypeStruct((n, TOP_K), I32), jax.ShapeDtypeStruct((n, TOP_K), F32),
                   jax.ShapeDtypeStruct((8, LANES), I32)],
        scratch_shapes=[pltpu.VMEM((8, LANES), F32)],
        compiler_params=pltpu.CompilerParams(
            dimension_semantics=("arbitrary",), vmem_limit_bytes=VMEM_LIMIT),
        name="moe_router",
    )(x2d, sh[:, None, :], sc[:, None, :], ng[None, :], wr, br)


def _moe_ffn_kernel(gid_ref, src0_ref, src_next_ref, dst_prev_ref, h_hbm, wgu_hbm, wd_hbm,
                    ys_hbm, x0, x1, y0, y1, wg_b, wu_b, wd_b, stage_gu, stage_d, gsem, ssem, wsem,
                    *, n_tiles, tm, spare_row):
    i = pl.program_id(0)
    xbufs = (x0, x1)
    ybufs = (y0, y1)
    f = wd_b.shape[0]
    wc = stage_gu.shape[2]

    def load_expert(e):
        chunks = []
        for c in range(f // wc):
            chunks.append((wgu_hbm.at[e, :, pl.ds(c * wc, wc)], stage_gu, wg_b, (slice(None), slice(c * wc, (c + 1) * wc))))
            chunks.append((wgu_hbm.at[e, :, pl.ds(f + c * wc, wc)], stage_gu, wu_b, (slice(None), slice(c * wc, (c + 1) * wc))))
        for c in range(f // wc):
            chunks.append((wd_hbm.at[e, pl.ds(c * wc, wc), :], stage_d, wd_b, (slice(c * wc, (c + 1) * wc), slice(None))))
        next_slot = {}
        copies = []
        for n_c, (src, stage, _, _) in enumerate(chunks):
            slot = next_slot.get(id(stage), 0)
            next_slot[id(stage)] = 1 - slot
            copies.append((pltpu.make_async_copy(src, stage.at[slot], wsem.at[n_c % 2]), stage, slot))
        copies[0][0].start()
        for n_c, (_, _, dest, where) in enumerate(chunks):
            if n_c + 1 < len(chunks):
                copies[n_c + 1][0].start()
            copy, stage, slot = copies[n_c]
            copy.wait()
            dest[where] = stage[slot].astype(BF16)

    def gather(src_ref, xbuf):
        for t in range(tm):
            pltpu.make_async_copy(h_hbm.at[pl.ds(src_ref[0, 0, t], 1)], xbuf.at[pl.ds(t, 1)],
                                  gsem).start()

    def scatter(ybuf):
        for t in range(tm):
            pltpu.make_async_copy(ybuf.at[pl.ds(t, 1)],
                                  ys_hbm.at[pl.ds(dst_prev_ref[0, 0, t], 1)], ssem).start()

    def wait_gather(xbuf):
        pltpu.make_async_copy(h_hbm.at[pl.ds(0, tm)], xbuf, gsem).wait()

    def wait_scatter(ybuf):
        pltpu.make_async_copy(ybuf, ys_hbm.at[pl.ds(0, tm)], ssem).wait()

    @pl.when(i == 0)
    def _():
        y0[...] = jnp.zeros_like(y0)
        y1[...] = jnp.zeros_like(y1)
        pltpu.make_async_copy(y0, ys_hbm.at[pl.ds(spare_row, tm)], ssem).start()
        gather(src0_ref, x0)

    for par in range(2):
        @pl.when((i < n_tiles) & (i % 2 == par))
        def _(par=par):
            xa, xb, ya, yb = xbufs[par], xbufs[1 - par], ybufs[par], ybufs[1 - par]
            wait_gather(xa)
            wait_scatter(ya)
            expert = gid_ref[i]

            @pl.when((i == 0) | (expert != gid_ref[jnp.maximum(i - 1, 0)]))
            def _():
                load_expert(expert)

            gather(src_next_ref, xb)
            scatter(yb)
            ya[...] = _swiglu(xa[...].astype(BF16), wg_b[...], wu_b[...], wd_b[...])

    @pl.when(i == n_tiles)
    def _():
        par = n_tiles % 2
        wait_gather(xbufs[par])
        wait_scatter(ybufs[par])
        scatter(ybufs[1 - par])
        wait_scatter(ybufs[1 - par])


def _moe_ffn(gid, src, dst, h2d, w_gu, w_d, out_rows, spare_row, tm):
    n_tiles = gid.shape[0]
    _, d = h2d.shape
    f = w_d.shape[1]
    idx = lambda fn: pl.BlockSpec((1, 1, tm), fn, memory_space=pltpu.SMEM)
    wc = MOE_WEIGHT_CHUNK
    grid_spec = pltpu.PrefetchScalarGridSpec(
        num_scalar_prefetch=1,
        grid=(n_tiles + 1,),
        in_specs=[idx(lambda i, gid: (0, 0, 0)),
                  idx(lambda i, gid: (jnp.minimum(i + 1, n_tiles), 0, 0)),
                  idx(lambda i, gid: (i, 0, 0)),
                  pl.BlockSpec(memory_space=pl.ANY), pl.BlockSpec(memory_space=pl.ANY),
                  pl.BlockSpec(memory_space=pl.ANY)],
        out_specs=pl.BlockSpec(memory_space=pl.ANY),
        scratch_shapes=[pltpu.VMEM((tm, d), F32)] * 4
        + [pltpu.VMEM((d, f), BF16), pltpu.VMEM((d, f), BF16), pltpu.VMEM((f, d), BF16),
           pltpu.VMEM((2, d, wc), F32), pltpu.VMEM((2, wc, d), F32),
           pltpu.SemaphoreType.DMA(()), pltpu.SemaphoreType.DMA(()), pltpu.SemaphoreType.DMA((2,))],
    )
    return pl.pallas_call(
        functools.partial(_moe_ffn_kernel, n_tiles=n_tiles, tm=tm, spare_row=spare_row),
        grid_spec=grid_spec,
        out_shape=jax.ShapeDtypeStruct((out_rows, d), F32),
        compiler_params=pltpu.CompilerParams(
            dimension_semantics=("arbitrary",), vmem_limit_bytes=VMEM_LIMIT),
        name="moe_ffn",
    )(gid, src, src, dst, h2d, w_gu, w_d)


def _combine_kernel(y0_ref, y1_ref, x_ref, gt_ref, prob_ref, fg_ref, o_ref):
    y = prob_ref[:, 0:1] * y0_ref[...] + prob_ref[:, 1:2] * y1_ref[...]
    x = x_ref[...] + (1.0 + gt_ref[0]) * y
    ms = jnp.mean(x * x, axis=-1, keepdims=True)
    o_ref[...] = x * lax.rsqrt(ms + NORM_EPS) * fg_ref[...]


def _combine(ys, x2d, gt, prob, final_g, tiles_per_batch, tc):
    n, d = x2d.shape
    nt = n // tc
    tok = pl.BlockSpec((tc, d), lambda i: (i, 0))
    return pl.pallas_call(
        _combine_kernel,
        grid=(nt,),
        in_specs=[tok, pl.BlockSpec((tc, d), lambda i: (nt + i, 0)), tok,
                  pl.BlockSpec((1, 1, d), lambda i: (i // tiles_per_batch, 0, 0)),
                  pl.BlockSpec((tc, TOP_K), lambda i: (i, 0)),
                  pl.BlockSpec((1, d), lambda i: (0, 0))],
        out_specs=tok,
        out_shape=jax.ShapeDtypeStruct((n, d), F32),
        compiler_params=pltpu.CompilerParams(
            dimension_semantics=("arbitrary",), vmem_limit_bytes=VMEM_LIMIT),
        name="moe_combine",
    )(ys, ys, x2d, gt[:, None, :], prob, final_g[None, :])


def _moe_layout(eid, rank, counts, n_experts, tm, n_tiles):
    n = eid.shape[0]
    rows = n_tiles * tm
    cnt = counts[0, :n_experts]
    tiles = (cnt + tm - 1) // tm
    tile_end = jnp.cumsum(tiles)
    offsets = (tile_end - tiles) * tm
    first_rank = jnp.cumsum(cnt) - cnt
    pos = (offsets[eid] + rank).reshape(-1)
    order = jnp.argsort(pos).astype(I32)
    tile_ids = jnp.arange(n_tiles, dtype=I32)
    last_id = jnp.minimum(tile_ids, tile_end[-1] - 1)
    gid = jnp.sum((last_id[:, None] >= tile_end[None, :]).astype(I32), axis=1)
    p = jnp.arange(rows, dtype=I32)
    e_of_p = jnp.repeat(gid, tm)
    within = p - offsets[e_of_p]
    valid = within < cnt[e_of_p]
    flat = order[jnp.clip(first_rank[e_of_p] + within, 0, TOP_K * n - 1)]
    tok = flat // TOP_K
    src = jnp.where(valid, tok, 0)
    pad_id = jnp.cumsum(jnp.logical_not(valid).astype(I32)) - 1
    dst = jnp.where(valid, (flat % TOP_K) * n + tok, TOP_K * n + pad_id)
    spare = rows
    src = jnp.concatenate([src, jnp.zeros((tm,), I32)]).reshape(n_tiles + 1, 1, tm)
    dst = jnp.concatenate([spare + jnp.arange(tm, dtype=I32), dst]).reshape(n_tiles + 1, 1, tm)
    return gid, src.astype(I32), dst.astype(I32), spare + 2 * tm, spare + tm


def kernel(x, c, ada_w, ada_b, norm_g, final_g, rwkv_mu, rwkv_w_rkv, rwkv_w_o, rwkv_w0, rwkv_w1, rwkv_w2, rwkv_a0, rwkv_a1, rwkv_a2, rwkv_g1, rwkv_g2, rwkv_k_k, rwkv_k_a, rwkv_r_k, rwkv_gn_w, rwkv_gn_b, lru_w_in, lru_conv_w, lru_conv_b, lru_w_gates, lru_b_gates, lru_lam, lru_w_out, ffn_w_gu, ffn_w_d, moe_w_router, moe_b_router, moe_w_gu, moe_w_d):
    b, t, d = x.shape
    n = b * t
    n_experts = moe_w_router.shape[-1]
    mod = _ada_mod(c, ada_w, ada_b)

    def mods(i):
        return [mod[i, :, q * d:(q + 1) * d] for q in range(6)]

    sh1, sc1, gt1, sh2, sc2, gt2 = mods(0)
    r, lw, k2, v, kk, bv, g, bonus = _rwkv_pre(
        x, sh1, sc1, norm_g[0, 0], rwkv_mu[0], rwkv_w_rkv[0], rwkv_w1[0], rwkv_w2[0],
        rwkv_a1[0], rwkv_a2[0], rwkv_g1[0], rwkv_g2[0], rwkv_w0[0], rwkv_a0[0],
        rwkv_k_k[0], rwkv_k_a[0], rwkv_r_k[0], tm=min(256, t))
    yg = _wkv_scan(r, lw, k2, v, kk, bv, g, bonus, rwkv_gn_w[0], rwkv_gn_b[0],
                   chunks_per_step=min(4, t // CHUNK))
    x = _ffn_dense(yg, x, gt1, sh2, sc2, gt2, norm_g[0, 1], rwkv_w_o[0], ffn_w_gu[0], ffn_w_d[0],
                   tm=min(256, t))

    sh1, sc1, gt1, sh2, sc2, gt2 = mods(1)
    x = _lru_block(x, sh1, sc1, gt1, norm_g[1, 0], lru_w_in[0], lru_conv_w[0], lru_conv_b[0],
                   lru_w_gates[0], lru_b_gates[0], lru_lam[0], lru_w_out[0], tm=min(256, t))

    x2d = x.reshape(n, d)
    tm_r = min(512, t)
    h2, eid, rank, prob, counts = _router(x2d, sh2, sc2, norm_g[1, 1], moe_w_router[0],
                                          moe_b_router[0], t // tm_r, tm_r)
    tm_g = min(256, t)
    n_tiles = (TOP_K * n) // tm_g + n_experts
    gid, src, dst, out_rows, spare_row = _moe_layout(eid, rank, counts, n_experts, tm_g, n_tiles)
    ys = _moe_ffn(gid, src, dst, h2, moe_w_gu[0], moe_w_d[0], out_rows, spare_row, tm_g)
    tc = min(512, t)
    out = _combine(ys, x2d, gt2, prob, final_g, t // tc, tc)
    return out.reshape(b, t, d)
```

```python
import functools

import jax
import jax.numpy as jnp
from jax import lax
from jax.experimental import pallas as pl
from jax.experimental.pallas import tpu as pltpu

F32 = jnp.float32
BF16 = jnp.bfloat16
I32 = jnp.int32

HEAD = 64
CHUNK = 64
GROUP = 256
HEADS_PER_GROUP = GROUP // HEAD
GN_EPS = 64e-5
NORM_EPS = 1e-6
LRU_C = 8.0
CONV_WIDTH = 4
LRU_BLOCK = 256
TOP_K = 2
LANES = 128
VMEM_LIMIT = 56 * 1024 * 1024


def _dot(a, b):
    return jnp.dot(a, b, preferred_element_type=F32)


def _dot_nt(a, b):
    return lax.dot_general(a, b, (((1,), (1,)), ((), ())), preferred_element_type=F32)


def _dot_tn(a, b):
    return lax.dot_general(a, b, (((0,), (0,)), ((), ())), preferred_element_type=F32)


def _softplus(u):
    return jnp.maximum(u, 0.0) + jnp.log1p(jnp.exp(-jnp.abs(u)))


def _modulate(x, ng, sh, sc):
    ms = jnp.mean(x * x, axis=-1, keepdims=True)
    return x * lax.rsqrt(ms + NORM_EPS) * ng * (1.0 + sc) + sh


def _split_bf16(x):
    hi = x.astype(BF16)
    lo = (x - hi.astype(F32)).astype(BF16)
    return hi, lo


def _cast_specs(weights, steps, step_index):
    ins, specs, out_shapes = [], [], []
    for w in weights:
        w2 = w.reshape(-1, w.shape[-1])
        rows = w2.shape[0]
        nblk = max(n for n in range(1, steps + 1) if rows % n == 0 and (rows // n) % 16 == 0)
        specs.append(pl.BlockSpec(
            (rows // nblk, w2.shape[1]),
            lambda *idx, nblk=nblk: (jnp.minimum(step_index(*idx), nblk - 1), 0)))
        ins.append(w2)
        out_shapes.append(jax.ShapeDtypeStruct(w2.shape, BF16))
    return ins, specs, out_shapes


def _with_casts(body, n_in, n_out, n_cast):
    def kernel(*refs):
        cast_in = refs[n_in:n_in + n_cast]
        outs = refs[n_in + n_cast:n_in + n_cast + n_out]
        cast_out = refs[n_in + n_cast + n_out:n_in + 2 * n_cast + n_out]
        for src, dst in zip(cast_in, cast_out):
            dst[...] = src[...].astype(BF16)
        body(*refs[:n_in], *outs, *refs[n_in + 2 * n_cast + n_out:])
    return kernel


def _shift_select(tm, shifts):
    rr = jnp.arange(tm)[:, None]
    cc = jnp.arange(tm)[None, :]
    return jnp.concatenate([(cc == rr - s) for s in shifts], axis=0).astype(BF16)


def _mod_kernel(c_ref, w_ref, b_ref, o_ref):
    c = c_ref[...]
    cond = c * jax.nn.sigmoid(c)
    o_ref[0] = _dot(cond.astype(BF16), w_ref[0].astype(BF16)) + b_ref[0]


def _ada_mod(c, ada_w, ada_b):
    depth, d, d6 = ada_w.shape
    b = c.shape[0]
    rows = 8
    c8 = jnp.pad(c, ((0, rows - b), (0, 0)))
    tn = 1024
    out = pl.pallas_call(
        _mod_kernel,
        grid=(depth, d6 // tn),
        in_specs=[
            pl.BlockSpec((rows, d), lambda i, j: (0, 0)),
            pl.BlockSpec((1, d, tn), lambda i, j: (i, 0, j)),
            pl.BlockSpec((1, 1, tn), lambda i, j: (i, 0, j)),
        ],
        out_specs=pl.BlockSpec((1, rows, tn), lambda i, j: (i, 0, j)),
        out_shape=jax.ShapeDtypeStruct((depth, rows, d6), F32),
        name="ada_mod",
    )(c8, ada_w, ada_b.reshape(depth, 1, d6))
    return out[:, :b]


def _rwkv_pre_kernel(x_ref, sh_ref, sc_ref, ng_ref, mu_ref, wrkv_ref, w1_ref, w2_ref,
                     a1_ref, a2_ref, g1_ref, g2_ref, vec_ref, seg_ref, segt_ref, sel_ref,
                     r_out, lw_out, k_out, v_out, kk_out, b_out, g_out, bonus_out,
                     hbuf):
    t = pl.program_id(1)
    tm = x_ref.shape[1]
    d = x_ref.shape[2]

    h = _modulate(x_ref[0], ng_ref[...], sh_ref[0], sc_ref[0])

    @pl.when(t == 0)
    def _():
        hbuf[...] = jnp.zeros((8, d), F32)

    hprev = _dot(sel_ref[...], h.astype(BF16))
    first = lax.broadcasted_iota(I32, (tm, d), 0) == 0
    hprev = jnp.where(first, hbuf[7:8, :], hprev)
    hbuf[...] = h[tm - 8:tm, :]
    xx = hprev - h

    def mix(p):
        return (h + xx * mu_ref[p:p + 1, :]).astype(BF16)

    r = _dot(mix(0), wrkv_ref[0])
    k = _dot(mix(1), wrkv_ref[1])
    v = _dot(mix(2), wrkv_ref[2])
    wl = _dot(jnp.tanh(_dot(mix(3), w1_ref[...])).astype(BF16), w2_ref[...])
    al = _dot(_dot(mix(4), a1_ref[...]).astype(BF16), a2_ref[...])
    g = _dot(jax.nn.sigmoid(_dot(mix(5), g1_ref[...])).astype(BF16), g2_ref[...])

    w0 = vec_ref[0:1, :]
    a0 = vec_ref[1:2, :]
    k_k = vec_ref[2:3, :]
    k_a = vec_ref[3:4, :]
    r_k = vec_ref[4:5, :]

    def headsum(z):
        s = _dot(z.astype(BF16), seg_ref[...])
        s_hi, s_lo = _split_bf16(s)
        return _dot(jnp.concatenate([s_hi, s_lo], axis=1), segt_ref[...])

    lw = -0.6065306597126334 * jax.nn.sigmoid(w0 + wl)
    a = jax.nn.sigmoid(a0 + al)
    kk = k * k_k
    kk = kk * lax.rsqrt(jnp.maximum(headsum(kk * kk), 1e-24))
    k2 = k * (1.0 + (a - 1.0) * k_a)
    bonus = headsum(r * k2 * r_k) * v

    r_out[0] = r
    lw_out[0] = lw
    k_out[0] = k2
    v_out[0] = v
    kk_out[0] = kk
    b_out[0] = kk * a
    g_out[0] = g
    bonus_out[0] = bonus


def _pad_to(x, axis, size):
    pad = [(0, 0)] * x.ndim
    pad[axis] = (0, size - x.shape[axis])
    return jnp.pad(x, pad)


def _rwkv_pre(x, sh, sc, ng, mu, w_rkv, w1, w2, a1, a2, g1, g2, w0, a0, k_k, k_a, r_k, tm,
              later_weights):
    b, t, d = x.shape
    nh = d // HEAD
    lw_pad = LANES * pl.cdiv(w1.shape[1], LANES)
    la_pad = LANES * pl.cdiv(a1.shape[1], LANES)
    lg_pad = LANES * pl.cdiv(g1.shape[1], LANES)
    vecs = _pad_to(jnp.stack([w0, a0, k_k, k_a, r_k.reshape(d)]), 0, 8)
    head_of_lane = jnp.arange(d) // HEAD
    seg = (head_of_lane[:, None] == jnp.arange(LANES)[None, :]).astype(BF16)
    segt = jnp.concatenate([seg.T, seg.T], axis=0)
    del nh
    full = lambda *shape: pl.BlockSpec(shape, lambda bi, ti: (0,) * len(shape))
    tok = pl.BlockSpec((1, tm, d), lambda bi, ti: (bi, ti, 0))
    vec = pl.BlockSpec((1, 1, d), lambda bi, ti: (bi, 0, 0))
    nt = t // tm
    cast_in, cast_specs, cast_shapes = _cast_specs(later_weights, b * nt, lambda bi, ti: bi * nt + ti)
    in_specs = [tok, vec, vec, full(1, d), full(8, d), full(3, d, d),
                full(d, lw_pad), full(lw_pad, d), full(d, la_pad), full(la_pad, d),
                full(d, lg_pad), full(lg_pad, d), full(8, d), full(d, LANES), full(2 * LANES, d),
                full(tm, tm)]
    outs = pl.pallas_call(
        _with_casts(_rwkv_pre_kernel, len(in_specs), 8, len(cast_in)),
        grid=(b, nt),
        in_specs=in_specs + cast_specs,
        out_specs=[tok] * 8 + cast_specs,
        out_shape=[jax.ShapeDtypeStruct((b, t, d), F32)] * 8 + cast_shapes,
        scratch_shapes=[pltpu.VMEM((8, d), F32)],
        compiler_params=pltpu.CompilerParams(
            dimension_semantics=("arbitrary", "arbitrary"), vmem_limit_bytes=VMEM_LIMIT),
        name="rwkv_pre",
    )(x, sh[:, None, :], sc[:, None, :], ng[None, :], _pad_to(mu, 0, 8), w_rkv.astype(BF16),
      _pad_to(w1, 1, lw_pad).astype(BF16), _pad_to(w2, 0, lw_pad).astype(BF16),
      _pad_to(a1, 1, la_pad).astype(BF16), _pad_to(a2, 0, la_pad).astype(BF16),
      _pad_to(g1, 1, lg_pad).astype(BF16), _pad_to(g2, 0, lg_pad).astype(BF16),
      vecs, seg, segt, _shift_select(tm, [1]), *cast_in)
    casts = [o.reshape(w.shape) for o, w in zip(outs[8:], later_weights)]
    return outs[:8], casts


def _wkv_scan_kernel(r_ref, lw_ref, k_ref, v_ref, kk_ref, b_ref, g_ref, bonus_ref,
                     gnw_ref, gnb_ref, o_ref, h_scr, *, chunks_per_step):
    L = CHUNK
    W = GROUP
    ngroups = r_ref.shape[2] // W

    @pl.when(pl.program_id(1) == 0)
    def _():
        h_scr[...] = jnp.zeros_like(h_scr)

    row = lax.broadcasted_iota(I32, (L, W), 0)
    lane = lax.broadcasted_iota(I32, (L, W), 1)
    sidx = lane & (L - 1)
    lane_head = lane >> 6
    strict = sidx < row
    incl = sidx <= row
    eye = (sidx == row).astype(F32)
    same16 = (row >> 4) == (sidx >> 4)
    same32 = (row >> 5) == (sidx >> 5)
    m16 = strict & same16
    m32 = strict & same32 & jnp.logical_not(same16)
    m64 = strict & jnp.logical_not(same32)
    rb = lax.broadcasted_iota(I32, (W, W), 0)
    cb = lax.broadcasted_iota(I32, (W, W), 1)
    bmask = (rb >> 6) == (cb >> 6)
    diag = rb == cb
    ones_bd = bmask.astype(BF16)
    tri_r = lax.broadcasted_iota(I32, (L, 3 * L), 0)
    tri_c = lax.broadcasted_iota(I32, (L, 3 * L), 1)
    tri3 = ((tri_c & (L - 1)) <= tri_r).astype(BF16)

    def bd(y):
        yt = jnp.concatenate([y] * HEADS_PER_GROUP, axis=0)
        return jnp.where(bmask, yt, 0.0).astype(BF16)

    def hmm(x, ybd):
        return _dot(x.astype(BF16), ybd)

    streams = [(q, j) for j in range(chunks_per_step) for q in range(ngroups)]
    S = range(len(streams))

    def ld(ref, s):
        q, j = streams[s]
        return ref[0, j * L:(j + 1) * L, q * W:(q + 1) * W]

    r = [ld(r_ref, s) for s in S]
    lw = [ld(lw_ref, s) for s in S]
    k = [ld(k_ref, s) for s in S]
    v = [ld(v_ref, s) for s in S]
    kk = [ld(kk_ref, s) for s in S]
    bv = [ld(b_ref, s) for s in S]

    def cumlog(x):
        hi = x.astype(BF16)
        rem = x - hi.astype(F32)
        mid = rem.astype(BF16)
        lo = (rem - mid.astype(F32)).astype(BF16)
        return _dot(tri3, jnp.concatenate([hi, mid, lo], axis=0))

    cl = [cumlog(lw[s]) for s in S]
    cl_last = [cl[s][L - 1:L, :] for s in S]
    e_pos = [jnp.exp(cl[s]) for s in S]
    e_neg = [jnp.exp(-cl[s]) for s in S]
    e_end = [jnp.exp(cl_last[s] - cl[s]) for s in S]
    rt = [r[s] * e_pos[s] for s in S]
    at = [-kk[s] * jnp.exp(cl[s] - lw[s]) for s in S]
    bt = [bv[s] * e_neg[s] for s in S]
    kt = [k[s] * e_neg[s] for s in S]

    def gram(s):
        x = jnp.concatenate([at[s], rt[s]], axis=0).astype(BF16)
        ys = [jnp.where(lane_head == hh, bt[s], 0.0) for hh in range(HEADS_PER_GROUP)]
        ys += [jnp.where(lane_head == hh, kt[s], 0.0) for hh in range(HEADS_PER_GROUP)]
        return _dot_nt(x, jnp.concatenate(ys, axis=0).astype(BF16))

    gm = [gram(s) for s in S]
    a_ab = [jnp.where(strict, gm[s][:L, :W], 0.0) for s in S]
    a_ak = [jnp.where(strict, gm[s][:L, W:], 0.0) for s in S]
    a_rb = [jnp.where(incl, gm[s][L:, :W], 0.0) for s in S]
    a_rk = [jnp.where(incl, gm[s][L:, W:], 0.0) for s in S]

    a0 = [jnp.where(m16, a_ab[s], 0.0) for s in S]
    pw = [hmm(a0[s], bd(a0[s])) for s in S]
    tinv = [eye + a0[s] for s in S]
    for _ in range(2):
        ts = [_dot(jnp.concatenate([tinv[s], pw[s]], axis=0).astype(BF16), bd(pw[s])) for s in S]
        tinv = [tinv[s] + ts[s][:L] for s in S]
        pw = [ts[s][L:] for s in S]
    tinv = [tinv[s] + hmm(tinv[s], bd(pw[s])) for s in S]
    for msk in (m32, m64):
        inner = [hmm(jnp.where(msk, a_ab[s], 0.0), bd(tinv[s])) for s in S]
        tinv = [tinv[s] + hmm(tinv[s], bd(inner[s])) for s in S]

    vbd = [bd(v[s]) for s in S]
    avs = [_dot(jnp.concatenate([a_ak[s], a_rk[s]], axis=0).astype(BF16), vbd[s]) for s in S]
    av = [avs[s][:L] for s in S]
    tx = [_dot(tinv[s].astype(BF16), jnp.concatenate([bd(at[s]), bd(av[s])], axis=1)) for s in S]
    ahat = [tx[s][:, :W] for s in S]
    vp = [tx[s][:, W:] for s in S]
    ox = [_dot(a_rb[s].astype(BF16), jnp.concatenate([bd(ahat[s]), bd(vp[s])], axis=1)) for s in S]
    rhat = [rt[s] + ox[s][:, :W] for s in S]
    o_intra = [ox[s][:, W:] + avs[s][L:] for s in S]

    def state_terms(s):
        z = jnp.concatenate([bv[s] * e_end[s], k[s] * e_end[s]], axis=0).astype(BF16)
        wm = jnp.concatenate(
            [jnp.concatenate([ahat[s], vp[s]], axis=1),
             jnp.concatenate([jnp.zeros((L, W), F32), v[s]], axis=1)], axis=0).astype(BF16)
        mn = _dot_tn(z, wm)
        m_mat = jnp.where(bmask, mn[:, :W], 0.0) + jnp.where(diag, jnp.exp(cl_last[s]), 0.0)
        return m_mat, jnp.where(bmask, mn[:, W:], 0.0)

    mn = [state_terms(s) for s in S]

    o = [None] * len(streams)
    hq = [h_scr[q] for q in range(ngroups)]
    for j in range(chunks_per_step):
        for q in range(ngroups):
            s = streams.index((q, j))
            m_hi, m_lo = _split_bf16(mn[s][0])
            lhs = jnp.concatenate([m_hi, m_lo, rhat[s].astype(BF16)], axis=0)
            res = _dot(lhs, hq[q].astype(BF16))
            o[s] = res[2 * W:] + o_intra[s]
            hq[q] = res[:W] + res[W:2 * W] + mn[s][1]
    for q in range(ngroups):
        h_scr[q] = hq[q]

    def headmean(zs):
        parts = []
        for z in zs:
            parts += list(_split_bf16(z))
        red = _dot(jnp.concatenate(parts, axis=0), ones_bd) * (1.0 / HEAD)
        return [red[2 * L * s:2 * L * s + L] + red[2 * L * s + L:2 * L * (s + 1)] for s in S]

    mean = headmean(o)
    dlt = [o[s] - mean[s] for s in S]
    var = headmean([dlt[s] * dlt[s] for s in S])
    for s in S:
        q, j = streams[s]
        gsl = slice(q * W, (q + 1) * W)
        rsl = slice(j * L, (j + 1) * L)
        yn = dlt[s] * lax.rsqrt(var[s] + GN_EPS) * gnw_ref[:, gsl] + gnb_ref[:, gsl]
        o_ref[0, rsl, gsl] = ((yn + bonus_ref[0, rsl, gsl]) * g_ref[0, rsl, gsl]).astype(o_ref.dtype)


def _wkv_scan(r, lw, k, v, kk, bv, g, bonus, gn_w, gn_b, chunks_per_step):
    b, t, d = r.shape
    lb = CHUNK * chunks_per_step
    tok = pl.BlockSpec((1, lb, d), lambda bi, ci: (bi, ci, 0))
    vec = pl.BlockSpec((1, d), lambda bi, ci: (0, 0))
    return pl.pallas_call(
        functools.partial(_wkv_scan_kernel, chunks_per_step=chunks_per_step),
        grid=(b, t // lb),
        in_specs=[tok] * 8 + [vec, vec],
        out_specs=tok,
        out_shape=jax.ShapeDtypeStruct((b, t, d), BF16),
        scratch_shapes=[pltpu.VMEM((d // GROUP, GROUP, GROUP), F32)],
        compiler_params=pltpu.CompilerParams(
            dimension_semantics=("arbitrary", "arbitrary"), vmem_limit_bytes=VMEM_LIMIT),
        name="wkv_scan",
    )(r, lw, k, v, kk, bv, g, bonus, gn_w[None, :], gn_b[None, :])


def _swiglu(h, wg, wu, wd):
    g = _dot(h, wg)
    u = _dot(h, wu)
    return _dot((g * jax.nn.sigmoid(g) * u).astype(BF16), wd)


def _ffn_dense_kernel(a_ref, x_ref, gt1_ref, sh_ref, sc_ref, gt2_ref, ng_ref, wo_ref,
                      wg_ref, wu_ref, wd_ref, o_ref):
    x1 = x_ref[0] + (1.0 + gt1_ref[0]) * _dot(a_ref[0], wo_ref[...])
    h = _modulate(x1, ng_ref[...], sh_ref[0], sc_ref[0]).astype(BF16)
    o_ref[0] = x1 + (1.0 + gt2_ref[0]) * _swiglu(h, wg_ref[...], wu_ref[...], wd_ref[...])


def _resident(shape, index_map):
    return pl.BlockSpec(shape, index_map, pipeline_mode=pl.Buffered(1))


def _ffn_dense(a, x, gt1, sh, sc, gt2, ng, w_o, w_gu, w_d, tm, later_weights):
    b, t, d = x.shape
    f = w_d.shape[0]
    nt = t // tm
    tok = pl.BlockSpec((1, tm, d), lambda bi, ti: (bi, ti, 0))
    vec = pl.BlockSpec((1, 1, d), lambda bi, ti: (bi, 0, 0))
    cast_in, cast_specs, cast_shapes = _cast_specs(later_weights, b * nt, lambda bi, ti: bi * nt + ti)
    in_specs = [tok, tok, vec, vec, vec, vec, pl.BlockSpec((1, d), lambda bi, ti: (0, 0)),
                _resident((d, d), lambda bi, ti: (0, 0)),
                _resident((d, f), lambda bi, ti: (0, 0)),
                _resident((d, f), lambda bi, ti: (0, 1)),
                _resident((f, d), lambda bi, ti: (0, 0))]
    outs = pl.pallas_call(
        _with_casts(_ffn_dense_kernel, len(in_specs), 1, len(cast_in)),
        grid=(b, nt),
        in_specs=in_specs + cast_specs,
        out_specs=[tok] + cast_specs,
        out_shape=[jax.ShapeDtypeStruct((b, t, d), F32)] + cast_shapes,
        compiler_params=pltpu.CompilerParams(
            dimension_semantics=("arbitrary", "arbitrary"), vmem_limit_bytes=VMEM_LIMIT),
        name="ffn_dense",
    )(a, x, gt1[:, None, :], sh[:, None, :], sc[:, None, :], gt2[:, None, :], ng[None, :],
      w_o, w_gu, w_gu, w_d, *cast_in)
    return outs[0], [o.reshape(w.shape) for o, w in zip(outs[1:], later_weights)]


def _lru_kernel(x_ref, sh_ref, sc_ref, gt_ref, ng_ref, win_ref, cw_ref, cb_ref, wg_ref, bg_ref,
                lam_ref, wout_ref, o_ref, xbuf, abuf, bbuf, hbuf, ga, gb, carry):
    t = pl.program_id(1)
    tm = x_ref.shape[1]
    w = win_ref.shape[1] // 2
    nblk = w // LRU_BLOCK
    nslab = w // LANES
    ng = tm // 8

    @pl.when(t == 0)
    def _():
        xbuf[0:8, :] = jnp.zeros((8, w), F32)
        carry[...] = jnp.zeros_like(carry)
        ga[:, :, 0:ng, :] = jnp.ones((2, nslab, ng, LANES), F32)
        gb[:, :, 0:ng, :] = jnp.zeros((2, nslab, ng, LANES), F32)

    x = x_ref[0]
    h = _modulate(x, ng_ref[...], sh_ref[0], sc_ref[0]).astype(BF16)
    xg = _dot(h, win_ref[...])
    xb = xg[:, :w]
    gx = xg[:, w:]
    gate = 0.5 * gx * (1.0 + jnp.tanh(0.7978845608028654 * (gx + 0.044715 * gx * gx * gx)))

    xbuf[8:8 + tm, :] = xb
    conv = cb_ref[...] + cw_ref[CONV_WIDTH - 1:CONV_WIDTH, :] * xb
    for jj in range(CONV_WIDTH - 1):
        shift = CONV_WIDTH - 1 - jj
        conv = conv + cw_ref[jj:jj + 1, :] * xbuf[8 - shift:8 - shift + tm, :]
    xbuf[0:8, :] = xb[tm - 8:tm, :]

    conv_b = conv.astype(BF16)
    rs, is_ = [], []
    for n in range(nblk):
        gts = _dot(conv_b[:, n * LRU_BLOCK:(n + 1) * LRU_BLOCK], wg_ref[n]) + bg_ref[n]
        gts = jax.nn.sigmoid(gts)
        rs.append(gts[:, :LRU_BLOCK])
        is_.append(gts[:, LRU_BLOCK:])
    r_t = jnp.concatenate(rs, axis=1)
    i_t = jnp.concatenate(is_, axis=1)

    log_a = -LRU_C * r_t * _softplus(-lam_ref[...])
    a_t = jnp.exp(log_a)
    b_t = jnp.sqrt(-jnp.tanh(log_a) * (a_t * a_t + 1.0)) * (i_t * conv)

    for j in range(nslab):
        abuf[j] = a_t[:, j * LANES:(j + 1) * LANES]
        bbuf[j] = b_t[:, j * LANES:(j + 1) * LANES]
    hs_slabs = []
    for j in range(nslab):
        a_loc = [abuf[j, pl.ds(0, ng, stride=8), :]]
        b_loc = [bbuf[j, pl.ds(0, ng, stride=8), :]]
        for s in range(1, 8):
            a_s = abuf[j, pl.ds(s, ng, stride=8), :]
            b_s = bbuf[j, pl.ds(s, ng, stride=8), :]
            b_loc.append(a_s * b_loc[-1] + b_s)
            a_loc.append(a_s * a_loc[-1])
        ga[0, j, ng:2 * ng, :] = a_loc[-1]
        gb[0, j, ng:2 * ng, :] = b_loc[-1]
        step = 1
        src = 0
        while step < ng:
            a_cur = ga[src, j, ng:2 * ng, :]
            b_cur = gb[src, j, ng:2 * ng, :]
            a_sh = ga[src, j, ng - step:2 * ng - step, :]
            b_sh = gb[src, j, ng - step:2 * ng - step, :]
            ga[1 - src, j, ng:2 * ng, :] = a_cur * a_sh
            gb[1 - src, j, ng:2 * ng, :] = a_cur * b_sh + b_cur
            src = 1 - src
            step *= 2
        c_in = carry[0:1, j * LANES:(j + 1) * LANES]
        h_in = gb[src, j, ng - 1:2 * ng - 1, :] + ga[src, j, ng - 1:2 * ng - 1, :] * c_in
        for s in range(8):
            h_s = b_loc[s] + a_loc[s] * h_in
            hbuf[j, pl.ds(s, ng, stride=8), :] = h_s
        carry[:, j * LANES:(j + 1) * LANES] = jnp.broadcast_to(h_s[ng - 1:ng, :], (8, LANES))
        hs_slabs.append(hbuf[j])
    hs = jnp.concatenate(hs_slabs, axis=1)

    y = _dot((hs * gate).astype(BF16), wout_ref[...])
    o_ref[0] = x + (1.0 + gt_ref[0]) * y


def _lru_block(x, sh, sc, gt, ng, w_in, conv_w, conv_b, w_gates, b_gates, lam, w_out, tm,
               later_weights):
    b, t, d = x.shape
    w = w_out.shape[0]
    nblk = w // LRU_BLOCK
    nt = t // tm
    full = lambda *shape: pl.BlockSpec(shape, lambda bi, ti: (0,) * len(shape))
    tok = pl.BlockSpec((1, tm, d), lambda bi, ti: (bi, ti, 0))
    vec = pl.BlockSpec((1, 1, d), lambda bi, ti: (bi, 0, 0))
    cast_in, cast_specs, cast_shapes = _cast_specs(later_weights, b * nt, lambda bi, ti: bi * nt + ti)
    in_specs = [tok, vec, vec, vec, full(1, d), full(d, 2 * w), full(CONV_WIDTH, w), full(1, w),
                full(nblk, LRU_BLOCK, 2 * LRU_BLOCK), full(nblk, 1, 2 * LRU_BLOCK),
                full(1, w), full(w, d)]
    outs = pl.pallas_call(
        _with_casts(_lru_kernel, len(in_specs), 1, len(cast_in)),
        grid=(b, nt),
        in_specs=in_specs + cast_specs,
        out_specs=[tok] + cast_specs,
        out_shape=[jax.ShapeDtypeStruct((b, t, d), F32)] + cast_shapes,
        scratch_shapes=[pltpu.VMEM((tm + 8, w), F32)]
        + [pltpu.VMEM((w // LANES, tm, LANES), F32)] * 3
        + [pltpu.VMEM((2, w // LANES, tm // 4, LANES), F32)] * 2
        + [pltpu.VMEM((8, w), F32)],
        compiler_params=pltpu.CompilerParams(
            dimension_semantics=("arbitrary", "arbitrary"), vmem_limit_bytes=VMEM_LIMIT),
        name="rglru_block",
    )(x, sh[:, None, :], sc[:, None, :], gt[:, None, :], ng[None, :], w_in,
      conv_w, conv_b[None, :], w_gates, b_gates[:, None, :], lam[None, :], w_out, *cast_in)
    return outs[0], [o.reshape(wt.shape) for o, wt in zip(outs[1:], later_weights)]


def _router_kernel(x_ref, sh_ref, sc_ref, ng_ref, wr_ref, br_ref,
                   h_out, eid_out, rank_out, prob_out, cnt_out, cnt_scr):
    i = pl.program_id(0)
    tm = x_ref.shape[0]

    @pl.when(i == 0)
    def _():
        cnt_scr[...] = jnp.zeros_like(cnt_scr)

    h = _modulate(x_ref[...], ng_ref[...], sh_ref[0], sc_ref[0])
    h_out[...] = h
    h_hi, h_lo = _split_bf16(h)
    w_hi, w_lo = _split_bf16(wr_ref[...])
    logits = _dot(h_hi, w_hi) + _dot(h_lo, w_hi) + _dot(h_hi, w_lo) + br_ref[...]

    lane = lax.broadcasted_iota(I32, logits.shape, 1)
    m1 = jnp.max(logits, axis=-1, keepdims=True)
    i1 = jnp.min(jnp.where(logits == m1, lane, LANES), axis=-1, keepdims=True)
    l2 = jnp.where(lane == i1, -jnp.inf, logits)
    m2 = jnp.max(l2, axis=-1, keepdims=True)
    i2 = jnp.min(jnp.where(l2 == m2, lane, LANES), axis=-1, keepdims=True)
    e = jnp.exp(m2 - m1)
    p1 = 1.0 / (1.0 + e)
    p2 = e / (1.0 + e)

    oh1 = (lane == i1).astype(F32)
    oh2 = (lane == i2).astype(F32)
    oh = oh1 + oh2
    rr = lax.broadcasted_iota(I32, (tm, tm), 0)
    cc = lax.broadcasted_iota(I32, (tm, tm), 1)
    tri = (cc < rr).astype(BF16)
    before = _dot(tri, oh.astype(BF16)) + cnt_scr[0:1, :]
    rank1 = jnp.sum(before * oh1, axis=-1, keepdims=True)
    rank2 = jnp.sum(before * oh2, axis=-1, keepdims=True)
    cnt_scr[...] = cnt_scr[...] + jnp.sum(oh, axis=0, keepdims=True)

    eid_out[:, 0:1] = i1
    eid_out[:, 1:2] = i2
    rank_out[:, 0:1] = rank1.astype(I32)
    rank_out[:, 1:2] = rank2.astype(I32)
    prob_out[:, 0:1] = p1
    prob_out[:, 1:2] = p2
    cnt_out[...] = cnt_scr[...].astype(I32)


def _router(x2d, sh, sc, ng, w_router, b_router, tiles_per_batch, tm):
    n, d = x2d.shape
    e = w_router.shape[1]
    wr = _pad_to(w_router, 1, LANES)
    br = jnp.concatenate([b_router, jnp.full((LANES - e,), -1e30, F32)])[None, :]
    tok = pl.BlockSpec((tm, d), lambda i: (i, 0))
    vec = pl.BlockSpec((1, 1, d), lambda i: (i // tiles_per_batch, 0, 0))
    two = pl.BlockSpec((tm, TOP_K), lambda i: (i, 0))
    return pl.pallas_call(
        _router_kernel,
        grid=(n // tm,),
        in_specs=[tok, vec, vec, pl.BlockSpec((1, d), lambda i: (0, 0)),
                  pl.BlockSpec((d, LANES), lambda i: (0, 0)), pl.BlockSpec((1, LANES), lambda i: (0, 0))],
        out_specs=[tok, two, two, two, pl.BlockSpec((8, LANES), lambda i: (0, 0))],
        out_shape=[jax.ShapeDtypeStruct((n, d), F32), jax.ShapeDtypeStruct((n, TOP_K), I32),
                   jax.ShapeDtypeStruct((n, TOP_K), I32), jax.ShapeDtypeStruct((n, TOP_K), F32),
                   jax.ShapeDtypeStruct((8, LANES), I32)],
        scratch_shapes=[pltpu.VMEM((8, LANES), F32)],
        compiler_params=pltpu.CompilerParams(
            dimension_semantics=("arbitrary",), vmem_limit_bytes=VMEM_LIMIT),
        name="moe_router",
    )(x2d, sh[:, None, :], sc[:, None, :], ng[None, :], wr, br)


def _moe_ffn_kernel(gid_ref, valid_ref, src0_ref, src_next_ref, dst_prev_ref, h_hbm,
                    wg_ref, wu_ref, wd_ref, ys_hbm, x0, x1, y0, y1, gsem, ssem,
                    *, n_tiles, tm, spare_row):
    del gid_ref
    i = pl.program_id(0)
    xbufs = (x0, x1)
    ybufs = (y0, y1)
    valid = valid_ref[jnp.minimum(i, n_tiles - 1)] > 0

    def gather(src_ref, xbuf):
        for t in range(tm):
            pltpu.make_async_copy(h_hbm.at[pl.ds(src_ref[0, 0, t], 1)], xbuf.at[pl.ds(t, 1)],
                                  gsem).start()

    def scatter(ybuf):
        for t in range(tm):
            pltpu.make_async_copy(ybuf.at[pl.ds(t, 1)],
                                  ys_hbm.at[pl.ds(dst_prev_ref[0, 0, t], 1)], ssem).start()

    def wait_gather(xbuf):
        pltpu.make_async_copy(h_hbm.at[pl.ds(0, tm)], xbuf, gsem).wait()

    def wait_scatter(ybuf):
        pltpu.make_async_copy(ybuf, ys_hbm.at[pl.ds(0, tm)], ssem).wait()

    @pl.when(i == 0)
    def _():
        y0[...] = jnp.zeros_like(y0)
        y1[...] = jnp.zeros_like(y1)
        pltpu.make_async_copy(y0, ys_hbm.at[pl.ds(spare_row, tm)], ssem).start()
        gather(src0_ref, x0)

    for par in range(2):
        for compute in (True, False):
            @pl.when((i < n_tiles) & (i % 2 == par) & (valid if compute else jnp.logical_not(valid)))
            def _(par=par, compute=compute):
                xa, xb, ya, yb = xbufs[par], xbufs[1 - par], ybufs[par], ybufs[1 - par]
                wait_gather(xa)
                wait_scatter(ya)
                gather(src_next_ref, xb)
                scatter(yb)
                if compute:
                    ya[...] = _swiglu(xa[...].astype(BF16), wg_ref[0], wu_ref[0], wd_ref[0])

    @pl.when(i == n_tiles)
    def _():
        par = n_tiles % 2
        wait_gather(xbufs[par])
        wait_scatter(ybufs[par])
        scatter(ybufs[1 - par])
        wait_scatter(ybufs[1 - par])


def _moe_ffn(gid, valid, src, dst, h2d, w_gu, w_d, out_rows, spare_row, tm):
    n_tiles = gid.shape[0]
    _, d = h2d.shape
    f = w_d.shape[1]
    idx = lambda fn: pl.BlockSpec((1, 1, tm), fn, memory_space=pltpu.SMEM)
    last = n_tiles - 1
    grid_spec = pltpu.PrefetchScalarGridSpec(
        num_scalar_prefetch=2,
        grid=(n_tiles + 1,),
        in_specs=[idx(lambda i, gid, vld: (0, 0, 0)),
                  idx(lambda i, gid, vld: (jnp.minimum(i + 1, n_tiles), 0, 0)),
                  idx(lambda i, gid, vld: (i, 0, 0)),
                  pl.BlockSpec(memory_space=pl.ANY),
                  pl.BlockSpec((1, d, f), lambda i, gid, vld: (gid[jnp.minimum(i, last)], 0, 0)),
                  _resident((1, d, f), lambda i, gid, vld: (gid[jnp.minimum(i, last)], 0, 1)),
                  _resident((1, f, d), lambda i, gid, vld: (gid[jnp.minimum(i, last)], 0, 0))],
        out_specs=pl.BlockSpec(memory_space=pl.ANY),
        scratch_shapes=[pltpu.VMEM((tm, d), F32)] * 4 + [pltpu.SemaphoreType.DMA(())] * 2,
    )
    return pl.pallas_call(
        functools.partial(_moe_ffn_kernel, n_tiles=n_tiles, tm=tm, spare_row=spare_row),
        grid_spec=grid_spec,
        out_shape=jax.ShapeDtypeStruct((out_rows, d), F32),
        compiler_params=pltpu.CompilerParams(
            dimension_semantics=("arbitrary",), vmem_limit_bytes=VMEM_LIMIT),
        name="moe_ffn",
    )(gid, valid, src, src, dst, h2d, w_gu, w_gu, w_d)


def _combine_kernel(y0_ref, y1_ref, x_ref, gt_ref, prob_ref, fg_ref, o_ref):
    y = prob_ref[:, 0:1] * y0_ref[...] + prob_ref[:, 1:2] * y1_ref[...]
    x = x_ref[...] + (1.0 + gt_ref[0]) * y
    ms = jnp.mean(x * x, axis=-1, keepdims=True)
    o_ref[...] = x * lax.rsqrt(ms + NORM_EPS) * fg_ref[...]


def _combine(ys, x2d, gt, prob, final_g, tiles_per_batch, tc):
    n, d = x2d.shape
    nt = n // tc
    tok = pl.BlockSpec((tc, d), lambda i: (i, 0))
    return pl.pallas_call(
        _combine_kernel,
        grid=(nt,),
        in_specs=[tok, pl.BlockSpec((tc, d), lambda i: (nt + i, 0)), tok,
                  pl.BlockSpec((1, 1, d), lambda i: (i // tiles_per_batch, 0, 0)),
                  pl.BlockSpec((tc, TOP_K), lambda i: (i, 0)),
                  pl.BlockSpec((1, d), lambda i: (0, 0))],
        out_specs=tok,
        out_shape=jax.ShapeDtypeStruct((n, d), F32),
        compiler_params=pltpu.CompilerParams(
            dimension_semantics=("arbitrary",), vmem_limit_bytes=VMEM_LIMIT),
        name="moe_combine",
    )(ys, ys, x2d, gt[:, None, :], prob, final_g[None, :])


def _moe_layout(eid, rank, counts, n_experts, tm, n_tiles):
    n = eid.shape[0]
    rows = n_tiles * tm
    cnt = counts[0, :n_experts]
    tiles = (cnt + tm - 1) // tm
    tile_end = jnp.cumsum(tiles)
    offsets = (tile_end - tiles) * tm
    first_rank = jnp.cumsum(cnt) - cnt
    pos = (offsets[eid] + rank).reshape(-1)
    order = jnp.argsort(pos).astype(I32)
    tile_ids = jnp.arange(n_tiles, dtype=I32)
    last_id = jnp.minimum(tile_ids, tile_end[-1] - 1)
    gid = jnp.sum((last_id[:, None] >= tile_end[None, :]).astype(I32), axis=1)
    p = jnp.arange(rows, dtype=I32)
    e_of_p = jnp.repeat(gid, tm)
    within = p - offsets[e_of_p]
    valid = within < cnt[e_of_p]
    flat = order[jnp.clip(first_rank[e_of_p] + within, 0, TOP_K * n - 1)]
    tok = flat // TOP_K
    src = jnp.where(valid, tok, 0)
    pad_id = jnp.cumsum(jnp.logical_not(valid).astype(I32)) - 1
    dst = jnp.where(valid, (flat % TOP_K) * n + tok, TOP_K * n + pad_id)
    spare = rows
    src = jnp.concatenate([src, jnp.zeros((tm,), I32)]).reshape(n_tiles + 1, 1, tm)
    dst = jnp.concatenate([spare + jnp.arange(tm, dtype=I32), dst]).reshape(n_tiles + 1, 1, tm)
    tile_valid = (tile_ids < tile_end[-1]).astype(I32)
    return gid, tile_valid, src.astype(I32), dst.astype(I32), spare + 2 * tm, spare + tm


def kernel(x, c, ada_w, ada_b, norm_g, final_g, rwkv_mu, rwkv_w_rkv, rwkv_w_o, rwkv_w0, rwkv_w1, rwkv_w2, rwkv_a0, rwkv_a1, rwkv_a2, rwkv_g1, rwkv_g2, rwkv_k_k, rwkv_k_a, rwkv_r_k, rwkv_gn_w, rwkv_gn_b, lru_w_in, lru_conv_w, lru_conv_b, lru_w_gates, lru_b_gates, lru_lam, lru_w_out, ffn_w_gu, ffn_w_d, moe_w_router, moe_b_router, moe_w_gu, moe_w_d):
    b, t, d = x.shape
    n = b * t
    n_experts = moe_w_router.shape[-1]
    mod = _ada_mod(c, ada_w, ada_b)

    def mods(i):
        return [mod[i, :, q * d:(q + 1) * d] for q in range(6)]

    sh1, sc1, gt1, sh2, sc2, gt2 = mods(0)
    (r, lw, k2, v, kk, bv, g, bonus), (w_o_b, ffn_gu_b, ffn_d_b, lru_in_b, lru_gates_b, lru_out_b) = _rwkv_pre(
        x, sh1, sc1, norm_g[0, 0], rwkv_mu[0], rwkv_w_rkv[0], rwkv_w1[0], rwkv_w2[0],
        rwkv_a1[0], rwkv_a2[0], rwkv_g1[0], rwkv_g2[0], rwkv_w0[0], rwkv_a0[0],
        rwkv_k_k[0], rwkv_k_a[0], rwkv_r_k[0], tm=min(256, t),
        later_weights=[rwkv_w_o[0], ffn_w_gu[0], ffn_w_d[0], lru_w_in[0], lru_w_gates[0], lru_w_out[0]])
    yg = _wkv_scan(r, lw, k2, v, kk, bv, g, bonus, rwkv_gn_w[0], rwkv_gn_b[0],
                   chunks_per_step=min(4, t // CHUNK))
    x, (moe_gu_b,) = _ffn_dense(yg, x, gt1, sh2, sc2, gt2, norm_g[0, 1], w_o_b, ffn_gu_b, ffn_d_b,
                                tm=min(256, t), later_weights=[moe_w_gu[0]])

    sh1, sc1, gt1, sh2, sc2, gt2 = mods(1)
    x, (moe_d_b,) = _lru_block(x, sh1, sc1, gt1, norm_g[1, 0], lru_in_b, lru_conv_w[0], lru_conv_b[0],
                               lru_gates_b, lru_b_gates[0], lru_lam[0], lru_out_b, tm=min(256, t),
                               later_weights=[moe_w_d[0]])

    x2d = x.reshape(n, d)
    tm_r = min(512, t)
    h2, eid, rank, prob, counts = _router(x2d, sh2, sc2, norm_g[1, 1], moe_w_router[0],
                                          moe_b_router[0], t // tm_r, tm_r)
    tm_g = min(256, t)
    n_tiles = (TOP_K * n) // tm_g + n_experts
    gid, tile_valid, src, dst, out_rows, spare_row = _moe_layout(eid, rank, counts, n_experts, tm_g, n_tiles)
    ys = _moe_ffn(gid, tile_valid, src, dst, h2, moe_gu_b, moe_d_b, out_rows, spare_row, tm_g)
    tc = min(512, t)
    out = _combine(ys, x2d, gt2, prob, final_g, t // tc, tc)
    return out.reshape(b, t, d)
```

```python
import functools

import jax
import jax.numpy as jnp
from jax import lax
from jax.experimental import pallas as pl
from jax.experimental.pallas import tpu as pltpu

F32 = jnp.float32
BF16 = jnp.bfloat16
I32 = jnp.int32

HEAD = 64
CHUNK = 64
GROUP = 256
HEADS_PER_GROUP = GROUP // HEAD
GN_EPS = 64e-5
NORM_EPS = 1e-6
LRU_C = 8.0
CONV_WIDTH = 4
LRU_BLOCK = 256
TOP_K = 2
LANES = 128
VMEM_LIMIT = 56 * 1024 * 1024


def _dot(a, b):
    return jnp.dot(a, b, preferred_element_type=F32)


def _dot_nt(a, b):
    return lax.dot_general(a, b, (((1,), (1,)), ((), ())), preferred_element_type=F32)


def _dot_tn(a, b):
    return lax.dot_general(a, b, (((0,), (0,)), ((), ())), preferred_element_type=F32)


def _softplus(u):
    return jnp.maximum(u, 0.0) + jnp.log1p(jnp.exp(-jnp.abs(u)))


def _modulate(x, ng, sh, sc):
    ms = jnp.mean(x * x, axis=-1, keepdims=True)
    return x * lax.rsqrt(ms + NORM_EPS) * ng * (1.0 + sc) + sh


def _split_bf16(x):
    hi = x.astype(BF16)
    lo = (x - hi.astype(F32)).astype(BF16)
    return hi, lo


def _cast_specs(weights, steps, step_index):
    ins, specs, out_shapes = [], [], []
    for w in weights:
        w2 = w.reshape(-1, w.shape[-1])
        rows = w2.shape[0]
        nblk = max(n for n in range(1, steps + 1) if rows % n == 0 and (rows // n) % 16 == 0)
        specs.append(pl.BlockSpec(
            (rows // nblk, w2.shape[1]),
            lambda *idx, nblk=nblk: (jnp.minimum(step_index(*idx), nblk - 1), 0)))
        ins.append(w2)
        out_shapes.append(jax.ShapeDtypeStruct(w2.shape, BF16))
    return ins, specs, out_shapes


def _with_casts(body, n_in, n_out, n_cast):
    def kernel(*refs):
        cast_in = refs[n_in:n_in + n_cast]
        outs = refs[n_in + n_cast:n_in + n_cast + n_out]
        cast_out = refs[n_in + n_cast + n_out:n_in + 2 * n_cast + n_out]
        for src, dst in zip(cast_in, cast_out):
            dst[...] = src[...].astype(BF16)
        body(*refs[:n_in], *outs, *refs[n_in + 2 * n_cast + n_out:])
    return kernel


def _shift_select(tm, shifts):
    rr = jnp.arange(tm)[:, None]
    cc = jnp.arange(tm)[None, :]
    return jnp.concatenate([(cc == rr - s) for s in shifts], axis=0).astype(BF16)


def _mod_kernel(c_ref, w_ref, b_ref, o_ref):
    c = c_ref[...]
    cond = c * jax.nn.sigmoid(c)
    o_ref[0] = _dot(cond.astype(BF16), w_ref[0].astype(BF16)) + b_ref[0]


def _ada_mod(c, ada_w, ada_b):
    depth, d, d6 = ada_w.shape
    b = c.shape[0]
    rows = 8
    c8 = jnp.pad(c, ((0, rows - b), (0, 0)))
    tn = 1024
    out = pl.pallas_call(
        _mod_kernel,
        grid=(depth, d6 // tn),
        in_specs=[
            pl.BlockSpec((rows, d), lambda i, j: (0, 0)),
            pl.BlockSpec((1, d, tn), lambda i, j: (i, 0, j)),
            pl.BlockSpec((1, 1, tn), lambda i, j: (i, 0, j)),
        ],
        out_specs=pl.BlockSpec((1, rows, tn), lambda i, j: (i, 0, j)),
        out_shape=jax.ShapeDtypeStruct((depth, rows, d6), F32),
        name="ada_mod",
    )(c8, ada_w, ada_b.reshape(depth, 1, d6))
    return out[:, :b]


def _rwkv_pre_kernel(x_ref, sh_ref, sc_ref, ng_ref, mu_ref, wrkv_ref, w1_ref, w2_ref,
                     a1_ref, a2_ref, g1_ref, g2_ref, vec_ref, seg_ref, segt_ref, sel_ref,
                     r_out, lw_out, k_out, v_out, kk_out, b_out, g_out, bonus_out,
                     hbuf):
    t = pl.program_id(1)
    tm = x_ref.shape[1]
    d = x_ref.shape[2]

    h = _modulate(x_ref[0], ng_ref[...], sh_ref[0], sc_ref[0])

    @pl.when(t == 0)
    def _():
        hbuf[...] = jnp.zeros((8, d), F32)

    hprev = _dot(sel_ref[...], h.astype(BF16))
    first = lax.broadcasted_iota(I32, (tm, d), 0) == 0
    hprev = jnp.where(first, hbuf[7:8, :], hprev)
    hbuf[...] = h[tm - 8:tm, :]
    xx = hprev - h

    def mix(p):
        return (h + xx * mu_ref[p:p + 1, :]).astype(BF16)

    r = _dot(mix(0), wrkv_ref[0])
    k = _dot(mix(1), wrkv_ref[1])
    v = _dot(mix(2), wrkv_ref[2])
    wl = _dot(jnp.tanh(_dot(mix(3), w1_ref[...])).astype(BF16), w2_ref[...])
    al = _dot(_dot(mix(4), a1_ref[...]).astype(BF16), a2_ref[...])
    g = _dot(jax.nn.sigmoid(_dot(mix(5), g1_ref[...])).astype(BF16), g2_ref[...])

    w0 = vec_ref[0:1, :]
    a0 = vec_ref[1:2, :]
    k_k = vec_ref[2:3, :]
    k_a = vec_ref[3:4, :]
    r_k = vec_ref[4:5, :]

    def headsum(z):
        s = _dot(z.astype(BF16), seg_ref[...])
        s_hi, s_lo = _split_bf16(s)
        return _dot(jnp.concatenate([s_hi, s_lo], axis=1), segt_ref[...])

    lw = -0.6065306597126334 * jax.nn.sigmoid(w0 + wl)
    a = jax.nn.sigmoid(a0 + al)
    kk = k * k_k
    kk = kk * lax.rsqrt(jnp.maximum(headsum(kk * kk), 1e-24))
    k2 = k * (1.0 + (a - 1.0) * k_a)
    bonus = headsum(r * k2 * r_k) * v

    r_out[0] = r
    lw_out[0] = lw
    k_out[0] = k2
    v_out[0] = v
    kk_out[0] = kk
    b_out[0] = kk * a
    g_out[0] = g
    bonus_out[0] = bonus


def _pad_to(x, axis, size):
    pad = [(0, 0)] * x.ndim
    pad[axis] = (0, size - x.shape[axis])
    return jnp.pad(x, pad)


def _rwkv_pre(x, sh, sc, ng, mu, w_rkv, w1, w2, a1, a2, g1, g2, w0, a0, k_k, k_a, r_k, tm,
              later_weights):
    b, t, d = x.shape
    nh = d // HEAD
    lw_pad = LANES * pl.cdiv(w1.shape[1], LANES)
    la_pad = LANES * pl.cdiv(a1.shape[1], LANES)
    lg_pad = LANES * pl.cdiv(g1.shape[1], LANES)
    vecs = _pad_to(jnp.stack([w0, a0, k_k, k_a, r_k.reshape(d)]), 0, 8)
    head_of_lane = jnp.arange(d) // HEAD
    seg = (head_of_lane[:, None] == jnp.arange(LANES)[None, :]).astype(BF16)
    segt = jnp.concatenate([seg.T, seg.T], axis=0)
    del nh
    full = lambda *shape: pl.BlockSpec(shape, lambda bi, ti: (0,) * len(shape))
    tok = pl.BlockSpec((1, tm, d), lambda bi, ti: (bi, ti, 0))
    vec = pl.BlockSpec((1, 1, d), lambda bi, ti: (bi, 0, 0))
    nt = t // tm
    cast_in, cast_specs, cast_shapes = _cast_specs(later_weights, b * nt, lambda bi, ti: bi * nt + ti)
    in_specs = [tok, vec, vec, full(1, d), full(8, d), full(3, d, d),
                full(d, lw_pad), full(lw_pad, d), full(d, la_pad), full(la_pad, d),
                full(d, lg_pad), full(lg_pad, d), full(8, d), full(d, LANES), full(2 * LANES, d),
                full(tm, tm)]
    outs = pl.pallas_call(
        _with_casts(_rwkv_pre_kernel, len(in_specs), 8, len(cast_in)),
        grid=(b, nt),
        in_specs=in_specs + cast_specs,
        out_specs=[tok] * 8 + cast_specs,
        out_shape=[jax.ShapeDtypeStruct((b, t, d), F32)] * 8 + cast_shapes,
        scratch_shapes=[pltpu.VMEM((8, d), F32)],
        compiler_params=pltpu.CompilerParams(
            dimension_semantics=("arbitrary", "arbitrary"), vmem_limit_bytes=VMEM_LIMIT),
        name="rwkv_pre",
    )(x, sh[:, None, :], sc[:, None, :], ng[None, :], _pad_to(mu, 0, 8), w_rkv.astype(BF16),
      _pad_to(w1, 1, lw_pad).astype(BF16), _pad_to(w2, 0, lw_pad).astype(BF16),
      _pad_to(a1, 1, la_pad).astype(BF16), _pad_to(a2, 0, la_pad).astype(BF16),
      _pad_to(g1, 1, lg_pad).astype(BF16), _pad_to(g2, 0, lg_pad).astype(BF16),
      vecs, seg, segt, _shift_select(tm, [1]), *cast_in)
    casts = [o.reshape(w.shape) for o, w in zip(outs[8:], later_weights)]
    return outs[:8], casts


def _wkv_scan_kernel(r_ref, lw_ref, k_ref, v_ref, kk_ref, b_ref, g_ref, bonus_ref,
                     gnw_ref, gnb_ref, o_ref, h_scr, *, chunks_per_step):
    L = CHUNK
    W = GROUP
    ngroups = r_ref.shape[2] // W

    @pl.when(pl.program_id(1) == 0)
    def _():
        h_scr[...] = jnp.zeros_like(h_scr)

    row = lax.broadcasted_iota(I32, (L, W), 0)
    lane = lax.broadcasted_iota(I32, (L, W), 1)
    sidx = lane & (L - 1)
    lane_head = lane >> 6
    strict = sidx < row
    incl = sidx <= row
    eye = (sidx == row).astype(F32)
    same16 = (row >> 4) == (sidx >> 4)
    same32 = (row >> 5) == (sidx >> 5)
    m16 = strict & same16
    m32 = strict & same32 & jnp.logical_not(same16)
    m64 = strict & jnp.logical_not(same32)
    rb = lax.broadcasted_iota(I32, (W, W), 0)
    cb = lax.broadcasted_iota(I32, (W, W), 1)
    bmask = (rb >> 6) == (cb >> 6)
    diag = rb == cb
    ones_bd = bmask.astype(BF16)
    tri_r = lax.broadcasted_iota(I32, (L, 3 * L), 0)
    tri_c = lax.broadcasted_iota(I32, (L, 3 * L), 1)
    tri3 = ((tri_c & (L - 1)) <= tri_r).astype(BF16)

    def bd(y):
        yt = jnp.concatenate([y] * HEADS_PER_GROUP, axis=0)
        return jnp.where(bmask, yt, 0.0).astype(BF16)

    def hmm(x, ybd):
        return _dot(x.astype(BF16), ybd)

    streams = [(q, j) for j in range(chunks_per_step) for q in range(ngroups)]
    S = range(len(streams))

    def ld(ref, s):
        q, j = streams[s]
        return ref[0, j * L:(j + 1) * L, q * W:(q + 1) * W]

    r = [ld(r_ref, s) for s in S]
    lw = [ld(lw_ref, s) for s in S]
    k = [ld(k_ref, s) for s in S]
    v = [ld(v_ref, s) for s in S]
    kk = [ld(kk_ref, s) for s in S]
    bv = [ld(b_ref, s) for s in S]

    def cumlog(x):
        hi = x.astype(BF16)
        rem = x - hi.astype(F32)
        mid = rem.astype(BF16)
        lo = (rem - mid.astype(F32)).astype(BF16)
        return _dot(tri3, jnp.concatenate([hi, mid, lo], axis=0))

    cl = [cumlog(lw[s]) for s in S]
    cl_last = [cl[s][L - 1:L, :] for s in S]
    e_pos = [jnp.exp(cl[s]) for s in S]
    e_neg = [jnp.exp(-cl[s]) for s in S]
    e_end = [jnp.exp(cl_last[s] - cl[s]) for s in S]
    rt = [r[s] * e_pos[s] for s in S]
    at = [-kk[s] * jnp.exp(cl[s] - lw[s]) for s in S]
    bt = [bv[s] * e_neg[s] for s in S]
    kt = [k[s] * e_neg[s] for s in S]

    def gram(s):
        x = jnp.concatenate([at[s], rt[s]], axis=0).astype(BF16)
        ys = [jnp.where(lane_head == hh, bt[s], 0.0) for hh in range(HEADS_PER_GROUP)]
        ys += [jnp.where(lane_head == hh, kt[s], 0.0) for hh in range(HEADS_PER_GROUP)]
        return _dot_nt(x, jnp.concatenate(ys, axis=0).astype(BF16))

    gm = [gram(s) for s in S]
    a_ab = [jnp.where(strict, gm[s][:L, :W], 0.0) for s in S]
    a_ak = [jnp.where(strict, gm[s][:L, W:], 0.0) for s in S]
    a_rb = [jnp.where(incl, gm[s][L:, :W], 0.0) for s in S]
    a_rk = [jnp.where(incl, gm[s][L:, W:], 0.0) for s in S]

    a0 = [jnp.where(m16, a_ab[s], 0.0) for s in S]
    pw = [hmm(a0[s], bd(a0[s])) for s in S]
    tinv = [eye + a0[s] for s in S]
    for _ in range(2):
        ts = [_dot(jnp.concatenate([tinv[s], pw[s]], axis=0).astype(BF16), bd(pw[s])) for s in S]
        tinv = [tinv[s] + ts[s][:L] for s in S]
        pw = [ts[s][L:] for s in S]
    tinv = [tinv[s] + hmm(tinv[s], bd(pw[s])) for s in S]
    for msk in (m32, m64):
        inner = [hmm(jnp.where(msk, a_ab[s], 0.0), bd(tinv[s])) for s in S]
        tinv = [tinv[s] + hmm(tinv[s], bd(inner[s])) for s in S]

    vbd = [bd(v[s]) for s in S]
    avs = [_dot(jnp.concatenate([a_ak[s], a_rk[s]], axis=0).astype(BF16), vbd[s]) for s in S]
    av = [avs[s][:L] for s in S]
    tx = [_dot(tinv[s].astype(BF16), jnp.concatenate([bd(at[s]), bd(av[s])], axis=1)) for s in S]
    ahat = [tx[s][:, :W] for s in S]
    vp = [tx[s][:, W:] for s in S]
    ox = [_dot(a_rb[s].astype(BF16), jnp.concatenate([bd(ahat[s]), bd(vp[s])], axis=1)) for s in S]
    rhat = [rt[s] + ox[s][:, :W] for s in S]
    o_intra = [ox[s][:, W:] + avs[s][L:] for s in S]

    def state_terms(s):
        z = jnp.concatenate([bv[s] * e_end[s], k[s] * e_end[s]], axis=0).astype(BF16)
        wm = jnp.concatenate(
            [jnp.concatenate([ahat[s], vp[s]], axis=1),
             jnp.concatenate([jnp.zeros((L, W), F32), v[s]], axis=1)], axis=0).astype(BF16)
        mn = _dot_tn(z, wm)
        m_mat = jnp.where(bmask, mn[:, :W], 0.0) + jnp.where(diag, jnp.exp(cl_last[s]), 0.0)
        return m_mat, jnp.where(bmask, mn[:, W:], 0.0)

    mn = [state_terms(s) for s in S]

    o = [None] * len(streams)
    hq = [h_scr[q] for q in range(ngroups)]
    for j in range(chunks_per_step):
        for q in range(ngroups):
            s = streams.index((q, j))
            m_hi, m_lo = _split_bf16(mn[s][0])
            lhs = jnp.concatenate([m_hi, m_lo, rhat[s].astype(BF16)], axis=0)
            res = _dot(lhs, hq[q].astype(BF16))
            o[s] = res[2 * W:] + o_intra[s]
            hq[q] = res[:W] + res[W:2 * W] + mn[s][1]
    for q in range(ngroups):
        h_scr[q] = hq[q]

    def headmean(zs):
        parts = []
        for z in zs:
            parts += list(_split_bf16(z))
        red = _dot(jnp.concatenate(parts, axis=0), ones_bd) * (1.0 / HEAD)
        return [red[2 * L * s:2 * L * s + L] + red[2 * L * s + L:2 * L * (s + 1)] for s in S]

    mean = headmean(o)
    dlt = [o[s] - mean[s] for s in S]
    sq = jnp.concatenate([(dlt[s] * dlt[s]).astype(BF16) for s in S], axis=0)
    var_all = _dot(sq, ones_bd) * (1.0 / HEAD)
    var = [var_all[L * s:L * (s + 1)] for s in S]
    for s in S:
        q, j = streams[s]
        gsl = slice(q * W, (q + 1) * W)
        rsl = slice(j * L, (j + 1) * L)
        yn = dlt[s] * lax.rsqrt(var[s] + GN_EPS) * gnw_ref[:, gsl] + gnb_ref[:, gsl]
        o_ref[0, rsl, gsl] = ((yn + bonus_ref[0, rsl, gsl]) * g_ref[0, rsl, gsl]).astype(o_ref.dtype)


def _wkv_scan(r, lw, k, v, kk, bv, g, bonus, gn_w, gn_b, chunks_per_step):
    b, t, d = r.shape
    lb = CHUNK * chunks_per_step
    tok = pl.BlockSpec((1, lb, d), lambda bi, ci: (bi, ci, 0))
    vec = pl.BlockSpec((1, d), lambda bi, ci: (0, 0))
    return pl.pallas_call(
        functools.partial(_wkv_scan_kernel, chunks_per_step=chunks_per_step),
        grid=(b, t // lb),
        in_specs=[tok] * 8 + [vec, vec],
        out_specs=tok,
        out_shape=jax.ShapeDtypeStruct((b, t, d), BF16),
        scratch_shapes=[pltpu.VMEM((d // GROUP, GROUP, GROUP), F32)],
        compiler_params=pltpu.CompilerParams(
            dimension_semantics=("arbitrary", "arbitrary"), vmem_limit_bytes=VMEM_LIMIT),
        name="wkv_scan",
    )(r, lw, k, v, kk, bv, g, bonus, gn_w[None, :], gn_b[None, :])


def _swiglu(h, wg, wu, wd):
    g = _dot(h, wg)
    u = _dot(h, wu)
    return _dot((g * jax.nn.sigmoid(g) * u).astype(BF16), wd)


def _ffn_dense_kernel(a_ref, x_ref, gt1_ref, sh_ref, sc_ref, gt2_ref, ng_ref, wo_ref,
                      wg_ref, wu_ref, wd_ref, o_ref):
    x1 = x_ref[0] + (1.0 + gt1_ref[0]) * _dot(a_ref[0], wo_ref[...])
    h = _modulate(x1, ng_ref[...], sh_ref[0], sc_ref[0]).astype(BF16)
    o_ref[0] = x1 + (1.0 + gt2_ref[0]) * _swiglu(h, wg_ref[...], wu_ref[...], wd_ref[...])


def _resident(shape, index_map):
    return pl.BlockSpec(shape, index_map, pipeline_mode=pl.Buffered(1))


def _ffn_dense(a, x, gt1, sh, sc, gt2, ng, w_o, w_gu, w_d, tm, later_weights):
    b, t, d = x.shape
    f = w_d.shape[0]
    nt = t // tm
    tok = pl.BlockSpec((1, tm, d), lambda bi, ti: (bi, ti, 0))
    vec = pl.BlockSpec((1, 1, d), lambda bi, ti: (bi, 0, 0))
    cast_in, cast_specs, cast_shapes = _cast_specs(later_weights, b * nt, lambda bi, ti: bi * nt + ti)
    in_specs = [tok, tok, vec, vec, vec, vec, pl.BlockSpec((1, d), lambda bi, ti: (0, 0)),
                _resident((d, d), lambda bi, ti: (0, 0)),
                _resident((d, f), lambda bi, ti: (0, 0)),
                _resident((d, f), lambda bi, ti: (0, 1)),
                _resident((f, d), lambda bi, ti: (0, 0))]
    outs = pl.pallas_call(
        _with_casts(_ffn_dense_kernel, len(in_specs), 1, len(cast_in)),
        grid=(b, nt),
        in_specs=in_specs + cast_specs,
        out_specs=[tok] + cast_specs,
        out_shape=[jax.ShapeDtypeStruct((b, t, d), F32)] + cast_shapes,
        compiler_params=pltpu.CompilerParams(
            dimension_semantics=("arbitrary", "arbitrary"), vmem_limit_bytes=VMEM_LIMIT),
        name="ffn_dense",
    )(a, x, gt1[:, None, :], sh[:, None, :], sc[:, None, :], gt2[:, None, :], ng[None, :],
      w_o, w_gu, w_gu, w_d, *cast_in)
    return outs[0], [o.reshape(w.shape) for o, w in zip(outs[1:], later_weights)]


def _lru_kernel(x_ref, sh_ref, sc_ref, gt_ref, ng_ref, win_ref, cw_ref, cb_ref, wg_ref, bg_ref,
                lam_ref, wout_ref, o_ref, xbuf, abuf, bbuf, hbuf, ga, gb, carry):
    t = pl.program_id(1)
    tm = x_ref.shape[1]
    w = win_ref.shape[1] // 2
    nblk = w // LRU_BLOCK
    nslab = w // LANES
    ng = tm // 8

    @pl.when(t == 0)
    def _():
        xbuf[0:8, :] = jnp.zeros((8, w), F32)
        carry[...] = jnp.zeros_like(carry)
        ga[:, :, 0:ng, :] = jnp.ones((2, nslab, ng, LANES), F32)
        gb[:, :, 0:ng, :] = jnp.zeros((2, nslab, ng, LANES), F32)

    x = x_ref[0]
    h = _modulate(x, ng_ref[...], sh_ref[0], sc_ref[0]).astype(BF16)
    xg = _dot(h, win_ref[...])
    xb = xg[:, :w]
    gx = xg[:, w:]
    gate = 0.5 * gx * (1.0 + jnp.tanh(0.7978845608028654 * (gx + 0.044715 * gx * gx * gx)))

    xbuf[8:8 + tm, :] = xb
    conv = cb_ref[...] + cw_ref[CONV_WIDTH - 1:CONV_WIDTH, :] * xb
    for jj in range(CONV_WIDTH - 1):
        shift = CONV_WIDTH - 1 - jj
        conv = conv + cw_ref[jj:jj + 1, :] * xbuf[8 - shift:8 - shift + tm, :]
    xbuf[0:8, :] = xb[tm - 8:tm, :]

    conv_b = conv.astype(BF16)
    rs, is_ = [], []
    for n in range(nblk):
        gts = _dot(conv_b[:, n * LRU_BLOCK:(n + 1) * LRU_BLOCK], wg_ref[n]) + bg_ref[n]
        gts = jax.nn.sigmoid(gts)
        rs.append(gts[:, :LRU_BLOCK])
        is_.append(gts[:, LRU_BLOCK:])
    r_t = jnp.concatenate(rs, axis=1)
    i_t = jnp.concatenate(is_, axis=1)

    log_a = -LRU_C * r_t * _softplus(-lam_ref[...])
    a_t = jnp.exp(log_a)
    b_t = jnp.sqrt(-jnp.tanh(log_a) * (a_t * a_t + 1.0)) * (i_t * conv)

    for j in range(nslab):
        abuf[j] = a_t[:, j * LANES:(j + 1) * LANES]
        bbuf[j] = b_t[:, j * LANES:(j + 1) * LANES]
    hs_slabs = []
    for j in range(nslab):
        a_loc = [abuf[j, pl.ds(0, ng, stride=8), :]]
        b_loc = [bbuf[j, pl.ds(0, ng, stride=8), :]]
        for s in range(1, 8):
            a_s = abuf[j, pl.ds(s, ng, stride=8), :]
            b_s = bbuf[j, pl.ds(s, ng, stride=8), :]
            b_loc.append(a_s * b_loc[-1] + b_s)
            a_loc.append(a_s * a_loc[-1])
        ga[0, j, ng:2 * ng, :] = a_loc[-1]
        gb[0, j, ng:2 * ng, :] = b_loc[-1]
        step = 1
        src = 0
        while step < ng:
            a_cur = ga[src, j, ng:2 * ng, :]
            b_cur = gb[src, j, ng:2 * ng, :]
            a_sh = ga[src, j, ng - step:2 * ng - step, :]
            b_sh = gb[src, j, ng - step:2 * ng - step, :]
            ga[1 - src, j, ng:2 * ng, :] = a_cur * a_sh
            gb[1 - src, j, ng:2 * ng, :] = a_cur * b_sh + b_cur
            src = 1 - src
            step *= 2
        c_in = carry[0:1, j * LANES:(j + 1) * LANES]
        h_in = gb[src, j, ng - 1:2 * ng - 1, :] + ga[src, j, ng - 1:2 * ng - 1, :] * c_in
        for s in range(8):
            h_s = b_loc[s] + a_loc[s] * h_in
            hbuf[j, pl.ds(s, ng, stride=8), :] = h_s
        carry[:, j * LANES:(j + 1) * LANES] = jnp.broadcast_to(h_s[ng - 1:ng, :], (8, LANES))
        hs_slabs.append(hbuf[j])
    hs = jnp.concatenate(hs_slabs, axis=1)

    y = _dot((hs * gate).astype(BF16), wout_ref[...])
    o_ref[0] = x + (1.0 + gt_ref[0]) * y


def _lru_block(x, sh, sc, gt, ng, w_in, conv_w, conv_b, w_gates, b_gates, lam, w_out, tm,
               later_weights):
    b, t, d = x.shape
    w = w_out.shape[0]
    nblk = w // LRU_BLOCK
    nt = t // tm
    full = lambda *shape: pl.BlockSpec(shape, lambda bi, ti: (0,) * len(shape))
    tok = pl.BlockSpec((1, tm, d), lambda bi, ti: (bi, ti, 0))
    vec = pl.BlockSpec((1, 1, d), lambda bi, ti: (bi, 0, 0))
    cast_in, cast_specs, cast_shapes = _cast_specs(later_weights, b * nt, lambda bi, ti: bi * nt + ti)
    in_specs = [tok, vec, vec, vec, full(1, d), full(d, 2 * w), full(CONV_WIDTH, w), full(1, w),
                full(nblk, LRU_BLOCK, 2 * LRU_BLOCK), full(nblk, 1, 2 * LRU_BLOCK),
                full(1, w), full(w, d)]
    outs = pl.pallas_call(
        _with_casts(_lru_kernel, len(in_specs), 1, len(cast_in)),
        grid=(b, nt),
        in_specs=in_specs + cast_specs,
        out_specs=[tok] + cast_specs,
        out_shape=[jax.ShapeDtypeStruct((b, t, d), F32)] + cast_shapes,
        scratch_shapes=[pltpu.VMEM((tm + 8, w), F32)]
        + [pltpu.VMEM((w // LANES, tm, LANES), F32)] * 3
        + [pltpu.VMEM((2, w // LANES, tm // 4, LANES), F32)] * 2
        + [pltpu.VMEM((8, w), F32)],
        compiler_params=pltpu.CompilerParams(
            dimension_semantics=("arbitrary", "arbitrary"), vmem_limit_bytes=VMEM_LIMIT),
        name="rglru_block",
    )(x, sh[:, None, :], sc[:, None, :], gt[:, None, :], ng[None, :], w_in,
      conv_w, conv_b[None, :], w_gates, b_gates[:, None, :], lam[None, :], w_out, *cast_in)
    return outs[0], [o.reshape(wt.shape) for o, wt in zip(outs[1:], later_weights)]


def _router_kernel(x_ref, sh_ref, sc_ref, ng_ref, wr_ref, br_ref, tri_ref,
                   h_out, eid_out, rank_out, prob_out, cnt_out, cnt_scr):
    i = pl.program_id(0)

    @pl.when(i == 0)
    def _():
        cnt_scr[...] = jnp.zeros_like(cnt_scr)

    h = _modulate(x_ref[...], ng_ref[...], sh_ref[0], sc_ref[0])
    h_out[...] = h
    h_hi, h_lo = _split_bf16(h)
    w_hi, w_lo = _split_bf16(wr_ref[...])
    logits = _dot_nt(w_hi, h_hi) + _dot_nt(w_hi, h_lo) + _dot_nt(w_lo, h_hi) + br_ref[:, 0:1]

    n_rows = logits.shape[0]
    eidx = lax.broadcasted_iota(I32, logits.shape, 0)
    m1 = jnp.max(logits, axis=0, keepdims=True)
    i1 = jnp.min(jnp.where(logits == m1, eidx, n_rows), axis=0, keepdims=True)
    l2 = jnp.where(eidx == i1, -jnp.inf, logits)
    m2 = jnp.max(l2, axis=0, keepdims=True)
    i2 = jnp.min(jnp.where(l2 == m2, eidx, n_rows), axis=0, keepdims=True)
    e = jnp.exp(m2 - m1)
    p1 = 1.0 / (1.0 + e)
    p2 = e / (1.0 + e)

    oh1 = (eidx == i1).astype(F32)
    oh2 = (eidx == i2).astype(F32)
    oh = oh1 + oh2
    before = _dot(oh.astype(BF16), tri_ref[...]) + cnt_scr[:, 0:1]
    rank1 = jnp.sum(before * oh1, axis=0, keepdims=True)
    rank2 = jnp.sum(before * oh2, axis=0, keepdims=True)
    cnt_scr[...] = cnt_scr[...] + jnp.sum(oh, axis=1, keepdims=True)

    eid_out[0:1, :] = i1
    eid_out[1:2, :] = i2
    rank_out[0:1, :] = rank1.astype(I32)
    rank_out[1:2, :] = rank2.astype(I32)
    prob_out[0:1, :] = p1
    prob_out[1:2, :] = p2
    cnt_out[...] = cnt_scr[...].astype(I32)


def _router(x2d, sh, sc, ng, w_router, b_router, tiles_per_batch, tm):
    n, d = x2d.shape
    e = w_router.shape[1]
    ep = 16 * pl.cdiv(e, 16)
    wr = _pad_to(w_router.T, 0, ep)
    br = jnp.broadcast_to(jnp.concatenate([b_router, jnp.full((ep - e,), -1e30, F32)])[:, None],
                          (ep, LANES))
    tri = (jnp.arange(tm)[:, None] < jnp.arange(tm)[None, :]).astype(BF16)
    tok = pl.BlockSpec((tm, d), lambda i: (i, 0))
    vec = pl.BlockSpec((1, 1, d), lambda i: (i // tiles_per_batch, 0, 0))
    two = pl.BlockSpec((TOP_K, tm), lambda i: (0, i))
    const = lambda *shape: pl.BlockSpec(shape, lambda i: (0,) * len(shape))
    return pl.pallas_call(
        _router_kernel,
        grid=(n // tm,),
        in_specs=[tok, vec, vec, const(1, d), const(ep, d), const(ep, LANES), const(tm, tm)],
        out_specs=[tok, two, two, two, const(ep, LANES)],
        out_shape=[jax.ShapeDtypeStruct((n, d), F32), jax.ShapeDtypeStruct((TOP_K, n), I32),
                   jax.ShapeDtypeStruct((TOP_K, n), I32), jax.ShapeDtypeStruct((TOP_K, n), F32),
                   jax.ShapeDtypeStruct((ep, LANES), I32)],
        scratch_shapes=[pltpu.VMEM((ep, LANES), F32)],
        compiler_params=pltpu.CompilerParams(
            dimension_semantics=("arbitrary",), vmem_limit_bytes=VMEM_LIMIT),
        name="moe_router",
    )(x2d, sh[:, None, :], sc[:, None, :], ng[None, :], wr, br, tri)


def _moe_ffn_kernel(gid_ref, valid_ref, src0_ref, src_next_ref, dst_prev_ref, h_hbm,
                    wg_ref, wu_ref, wd_ref, ys_hbm, x0, x1, y0, y1, gsem, ssem,
                    *, n_tiles, tm, spare_row):
    del gid_ref
    i = pl.program_id(0)
    xbufs = (x0, x1)
    ybufs = (y0, y1)
    valid = valid_ref[jnp.minimum(i, n_tiles - 1)] > 0

    def gather(src_ref, xbuf):
        for t in range(tm):
            pltpu.make_async_copy(h_hbm.at[pl.ds(src_ref[0, 0, t], 1)], xbuf.at[pl.ds(t, 1)],
                                  gsem).start()

    def scatter(ybuf):
        for t in range(tm):
            pltpu.make_async_copy(ybuf.at[pl.ds(t, 1)],
                                  ys_hbm.at[pl.ds(dst_prev_ref[0, 0, t], 1)], ssem).start()

    def wait_gather(xbuf):
        pltpu.make_async_copy(h_hbm.at[pl.ds(0, tm)], xbuf, gsem).wait()

    def wait_scatter(ybuf):
        pltpu.make_async_copy(ybuf, ys_hbm.at[pl.ds(0, tm)], ssem).wait()

    @pl.when(i == 0)
    def _():
        y0[...] = jnp.zeros_like(y0)
        y1[...] = jnp.zeros_like(y1)
        pltpu.make_async_copy(y0, ys_hbm.at[pl.ds(spare_row, tm)], ssem).start()
        gather(src0_ref, x0)

    for par in range(2):
        for compute in (True, False):
            @pl.when((i < n_tiles) & (i % 2 == par) & (valid if compute else jnp.logical_not(valid)))
            def _(par=par, compute=compute):
                xa, xb, ya, yb = xbufs[par], xbufs[1 - par], ybufs[par], ybufs[1 - par]
                wait_gather(xa)
                wait_scatter(ya)
                gather(src_next_ref, xb)
                scatter(yb)
                if compute:
                    ya[...] = _swiglu(xa[...].astype(BF16), wg_ref[0], wu_ref[0], wd_ref[0])

    @pl.when(i == n_tiles)
    def _():
        par = n_tiles % 2
        wait_gather(xbufs[par])
        wait_scatter(ybufs[par])
        scatter(ybufs[1 - par])
        wait_scatter(ybufs[1 - par])


def _moe_ffn(gid, valid, src, dst, h2d, w_gu, w_d, out_rows, spare_row, tm):
    n_tiles = gid.shape[0]
    _, d = h2d.shape
    f = w_d.shape[1]
    idx = lambda fn: pl.BlockSpec((1, 1, tm), fn, memory_space=pltpu.SMEM)
    last = n_tiles - 1
    grid_spec = pltpu.PrefetchScalarGridSpec(
        num_scalar_prefetch=2,
        grid=(n_tiles + 1,),
        in_specs=[idx(lambda i, gid, vld: (0, 0, 0)),
                  idx(lambda i, gid, vld: (jnp.minimum(i + 1, n_tiles), 0, 0)),
                  idx(lambda i, gid, vld: (i, 0, 0)),
                  pl.BlockSpec(memory_space=pl.ANY),
                  pl.BlockSpec((1, d, f), lambda i, gid, vld: (gid[jnp.minimum(i, last)], 0, 0)),
                  _resident((1, d, f), lambda i, gid, vld: (gid[jnp.minimum(i, last)], 0, 1)),
                  _resident((1, f, d), lambda i, gid, vld: (gid[jnp.minimum(i, last)], 0, 0))],
        out_specs=pl.BlockSpec(memory_space=pl.ANY),
        scratch_shapes=[pltpu.VMEM((tm, d), F32)] * 4 + [pltpu.SemaphoreType.DMA(())] * 2,
    )
    return pl.pallas_call(
        functools.partial(_moe_ffn_kernel, n_tiles=n_tiles, tm=tm, spare_row=spare_row),
        grid_spec=grid_spec,
        out_shape=jax.ShapeDtypeStruct((out_rows, d), F32),
        compiler_params=pltpu.CompilerParams(
            dimension_semantics=("arbitrary",), vmem_limit_bytes=VMEM_LIMIT),
        name="moe_ffn",
    )(gid, valid, src, src, dst, h2d, w_gu, w_gu, w_d)


def _combine_kernel(y0_ref, y1_ref, x_ref, gt_ref, prob_ref, fg_ref, o_ref):
    y = prob_ref[:, 0:1] * y0_ref[...] + prob_ref[:, 1:2] * y1_ref[...]
    x = x_ref[...] + (1.0 + gt_ref[0]) * y
    ms = jnp.mean(x * x, axis=-1, keepdims=True)
    o_ref[...] = x * lax.rsqrt(ms + NORM_EPS) * fg_ref[...]


def _combine(ys, x2d, gt, prob, final_g, tiles_per_batch, tc):
    n, d = x2d.shape
    nt = n // tc
    tok = pl.BlockSpec((tc, d), lambda i: (i, 0))
    return pl.pallas_call(
        _combine_kernel,
        grid=(nt,),
        in_specs=[tok, pl.BlockSpec((tc, d), lambda i: (nt + i, 0)), tok,
                  pl.BlockSpec((1, 1, d), lambda i: (i // tiles_per_batch, 0, 0)),
                  pl.BlockSpec((tc, TOP_K), lambda i: (i, 0)),
                  pl.BlockSpec((1, d), lambda i: (0, 0))],
        out_specs=tok,
        out_shape=jax.ShapeDtypeStruct((n, d), F32),
        compiler_params=pltpu.CompilerParams(
            dimension_semantics=("arbitrary",), vmem_limit_bytes=VMEM_LIMIT),
        name="moe_combine",
    )(ys, ys, x2d, gt[:, None, :], prob, final_g[None, :])


def _moe_layout(eid, rank, counts, n_experts, tm, n_tiles):
    n = eid.shape[1]
    rows = n_tiles * tm
    cnt = counts[:n_experts, 0]
    tiles = (cnt + tm - 1) // tm
    tile_end = jnp.cumsum(tiles)
    offsets = (tile_end - tiles) * tm
    first_rank = jnp.cumsum(cnt) - cnt
    pos = (offsets[eid] + rank).T.reshape(-1)
    order = jnp.argsort(pos).astype(I32)
    tile_ids = jnp.arange(n_tiles, dtype=I32)
    last_id = jnp.minimum(tile_ids, tile_end[-1] - 1)
    gid = jnp.sum((last_id[:, None] >= tile_end[None, :]).astype(I32), axis=1)
    p = jnp.arange(rows, dtype=I32)
    e_of_p = jnp.repeat(gid, tm)
    within = p - offsets[e_of_p]
    valid = within < cnt[e_of_p]
    flat = order[jnp.clip(first_rank[e_of_p] + within, 0, TOP_K * n - 1)]
    tok = flat // TOP_K
    src = jnp.where(valid, tok, 0)
    pad_id = jnp.cumsum(jnp.logical_not(valid).astype(I32)) - 1
    dst = jnp.where(valid, (flat % TOP_K) * n + tok, TOP_K * n + pad_id)
    spare = rows
    src = jnp.concatenate([src, jnp.zeros((tm,), I32)]).reshape(n_tiles + 1, 1, tm)
    dst = jnp.concatenate([spare + jnp.arange(tm, dtype=I32), dst]).reshape(n_tiles + 1, 1, tm)
    tile_valid = (tile_ids < tile_end[-1]).astype(I32)
    return gid, tile_valid, src.astype(I32), dst.astype(I32), spare + 2 * tm, spare + tm


def kernel(x, c, ada_w, ada_b, norm_g, final_g, rwkv_mu, rwkv_w_rkv, rwkv_w_o, rwkv_w0, rwkv_w1, rwkv_w2, rwkv_a0, rwkv_a1, rwkv_a2, rwkv_g1, rwkv_g2, rwkv_k_k, rwkv_k_a, rwkv_r_k, rwkv_gn_w, rwkv_gn_b, lru_w_in, lru_conv_w, lru_conv_b, lru_w_gates, lru_b_gates, lru_lam, lru_w_out, ffn_w_gu, ffn_w_d, moe_w_router, moe_b_router, moe_w_gu, moe_w_d):
    b, t, d = x.shape
    n = b * t
    n_experts = moe_w_router.shape[-1]
    mod = _ada_mod(c, ada_w, ada_b)

    def mods(i):
        return [mod[i, :, q * d:(q + 1) * d] for q in range(6)]

    sh1, sc1, gt1, sh2, sc2, gt2 = mods(0)
    (r, lw, k2, v, kk, bv, g, bonus), (w_o_b, ffn_gu_b, ffn_d_b, lru_in_b, lru_gates_b, lru_out_b) = _rwkv_pre(
        x, sh1, sc1, norm_g[0, 0], rwkv_mu[0], rwkv_w_rkv[0], rwkv_w1[0], rwkv_w2[0],
        rwkv_a1[0], rwkv_a2[0], rwkv_g1[0], rwkv_g2[0], rwkv_w0[0], rwkv_a0[0],
        rwkv_k_k[0], rwkv_k_a[0], rwkv_r_k[0], tm=min(256, t),
        later_weights=[rwkv_w_o[0], ffn_w_gu[0], ffn_w_d[0], lru_w_in[0], lru_w_gates[0], lru_w_out[0]])
    yg = _wkv_scan(r, lw, k2, v, kk, bv, g, bonus, rwkv_gn_w[0], rwkv_gn_b[0],
                   chunks_per_step=min(4, t // CHUNK))
    x, (moe_gu_b,) = _ffn_dense(yg, x, gt1, sh2, sc2, gt2, norm_g[0, 1], w_o_b, ffn_gu_b, ffn_d_b,
                                tm=min(256, t), later_weights=[moe_w_gu[0]])

    sh1, sc1, gt1, sh2, sc2, gt2 = mods(1)
    x, (moe_d_b,) = _lru_block(x, sh1, sc1, gt1, norm_g[1, 0], lru_in_b, lru_conv_w[0], lru_conv_b[0],
                               lru_gates_b, lru_b_gates[0], lru_lam[0], lru_out_b, tm=min(512, t),
                               later_weights=[moe_w_d[0]])

    x2d = x.reshape(n, d)
    tm_r = min(512, t)
    h2, eid, rank, prob, counts = _router(x2d, sh2, sc2, norm_g[1, 1], moe_w_router[0],
                                          moe_b_router[0], t // tm_r, tm_r)
    tm_g = min(256, t)
    n_tiles = (TOP_K * n) // tm_g + n_experts
    gid, tile_valid, src, dst, out_rows, spare_row = _moe_layout(eid, rank, counts, n_experts, tm_g, n_tiles)
    ys = _moe_ffn(gid, tile_valid, src, dst, h2, moe_gu_b, moe_d_b, out_rows, spare_row, tm_g)
    tc = min(512, t)
    out = _combine(ys, x2d, gt2, prob.T, final_g, t // tc, tc)
    return out.reshape(b, t, d)
```

```python
import functools

import jax
import jax.numpy as jnp
from jax import lax
from jax.experimental import pallas as pl
from jax.experimental.pallas import tpu as pltpu

F32 = jnp.float32
BF16 = jnp.bfloat16
I32 = jnp.int32

HEAD = 64
CHUNK = 64
GROUP = 256
HEADS_PER_GROUP = GROUP // HEAD
GN_EPS = 64e-5
NORM_EPS = 1e-6
LRU_C = 8.0
CONV_WIDTH = 4
LRU_BLOCK = 256
TOP_K = 2
LANES = 128
VMEM_LIMIT = 56 * 1024 * 1024


def _dot(a, b):
    return jnp.dot(a, b, preferred_element_type=F32)


def _dot_nt(a, b):
    return lax.dot_general(a, b, (((1,), (1,)), ((), ())), preferred_element_type=F32)


def _dot_tn(a, b):
    return lax.dot_general(a, b, (((0,), (0,)), ((), ())), preferred_element_type=F32)


def _softplus(u):
    return jnp.maximum(u, 0.0) + jnp.log1p(jnp.exp(-jnp.abs(u)))


def _modulate(x, ng, sh, sc):
    ms = jnp.mean(x * x, axis=-1, keepdims=True)
    return x * lax.rsqrt(ms + NORM_EPS) * ng * (1.0 + sc) + sh


def _split_bf16(x):
    hi = x.astype(BF16)
    lo = (x - hi.astype(F32)).astype(BF16)
    return hi, lo


def _cast_specs(weights, steps, step_index):
    ins, specs, out_shapes = [], [], []
    for w in weights:
        w2 = w.reshape(-1, w.shape[-1])
        rows = w2.shape[0]
        nblk = max(n for n in range(1, steps + 1) if rows % n == 0 and (rows // n) % 16 == 0)
        specs.append(pl.BlockSpec(
            (rows // nblk, w2.shape[1]),
            lambda *idx, nblk=nblk: (jnp.minimum(step_index(*idx), nblk - 1), 0)))
        ins.append(w2)
        out_shapes.append(jax.ShapeDtypeStruct(w2.shape, BF16))
    return ins, specs, out_shapes


def _with_casts(body, n_in, n_out, n_cast):
    def kernel(*refs):
        cast_in = refs[n_in:n_in + n_cast]
        outs = refs[n_in + n_cast:n_in + n_cast + n_out]
        cast_out = refs[n_in + n_cast + n_out:n_in + 2 * n_cast + n_out]
        for src, dst in zip(cast_in, cast_out):
            dst[...] = src[...].astype(BF16)
        body(*refs[:n_in], *outs, *refs[n_in + 2 * n_cast + n_out:])
    return kernel


def _shift_select(tm, shifts):
    rr = jnp.arange(tm)[:, None]
    cc = jnp.arange(tm)[None, :]
    return jnp.concatenate([(cc == rr - s) for s in shifts], axis=0).astype(BF16)


def _mod_kernel(c_ref, w_ref, b_ref, o_ref):
    c = c_ref[...]
    cond = c * jax.nn.sigmoid(c)
    o_ref[0] = _dot(cond.astype(BF16), w_ref[0].astype(BF16)) + b_ref[0]


def _ada_mod(c, ada_w, ada_b):
    depth, d, d6 = ada_w.shape
    b = c.shape[0]
    rows = 8
    c8 = jnp.pad(c, ((0, rows - b), (0, 0)))
    tn = 1024
    out = pl.pallas_call(
        _mod_kernel,
        grid=(depth, d6 // tn),
        in_specs=[
            pl.BlockSpec((rows, d), lambda i, j: (0, 0)),
            pl.BlockSpec((1, d, tn), lambda i, j: (i, 0, j)),
            pl.BlockSpec((1, 1, tn), lambda i, j: (i, 0, j)),
        ],
        out_specs=pl.BlockSpec((1, rows, tn), lambda i, j: (i, 0, j)),
        out_shape=jax.ShapeDtypeStruct((depth, rows, d6), F32),
        name="ada_mod",
    )(c8, ada_w, ada_b.reshape(depth, 1, d6))
    return out[:, :b]


def _rwkv_pre_kernel(x_ref, sh_ref, sc_ref, ng_ref, mu_ref, wrkv_ref, w1_ref, w2_ref,
                     a1_ref, a2_ref, g1_ref, g2_ref, vec_ref, seg_ref, segt_ref, sel_ref,
                     r_out, lw_out, k_out, v_out, kk_out, b_out, g_out, bonus_out,
                     hbuf):
    t = pl.program_id(1)
    tm = x_ref.shape[1]
    d = x_ref.shape[2]

    h = _modulate(x_ref[0], ng_ref[...], sh_ref[0], sc_ref[0])

    @pl.when(t == 0)
    def _():
        hbuf[...] = jnp.zeros((8, d), F32)

    hprev = _dot(sel_ref[...], h.astype(BF16))
    first = lax.broadcasted_iota(I32, (tm, d), 0) == 0
    hprev = jnp.where(first, hbuf[7:8, :], hprev)
    hbuf[...] = h[tm - 8:tm, :]
    xx = hprev - h

    def mix(p):
        return (h + xx * mu_ref[p:p + 1, :]).astype(BF16)

    r = _dot(mix(0), wrkv_ref[0])
    k = _dot(mix(1), wrkv_ref[1])
    v = _dot(mix(2), wrkv_ref[2])
    wl = _dot(jnp.tanh(_dot(mix(3), w1_ref[...])).astype(BF16), w2_ref[...])
    al = _dot(_dot(mix(4), a1_ref[...]).astype(BF16), a2_ref[...])
    g = _dot(jax.nn.sigmoid(_dot(mix(5), g1_ref[...])).astype(BF16), g2_ref[...])

    w0 = vec_ref[0:1, :]
    a0 = vec_ref[1:2, :]
    k_k = vec_ref[2:3, :]
    k_a = vec_ref[3:4, :]
    r_k = vec_ref[4:5, :]

    def headsum(z):
        s = _dot(z.astype(BF16), seg_ref[...])
        s_hi, s_lo = _split_bf16(s)
        return _dot(jnp.concatenate([s_hi, s_lo], axis=1), segt_ref[...])

    lw = -0.6065306597126334 * jax.nn.sigmoid(w0 + wl)
    a = jax.nn.sigmoid(a0 + al)
    kk = k * k_k
    kk = kk * lax.rsqrt(jnp.maximum(headsum(kk * kk), 1e-24))
    k2 = k * (1.0 + (a - 1.0) * k_a)
    bonus = headsum(r * k2 * r_k) * v

    r_out[0] = r
    lw_out[0] = lw
    k_out[0] = k2
    v_out[0] = v
    kk_out[0] = kk
    b_out[0] = kk * a
    g_out[0] = g
    bonus_out[0] = bonus


def _pad_to(x, axis, size):
    pad = [(0, 0)] * x.ndim
    pad[axis] = (0, size - x.shape[axis])
    return jnp.pad(x, pad)


def _rwkv_pre(x, sh, sc, ng, mu, w_rkv, w1, w2, a1, a2, g1, g2, w0, a0, k_k, k_a, r_k, tm,
              later_weights):
    b, t, d = x.shape
    nh = d // HEAD
    lw_pad = LANES * pl.cdiv(w1.shape[1], LANES)
    la_pad = LANES * pl.cdiv(a1.shape[1], LANES)
    lg_pad = LANES * pl.cdiv(g1.shape[1], LANES)
    vecs = _pad_to(jnp.stack([w0, a0, k_k, k_a, r_k.reshape(d)]), 0, 8)
    head_of_lane = jnp.arange(d) // HEAD
    seg = (head_of_lane[:, None] == jnp.arange(LANES)[None, :]).astype(BF16)
    segt = jnp.concatenate([seg.T, seg.T], axis=0)
    del nh
    full = lambda *shape: pl.BlockSpec(shape, lambda bi, ti: (0,) * len(shape))
    tok = pl.BlockSpec((1, tm, d), lambda bi, ti: (bi, ti, 0))
    vec = pl.BlockSpec((1, 1, d), lambda bi, ti: (bi, 0, 0))
    nt = t // tm
    cast_in, cast_specs, cast_shapes = _cast_specs(later_weights, b * nt, lambda bi, ti: bi * nt + ti)
    in_specs = [tok, vec, vec, full(1, d), full(8, d), full(3, d, d),
                full(d, lw_pad), full(lw_pad, d), full(d, la_pad), full(la_pad, d),
                full(d, lg_pad), full(lg_pad, d), full(8, d), full(d, LANES), full(2 * LANES, d),
                full(tm, tm)]
    outs = pl.pallas_call(
        _with_casts(_rwkv_pre_kernel, len(in_specs), 8, len(cast_in)),
        grid=(b, nt),
        in_specs=in_specs + cast_specs,
        out_specs=[tok] * 8 + cast_specs,
        out_shape=[jax.ShapeDtypeStruct((b, t, d), F32)] * 8 + cast_shapes,
        scratch_shapes=[pltpu.VMEM((8, d), F32)],
        compiler_params=pltpu.CompilerParams(
            dimension_semantics=("arbitrary", "arbitrary"), vmem_limit_bytes=VMEM_LIMIT),
        name="rwkv_pre",
    )(x, sh[:, None, :], sc[:, None, :], ng[None, :], _pad_to(mu, 0, 8), w_rkv.astype(BF16),
      _pad_to(w1, 1, lw_pad).astype(BF16), _pad_to(w2, 0, lw_pad).astype(BF16),
      _pad_to(a1, 1, la_pad).astype(BF16), _pad_to(a2, 0, la_pad).astype(BF16),
      _pad_to(g1, 1, lg_pad).astype(BF16), _pad_to(g2, 0, lg_pad).astype(BF16),
      vecs, seg, segt, _shift_select(tm, [1]), *cast_in)
    casts = [o.reshape(w.shape) for o, w in zip(outs[8:], later_weights)]
    return outs[:8], casts


def _wkv_scan_kernel(r_ref, lw_ref, k_ref, v_ref, kk_ref, b_ref, g_ref, bonus_ref,
                     gnw_ref, gnb_ref, o_ref, h_scr, *, chunks_per_step):
    L = CHUNK
    W = GROUP
    ngroups = r_ref.shape[2] // W

    @pl.when(pl.program_id(1) == 0)
    def _():
        h_scr[...] = jnp.zeros_like(h_scr)

    row = lax.broadcasted_iota(I32, (L, W), 0)
    lane = lax.broadcasted_iota(I32, (L, W), 1)
    sidx = lane & (L - 1)
    lane_head = lane >> 6
    strict = sidx < row
    incl = sidx <= row
    eye = (sidx == row).astype(F32)
    same16 = (row >> 4) == (sidx >> 4)
    same32 = (row >> 5) == (sidx >> 5)
    m16 = strict & same16
    m32 = strict & same32 & jnp.logical_not(same16)
    m64 = strict & jnp.logical_not(same32)
    rb = lax.broadcasted_iota(I32, (W, W), 0)
    cb = lax.broadcasted_iota(I32, (W, W), 1)
    bmask = (rb >> 6) == (cb >> 6)
    diag = rb == cb
    ones_bd = bmask.astype(BF16)
    tri_r = lax.broadcasted_iota(I32, (L, 3 * L), 0)
    tri_c = lax.broadcasted_iota(I32, (L, 3 * L), 1)
    tri3 = ((tri_c & (L - 1)) <= tri_r).astype(BF16)

    def bd(y):
        yt = jnp.concatenate([y] * HEADS_PER_GROUP, axis=0)
        return jnp.where(bmask, yt, 0.0).astype(BF16)

    def hmm(x, ybd):
        return _dot(x.astype(BF16), ybd)

    streams = [(q, j) for j in range(chunks_per_step) for q in range(ngroups)]
    S = range(len(streams))

    def ld(ref, s):
        q, j = streams[s]
        return ref[0, j * L:(j + 1) * L, q * W:(q + 1) * W]

    r = [ld(r_ref, s) for s in S]
    lw = [ld(lw_ref, s) for s in S]
    k = [ld(k_ref, s) for s in S]
    v = [ld(v_ref, s) for s in S]
    kk = [ld(kk_ref, s) for s in S]
    bv = [ld(b_ref, s) for s in S]

    def cumlog(x):
        hi = x.astype(BF16)
        rem = x - hi.astype(F32)
        mid = rem.astype(BF16)
        lo = (rem - mid.astype(F32)).astype(BF16)
        return _dot(tri3, jnp.concatenate([hi, mid, lo], axis=0))

    cl = [cumlog(lw[s]) for s in S]
    cl_last = [cl[s][L - 1:L, :] for s in S]
    e_pos = [jnp.exp(cl[s]) for s in S]
    e_neg = [jnp.exp(-cl[s]) for s in S]
    e_end = [jnp.exp(cl_last[s] - cl[s]) for s in S]
    rt = [r[s] * e_pos[s] for s in S]
    at = [-kk[s] * jnp.exp(cl[s] - lw[s]) for s in S]
    bt = [bv[s] * e_neg[s] for s in S]
    kt = [k[s] * e_neg[s] for s in S]

    def gram(s):
        x = jnp.concatenate([at[s], rt[s]], axis=0).astype(BF16)
        ys = [jnp.where(lane_head == hh, bt[s], 0.0) for hh in range(HEADS_PER_GROUP)]
        ys += [jnp.where(lane_head == hh, kt[s], 0.0) for hh in range(HEADS_PER_GROUP)]
        return _dot_nt(x, jnp.concatenate(ys, axis=0).astype(BF16))

    gm = [gram(s) for s in S]
    a_ab = [jnp.where(strict, gm[s][:L, :W], 0.0) for s in S]
    a_ak = [jnp.where(strict, gm[s][:L, W:], 0.0) for s in S]
    a_rb = [jnp.where(incl, gm[s][L:, :W], 0.0) for s in S]
    a_rk = [jnp.where(incl, gm[s][L:, W:], 0.0) for s in S]

    a0 = [jnp.where(m16, a_ab[s], 0.0) for s in S]
    pw = [hmm(a0[s], bd(a0[s])) for s in S]
    tinv = [eye + a0[s] for s in S]
    for _ in range(2):
        ts = [_dot(jnp.concatenate([tinv[s], pw[s]], axis=0).astype(BF16), bd(pw[s])) for s in S]
        tinv = [tinv[s] + ts[s][:L] for s in S]
        pw = [ts[s][L:] for s in S]
    tinv = [tinv[s] + hmm(tinv[s], bd(pw[s])) for s in S]
    for msk in (m32, m64):
        inner = [hmm(jnp.where(msk, a_ab[s], 0.0), bd(tinv[s])) for s in S]
        tinv = [tinv[s] + hmm(tinv[s], bd(inner[s])) for s in S]

    vbd = [bd(v[s]) for s in S]
    avs = [_dot(jnp.concatenate([a_ak[s], a_rk[s]], axis=0).astype(BF16), vbd[s]) for s in S]
    av = [avs[s][:L] for s in S]
    tx = [_dot(tinv[s].astype(BF16), jnp.concatenate([bd(at[s]), bd(av[s])], axis=1)) for s in S]
    ahat = [tx[s][:, :W] for s in S]
    vp = [tx[s][:, W:] for s in S]
    ox = [_dot(a_rb[s].astype(BF16), jnp.concatenate([bd(ahat[s]), bd(vp[s])], axis=1)) for s in S]
    rhat = [rt[s] + ox[s][:, :W] for s in S]
    o_intra = [ox[s][:, W:] + avs[s][L:] for s in S]

    def state_terms(s):
        z = jnp.concatenate([bv[s] * e_end[s], k[s] * e_end[s]], axis=0).astype(BF16)
        wm = jnp.concatenate(
            [jnp.concatenate([ahat[s], vp[s]], axis=1),
             jnp.concatenate([jnp.zeros((L, W), F32), v[s]], axis=1)], axis=0).astype(BF16)
        mn = _dot_tn(z, wm)
        m_mat = jnp.where(bmask, mn[:, :W], 0.0) + jnp.where(diag, jnp.exp(cl_last[s]), 0.0)
        return m_mat, jnp.where(bmask, mn[:, W:], 0.0)

    mn = [state_terms(s) for s in S]

    o = [None] * len(streams)
    hq = [h_scr[q] for q in range(ngroups)]
    for j in range(chunks_per_step):
        for q in range(ngroups):
            s = streams.index((q, j))
            m_hi, m_lo = _split_bf16(mn[s][0])
            lhs = jnp.concatenate([m_hi, m_lo, rhat[s].astype(BF16)], axis=0)
            res = _dot(lhs, hq[q].astype(BF16))
            o[s] = res[2 * W:] + o_intra[s]
            hq[q] = res[:W] + res[W:2 * W] + mn[s][1]
    for q in range(ngroups):
        h_scr[q] = hq[q]

    def headmean(zs):
        parts = []
        for z in zs:
            parts += list(_split_bf16(z))
        red = _dot(jnp.concatenate(parts, axis=0), ones_bd) * (1.0 / HEAD)
        return [red[2 * L * s:2 * L * s + L] + red[2 * L * s + L:2 * L * (s + 1)] for s in S]

    mean = headmean(o)
    dlt = [o[s] - mean[s] for s in S]
    sq = jnp.concatenate([(dlt[s] * dlt[s]).astype(BF16) for s in S], axis=0)
    var_all = _dot(sq, ones_bd) * (1.0 / HEAD)
    var = [var_all[L * s:L * (s + 1)] for s in S]
    for s in S:
        q, j = streams[s]
        gsl = slice(q * W, (q + 1) * W)
        rsl = slice(j * L, (j + 1) * L)
        yn = dlt[s] * lax.rsqrt(var[s] + GN_EPS) * gnw_ref[:, gsl] + gnb_ref[:, gsl]
        o_ref[0, rsl, gsl] = ((yn + bonus_ref[0, rsl, gsl]) * g_ref[0, rsl, gsl]).astype(o_ref.dtype)


def _wkv_scan(r, lw, k, v, kk, bv, g, bonus, gn_w, gn_b, chunks_per_step):
    b, t, d = r.shape
    lb = CHUNK * chunks_per_step
    tok = pl.BlockSpec((1, lb, d), lambda bi, ci: (bi, ci, 0))
    vec = pl.BlockSpec((1, d), lambda bi, ci: (0, 0))
    return pl.pallas_call(
        functools.partial(_wkv_scan_kernel, chunks_per_step=chunks_per_step),
        grid=(b, t // lb),
        in_specs=[tok] * 8 + [vec, vec],
        out_specs=tok,
        out_shape=jax.ShapeDtypeStruct((b, t, d), BF16),
        scratch_shapes=[pltpu.VMEM((d // GROUP, GROUP, GROUP), F32)],
        compiler_params=pltpu.CompilerParams(
            dimension_semantics=("arbitrary", "arbitrary"), vmem_limit_bytes=VMEM_LIMIT),
        name="wkv_scan",
    )(r, lw, k, v, kk, bv, g, bonus, gn_w[None, :], gn_b[None, :])


def _swiglu(h, wg, wu, wd):
    g = _dot(h, wg)
    u = _dot(h, wu)
    return _dot((g * jax.nn.sigmoid(g) * u).astype(BF16), wd)


def _ffn_dense_kernel(a_ref, x_ref, gt1_ref, sh_ref, sc_ref, gt2_ref, ng_ref, wo_ref,
                      wg_ref, wu_ref, wd_ref, o_ref):
    x1 = x_ref[0] + (1.0 + gt1_ref[0]) * _dot(a_ref[0], wo_ref[...])
    h = _modulate(x1, ng_ref[...], sh_ref[0], sc_ref[0]).astype(BF16)
    o_ref[0] = x1 + (1.0 + gt2_ref[0]) * _swiglu(h, wg_ref[...], wu_ref[...], wd_ref[...])


def _resident(shape, index_map):
    return pl.BlockSpec(shape, index_map, pipeline_mode=pl.Buffered(1))


def _ffn_dense(a, x, gt1, sh, sc, gt2, ng, w_o, w_gu, w_d, tm, later_weights):
    b, t, d = x.shape
    f = w_d.shape[0]
    nt = t // tm
    tok = pl.BlockSpec((1, tm, d), lambda bi, ti: (bi, ti, 0))
    vec = pl.BlockSpec((1, 1, d), lambda bi, ti: (bi, 0, 0))
    cast_in, cast_specs, cast_shapes = _cast_specs(later_weights, b * nt, lambda bi, ti: bi * nt + ti)
    in_specs = [tok, tok, vec, vec, vec, vec, pl.BlockSpec((1, d), lambda bi, ti: (0, 0)),
                _resident((d, d), lambda bi, ti: (0, 0)),
                _resident((d, f), lambda bi, ti: (0, 0)),
                _resident((d, f), lambda bi, ti: (0, 1)),
                _resident((f, d), lambda bi, ti: (0, 0))]
    outs = pl.pallas_call(
        _with_casts(_ffn_dense_kernel, len(in_specs), 1, len(cast_in)),
        grid=(b, nt),
        in_specs=in_specs + cast_specs,
        out_specs=[tok] + cast_specs,
        out_shape=[jax.ShapeDtypeStruct((b, t, d), F32)] + cast_shapes,
        compiler_params=pltpu.CompilerParams(
            dimension_semantics=("arbitrary", "arbitrary"), vmem_limit_bytes=VMEM_LIMIT),
        name="ffn_dense",
    )(a, x, gt1[:, None, :], sh[:, None, :], sc[:, None, :], gt2[:, None, :], ng[None, :],
      w_o, w_gu, w_gu, w_d, *cast_in)
    return outs[0], [o.reshape(w.shape) for o, w in zip(outs[1:], later_weights)]


def _lru_kernel(x_ref, sh_ref, sc_ref, gt_ref, ng_ref, win_ref, cw_ref, cb_ref, wg_ref, bg_ref,
                lam_ref, wout_ref, o_ref, xbuf, abuf, bbuf, hbuf, ga, gb, carry):
    t = pl.program_id(1)
    tm = x_ref.shape[1]
    w = win_ref.shape[1] // 2
    nblk = w // LRU_BLOCK
    nslab = w // LANES
    ng = tm // 8

    @pl.when(t == 0)
    def _():
        xbuf[0:8, :] = jnp.zeros((8, w), F32)
        carry[...] = jnp.zeros_like(carry)
        ga[:, :, 0:ng, :] = jnp.ones((2, nslab, ng, LANES), F32)
        gb[:, :, 0:ng, :] = jnp.zeros((2, nslab, ng, LANES), F32)

    x = x_ref[0]
    h = _modulate(x, ng_ref[...], sh_ref[0], sc_ref[0]).astype(BF16)
    xg = _dot(h, win_ref[...])
    xb = xg[:, :w]
    gx = xg[:, w:]
    gate = 0.5 * gx * (1.0 + jnp.tanh(0.7978845608028654 * (gx + 0.044715 * gx * gx * gx)))

    xbuf[8:8 + tm, :] = xb
    conv = cb_ref[...] + cw_ref[CONV_WIDTH - 1:CONV_WIDTH, :] * xb
    for jj in range(CONV_WIDTH - 1):
        shift = CONV_WIDTH - 1 - jj
        conv = conv + cw_ref[jj:jj + 1, :] * xbuf[8 - shift:8 - shift + tm, :]
    xbuf[0:8, :] = xb[tm - 8:tm, :]

    conv_b = conv.astype(BF16)
    rs, is_ = [], []
    for n in range(nblk):
        gts = _dot(conv_b[:, n * LRU_BLOCK:(n + 1) * LRU_BLOCK], wg_ref[n]) + bg_ref[n]
        gts = jax.nn.sigmoid(gts)
        rs.append(gts[:, :LRU_BLOCK])
        is_.append(gts[:, LRU_BLOCK:])
    r_t = jnp.concatenate(rs, axis=1)
    i_t = jnp.concatenate(is_, axis=1)

    log_a = -LRU_C * r_t * _softplus(-lam_ref[...])
    a_t = jnp.exp(log_a)
    b_t = jnp.sqrt(-jnp.tanh(log_a) * (a_t * a_t + 1.0)) * (i_t * conv)

    for j in range(nslab):
        abuf[j] = a_t[:, j * LANES:(j + 1) * LANES]
        bbuf[j] = b_t[:, j * LANES:(j + 1) * LANES]
    hs_slabs = []
    for j in range(nslab):
        a_loc = [abuf[j, pl.ds(0, ng, stride=8), :]]
        b_loc = [bbuf[j, pl.ds(0, ng, stride=8), :]]
        for s in range(1, 8):
            a_s = abuf[j, pl.ds(s, ng, stride=8), :]
            b_s = bbuf[j, pl.ds(s, ng, stride=8), :]
            b_loc.append(a_s * b_loc[-1] + b_s)
            a_loc.append(a_s * a_loc[-1])
        ga[0, j, ng:2 * ng, :] = a_loc[-1]
        gb[0, j, ng:2 * ng, :] = b_loc[-1]
        step = 1
        src = 0
        while step < ng:
            a_cur = ga[src, j, ng:2 * ng, :]
            b_cur = gb[src, j, ng:2 * ng, :]
            a_sh = ga[src, j, ng - step:2 * ng - step, :]
            b_sh = gb[src, j, ng - step:2 * ng - step, :]
            ga[1 - src, j, ng:2 * ng, :] = a_cur * a_sh
            gb[1 - src, j, ng:2 * ng, :] = a_cur * b_sh + b_cur
            src = 1 - src
            step *= 2
        c_in = carry[0:1, j * LANES:(j + 1) * LANES]
        h_in = gb[src, j, ng - 1:2 * ng - 1, :] + ga[src, j, ng - 1:2 * ng - 1, :] * c_in
        for s in range(8):
            h_s = b_loc[s] + a_loc[s] * h_in
            hbuf[j, pl.ds(s, ng, stride=8), :] = h_s
        carry[:, j * LANES:(j + 1) * LANES] = jnp.broadcast_to(h_s[ng - 1:ng, :], (8, LANES))
        hs_slabs.append(hbuf[j])
    hs = jnp.concatenate(hs_slabs, axis=1)

    y = _dot((hs * gate).astype(BF16), wout_ref[...])
    o_ref[0] = x + (1.0 + gt_ref[0]) * y


def _lru_block(x, sh, sc, gt, ng, w_in, conv_w, conv_b, w_gates, b_gates, lam, w_out, tm,
               later_weights):
    b, t, d = x.shape
    w = w_out.shape[0]
    nblk = w // LRU_BLOCK
    nt = t // tm
    full = lambda *shape: pl.BlockSpec(shape, lambda bi, ti: (0,) * len(shape))
    tok = pl.BlockSpec((1, tm, d), lambda bi, ti: (bi, ti, 0))
    vec = pl.BlockSpec((1, 1, d), lambda bi, ti: (bi, 0, 0))
    cast_in, cast_specs, cast_shapes = _cast_specs(later_weights, b * nt, lambda bi, ti: bi * nt + ti)
    in_specs = [tok, vec, vec, vec, full(1, d), full(d, 2 * w), full(CONV_WIDTH, w), full(1, w),
                full(nblk, LRU_BLOCK, 2 * LRU_BLOCK), full(nblk, 1, 2 * LRU_BLOCK),
                full(1, w), full(w, d)]
    outs = pl.pallas_call(
        _with_casts(_lru_kernel, len(in_specs), 1, len(cast_in)),
        grid=(b, nt),
        in_specs=in_specs + cast_specs,
        out_specs=[tok] + cast_specs,
        out_shape=[jax.ShapeDtypeStruct((b, t, d), F32)] + cast_shapes,
        scratch_shapes=[pltpu.VMEM((tm + 8, w), F32)]
        + [pltpu.VMEM((w // LANES, tm, LANES), F32)] * 3
        + [pltpu.VMEM((2, w // LANES, tm // 4, LANES), F32)] * 2
        + [pltpu.VMEM((8, w), F32)],
        compiler_params=pltpu.CompilerParams(
            dimension_semantics=("arbitrary", "arbitrary"), vmem_limit_bytes=VMEM_LIMIT),
        name="rglru_block",
    )(x, sh[:, None, :], sc[:, None, :], gt[:, None, :], ng[None, :], w_in,
      conv_w, conv_b[None, :], w_gates, b_gates[:, None, :], lam[None, :], w_out, *cast_in)
    return outs[0], [o.reshape(wt.shape) for o, wt in zip(outs[1:], later_weights)]


def _router_kernel(x_ref, sh_ref, sc_ref, ng_ref, wr_ref, br_ref, tri_ref,
                   h_out, eid_out, rank_out, prob_out, cnt_out, cnt_scr):
    i = pl.program_id(0)

    @pl.when(i == 0)
    def _():
        cnt_scr[...] = jnp.zeros_like(cnt_scr)

    h = _modulate(x_ref[...], ng_ref[...], sh_ref[0], sc_ref[0])
    h_out[...] = h
    h_hi, h_lo = _split_bf16(h)
    w_hi, w_lo = _split_bf16(wr_ref[...])
    logits = _dot_nt(w_hi, h_hi) + _dot_nt(w_hi, h_lo) + _dot_nt(w_lo, h_hi) + br_ref[:, 0:1]

    n_rows = logits.shape[0]
    eidx = lax.broadcasted_iota(I32, logits.shape, 0)
    m1 = jnp.max(logits, axis=0, keepdims=True)
    i1 = jnp.min(jnp.where(logits == m1, eidx, n_rows), axis=0, keepdims=True)
    l2 = jnp.where(eidx == i1, -jnp.inf, logits)
    m2 = jnp.max(l2, axis=0, keepdims=True)
    i2 = jnp.min(jnp.where(l2 == m2, eidx, n_rows), axis=0, keepdims=True)
    e = jnp.exp(m2 - m1)
    p1 = 1.0 / (1.0 + e)
    p2 = e / (1.0 + e)

    oh1 = (eidx == i1).astype(F32)
    oh2 = (eidx == i2).astype(F32)
    oh = oh1 + oh2
    before = _dot(oh.astype(BF16), tri_ref[...]) + cnt_scr[:, 0:1]
    rank1 = jnp.sum(before * oh1, axis=0, keepdims=True)
    rank2 = jnp.sum(before * oh2, axis=0, keepdims=True)
    cnt_scr[...] = cnt_scr[...] + jnp.sum(oh, axis=1, keepdims=True)

    eid_out[0:1, :] = i1
    eid_out[1:2, :] = i2
    rank_out[0:1, :] = rank1.astype(I32)
    rank_out[1:2, :] = rank2.astype(I32)
    prob_out[0:1, :] = p1
    prob_out[1:2, :] = p2
    cnt_out[...] = cnt_scr[...].astype(I32)


def _router(x2d, sh, sc, ng, w_router, b_router, tiles_per_batch, tm):
    n, d = x2d.shape
    e = w_router.shape[1]
    ep = 16 * pl.cdiv(e, 16)
    wr = _pad_to(w_router.T, 0, ep)
    br = jnp.broadcast_to(jnp.concatenate([b_router, jnp.full((ep - e,), -1e30, F32)])[:, None],
                          (ep, LANES))
    tri = (jnp.arange(tm)[:, None] < jnp.arange(tm)[None, :]).astype(BF16)
    tok = pl.BlockSpec((tm, d), lambda i: (i, 0))
    vec = pl.BlockSpec((1, 1, d), lambda i: (i // tiles_per_batch, 0, 0))
    two = pl.BlockSpec((TOP_K, tm), lambda i: (0, i))
    const = lambda *shape: pl.BlockSpec(shape, lambda i: (0,) * len(shape))
    return pl.pallas_call(
        _router_kernel,
        grid=(n // tm,),
        in_specs=[tok, vec, vec, const(1, d), const(ep, d), const(ep, LANES), const(tm, tm)],
        out_specs=[tok, two, two, two, const(ep, LANES)],
        out_shape=[jax.ShapeDtypeStruct((n, d), F32), jax.ShapeDtypeStruct((TOP_K, n), I32),
                   jax.ShapeDtypeStruct((TOP_K, n), I32), jax.ShapeDtypeStruct((TOP_K, n), F32),
                   jax.ShapeDtypeStruct((ep, LANES), I32)],
        scratch_shapes=[pltpu.VMEM((ep, LANES), F32)],
        compiler_params=pltpu.CompilerParams(
            dimension_semantics=("arbitrary",), vmem_limit_bytes=VMEM_LIMIT),
        name="moe_router",
    )(x2d, sh[:, None, :], sc[:, None, :], ng[None, :], wr, br, tri)


def _moe_ffn_kernel(gid_ref, valid_ref, src0_ref, src_next_ref, dst_prev_ref, h_hbm,
                    wg_ref, wu_ref, wd_ref, ys_hbm, x0, x1, y0, y1, gsem, ssem,
                    *, n_tiles, tm, spare_row):
    del gid_ref
    i = pl.program_id(0)
    xbufs = (x0, x1)
    ybufs = (y0, y1)
    valid = valid_ref[jnp.minimum(i, n_tiles - 1)] > 0

    def gather(src_ref, xbuf):
        for t in range(tm):
            pltpu.make_async_copy(h_hbm.at[pl.ds(src_ref[0, 0, t], 1)], xbuf.at[pl.ds(t, 1)],
                                  gsem).start()

    def scatter(ybuf):
        for t in range(tm):
            pltpu.make_async_copy(ybuf.at[pl.ds(t, 1)],
                                  ys_hbm.at[pl.ds(dst_prev_ref[0, 0, t], 1)], ssem).start()

    def wait_gather(xbuf):
        pltpu.make_async_copy(h_hbm.at[pl.ds(0, tm)], xbuf, gsem).wait()

    def wait_scatter(ybuf):
        pltpu.make_async_copy(ybuf, ys_hbm.at[pl.ds(0, tm)], ssem).wait()

    @pl.when(i == 0)
    def _():
        y0[...] = jnp.zeros_like(y0)
        y1[...] = jnp.zeros_like(y1)
        pltpu.make_async_copy(y0, ys_hbm.at[pl.ds(spare_row, tm)], ssem).start()
        gather(src0_ref, x0)

    for par in range(2):
        for compute in (True, False):
            @pl.when((i < n_tiles) & (i % 2 == par) & (valid if compute else jnp.logical_not(valid)))
            def _(par=par, compute=compute):
                xa, xb, ya, yb = xbufs[par], xbufs[1 - par], ybufs[par], ybufs[1 - par]
                wait_gather(xa)
                wait_scatter(ya)
                gather(src_next_ref, xb)
                scatter(yb)
                if compute:
                    ya[...] = _swiglu(xa[...].astype(BF16), wg_ref[0], wu_ref[0], wd_ref[0])

    @pl.when(i == n_tiles)
    def _():
        par = n_tiles % 2
        wait_gather(xbufs[par])
        wait_scatter(ybufs[par])
        scatter(ybufs[1 - par])
        wait_scatter(ybufs[1 - par])


def _moe_ffn(gid, valid, src, dst, h2d, w_gu, w_d, out_rows, spare_row, tm):
    n_tiles = gid.shape[0]
    _, d = h2d.shape
    f = w_d.shape[1]
    idx = lambda fn: pl.BlockSpec((1, 1, tm), fn, memory_space=pltpu.SMEM)
    last = n_tiles - 1
    grid_spec = pltpu.PrefetchScalarGridSpec(
        num_scalar_prefetch=2,
        grid=(n_tiles + 1,),
        in_specs=[idx(lambda i, gid, vld: (0, 0, 0)),
                  idx(lambda i, gid, vld: (jnp.minimum(i + 1, n_tiles), 0, 0)),
                  idx(lambda i, gid, vld: (i, 0, 0)),
                  pl.BlockSpec(memory_space=pl.ANY),
                  pl.BlockSpec((1, d, f), lambda i, gid, vld: (gid[jnp.minimum(i, last)], 0, 0)),
                  _resident((1, d, f), lambda i, gid, vld: (gid[jnp.minimum(i, last)], 0, 1)),
                  _resident((1, f, d), lambda i, gid, vld: (gid[jnp.minimum(i, last)], 0, 0))],
        out_specs=pl.BlockSpec(memory_space=pl.ANY),
        scratch_shapes=[pltpu.VMEM((tm, d), F32)] * 4 + [pltpu.SemaphoreType.DMA(())] * 2,
    )
    return pl.pallas_call(
        functools.partial(_moe_ffn_kernel, n_tiles=n_tiles, tm=tm, spare_row=spare_row),
        grid_spec=grid_spec,
        out_shape=jax.ShapeDtypeStruct((out_rows, d), F32),
        compiler_params=pltpu.CompilerParams(
            dimension_semantics=("arbitrary",), vmem_limit_bytes=VMEM_LIMIT),
        name="moe_ffn",
    )(gid, valid, src, src, dst, h2d, w_gu, w_gu, w_d)


def _combine_kernel(y0_ref, y1_ref, x_ref, gt_ref, prob_ref, fg_ref, o_ref):
    y = prob_ref[:, 0:1] * y0_ref[...] + prob_ref[:, 1:2] * y1_ref[...]
    x = x_ref[...] + (1.0 + gt_ref[0]) * y
    ms = jnp.mean(x * x, axis=-1, keepdims=True)
    o_ref[...] = x * lax.rsqrt(ms + NORM_EPS) * fg_ref[...]


def _combine(ys, x2d, gt, prob, final_g, tiles_per_batch, tc):
    n, d = x2d.shape
    nt = n // tc
    tok = pl.BlockSpec((tc, d), lambda i: (i, 0))
    return pl.pallas_call(
        _combine_kernel,
        grid=(nt,),
        in_specs=[tok, pl.BlockSpec((tc, d), lambda i: (nt + i, 0)), tok,
                  pl.BlockSpec((1, 1, d), lambda i: (i // tiles_per_batch, 0, 0)),
                  pl.BlockSpec((tc, TOP_K), lambda i: (i, 0)),
                  pl.BlockSpec((1, d), lambda i: (0, 0))],
        out_specs=tok,
        out_shape=jax.ShapeDtypeStruct((n, d), F32),
        compiler_params=pltpu.CompilerParams(
            dimension_semantics=("arbitrary",), vmem_limit_bytes=VMEM_LIMIT),
        name="moe_combine",
    )(ys, ys, x2d, gt[:, None, :], prob, final_g[None, :])


def _moe_layout(eid, rank, counts, n_experts, tm, n_tiles):
    n = eid.shape[1]
    rows = n_tiles * tm
    cnt = counts[:n_experts, 0]
    tiles = (cnt + tm - 1) // tm
    tile_end = jnp.cumsum(tiles)
    offsets = (tile_end - tiles) * tm
    first_rank = jnp.cumsum(cnt) - cnt

    def per_expert(table, idx):
        out = jnp.zeros(idx.shape, table.dtype)
        for e in range(n_experts):
            out = jnp.where(idx == e, table[e], out)
        return out

    pos = (per_expert(offsets, eid) + rank).T.reshape(-1)
    order = jnp.argsort(pos).astype(I32)
    tile_ids = jnp.arange(n_tiles, dtype=I32)
    last_id = jnp.minimum(tile_ids, tile_end[-1] - 1)
    gid = jnp.sum((last_id[:, None] >= tile_end[None, :]).astype(I32), axis=1)
    p = jnp.arange(rows, dtype=I32)
    e_of_p = jnp.repeat(gid, tm)
    within = p - per_expert(offsets, e_of_p)
    valid = within < per_expert(cnt, e_of_p)
    flat = order[jnp.clip(per_expert(first_rank, e_of_p) + within, 0, TOP_K * n - 1)]
    tok = flat // TOP_K
    src = jnp.where(valid, tok, 0)
    pad_id = jnp.cumsum(jnp.logical_not(valid).astype(I32)) - 1
    dst = jnp.where(valid, (flat % TOP_K) * n + tok, TOP_K * n + pad_id)
    spare = rows
    src = jnp.concatenate([src, jnp.zeros((tm,), I32)]).reshape(n_tiles + 1, 1, tm)
    dst = jnp.concatenate([spare + jnp.arange(tm, dtype=I32), dst]).reshape(n_tiles + 1, 1, tm)
    tile_valid = (tile_ids < tile_end[-1]).astype(I32)
    return gid, tile_valid, src.astype(I32), dst.astype(I32), spare + 2 * tm, spare + tm


def kernel(x, c, ada_w, ada_b, norm_g, final_g, rwkv_mu, rwkv_w_rkv, rwkv_w_o, rwkv_w0, rwkv_w1, rwkv_w2, rwkv_a0, rwkv_a1, rwkv_a2, rwkv_g1, rwkv_g2, rwkv_k_k, rwkv_k_a, rwkv_r_k, rwkv_gn_w, rwkv_gn_b, lru_w_in, lru_conv_w, lru_conv_b, lru_w_gates, lru_b_gates, lru_lam, lru_w_out, ffn_w_gu, ffn_w_d, moe_w_router, moe_b_router, moe_w_gu, moe_w_d):
    b, t, d = x.shape
    n = b * t
    n_experts = moe_w_router.shape[-1]
    mod = _ada_mod(c, ada_w, ada_b)

    def mods(i):
        return [mod[i, :, q * d:(q + 1) * d] for q in range(6)]

    sh1, sc1, gt1, sh2, sc2, gt2 = mods(0)
    (r, lw, k2, v, kk, bv, g, bonus), (w_o_b, ffn_gu_b, ffn_d_b, lru_in_b, lru_gates_b, lru_out_b) = _rwkv_pre(
        x, sh1, sc1, norm_g[0, 0], rwkv_mu[0], rwkv_w_rkv[0], rwkv_w1[0], rwkv_w2[0],
        rwkv_a1[0], rwkv_a2[0], rwkv_g1[0], rwkv_g2[0], rwkv_w0[0], rwkv_a0[0],
        rwkv_k_k[0], rwkv_k_a[0], rwkv_r_k[0], tm=min(256, t),
        later_weights=[rwkv_w_o[0], ffn_w_gu[0], ffn_w_d[0], lru_w_in[0], lru_w_gates[0], lru_w_out[0]])
    yg = _wkv_scan(r, lw, k2, v, kk, bv, g, bonus, rwkv_gn_w[0], rwkv_gn_b[0],
                   chunks_per_step=min(4, t // CHUNK))
    x, (moe_gu_b,) = _ffn_dense(yg, x, gt1, sh2, sc2, gt2, norm_g[0, 1], w_o_b, ffn_gu_b, ffn_d_b,
                                tm=min(256, t), later_weights=[moe_w_gu[0]])

    sh1, sc1, gt1, sh2, sc2, gt2 = mods(1)
    x, (moe_d_b,) = _lru_block(x, sh1, sc1, gt1, norm_g[1, 0], lru_in_b, lru_conv_w[0], lru_conv_b[0],
                               lru_gates_b, lru_b_gates[0], lru_lam[0], lru_out_b, tm=min(512, t),
                               later_weights=[moe_w_d[0]])

    x2d = x.reshape(n, d)
    tm_r = min(512, t)
    h2, eid, rank, prob, counts = _router(x2d, sh2, sc2, norm_g[1, 1], moe_w_router[0],
                                          moe_b_router[0], t // tm_r, tm_r)
    tm_g = min(256, t)
    n_tiles = (TOP_K * n) // tm_g + n_experts
    gid, tile_valid, src, dst, out_rows, spare_row = _moe_layout(eid, rank, counts, n_experts, tm_g, n_tiles)
    ys = _moe_ffn(gid, tile_valid, src, dst, h2, moe_gu_b, moe_d_b, out_rows, spare_row, tm_g)
    tc = min(512, t)
    out = _combine(ys, x2d, gt2, prob.T, final_g, t // tc, tc)
    return out.reshape(b, t, d)
```

```python
import functools

import jax
import jax.numpy as jnp
from jax import lax
from jax.experimental import pallas as pl
from jax.experimental.pallas import tpu as pltpu

F32 = jnp.float32
BF16 = jnp.bfloat16
I32 = jnp.int32

HEAD = 64
CHUNK = 64
GROUP = 256
HEADS_PER_GROUP = GROUP // HEAD
GN_EPS = 64e-5
NORM_EPS = 1e-6
LRU_C = 8.0
CONV_WIDTH = 4
LRU_BLOCK = 256
TOP_K = 2
LANES = 128
VMEM_LIMIT = 56 * 1024 * 1024


def _dot(a, b):
    return jnp.dot(a, b, preferred_element_type=F32)


def _dot_nt(a, b):
    return lax.dot_general(a, b, (((1,), (1,)), ((), ())), preferred_element_type=F32)


def _dot_tn(a, b):
    return lax.dot_general(a, b, (((0,), (0,)), ((), ())), preferred_element_type=F32)


def _softplus(u):
    return jnp.maximum(u, 0.0) + jnp.log1p(jnp.exp(-jnp.abs(u)))


def _modulate(x, ng, sh, sc):
    ms = jnp.mean(x * x, axis=-1, keepdims=True)
    return x * lax.rsqrt(ms + NORM_EPS) * ng * (1.0 + sc) + sh


def _split_bf16(x):
    hi = x.astype(BF16)
    lo = (x - hi.astype(F32)).astype(BF16)
    return hi, lo


def _cast_specs(weights, steps, step_index):
    ins, specs, out_shapes = [], [], []
    for w in weights:
        w2 = w.reshape(-1, w.shape[-1])
        rows = w2.shape[0]
        nblk = max(n for n in range(1, steps + 1) if rows % n == 0 and (rows // n) % 16 == 0)
        specs.append(pl.BlockSpec(
            (rows // nblk, w2.shape[1]),
            lambda *idx, nblk=nblk: (jnp.minimum(step_index(*idx), nblk - 1), 0)))
        ins.append(w2)
        out_shapes.append(jax.ShapeDtypeStruct(w2.shape, BF16))
    return ins, specs, out_shapes


def _with_casts(body, n_in, n_out, n_cast):
    def kernel(*refs):
        cast_in = refs[n_in:n_in + n_cast]
        outs = refs[n_in + n_cast:n_in + n_cast + n_out]
        cast_out = refs[n_in + n_cast + n_out:n_in + 2 * n_cast + n_out]
        for src, dst in zip(cast_in, cast_out):
            dst[...] = src[...].astype(BF16)
        body(*refs[:n_in], *outs, *refs[n_in + 2 * n_cast + n_out:])
    return kernel


def _shift_select(tm, shifts):
    rr = jnp.arange(tm)[:, None]
    cc = jnp.arange(tm)[None, :]
    return jnp.concatenate([(cc == rr - s) for s in shifts], axis=0).astype(BF16)


def _mod_kernel(c_ref, w_ref, b_ref, o_ref):
    c = c_ref[...]
    cond = c * jax.nn.sigmoid(c)
    o_ref[0] = _dot(cond.astype(BF16), w_ref[0].astype(BF16)) + b_ref[0]


def _ada_mod(c, ada_w, ada_b, later_weights):
    depth, d, d6 = ada_w.shape
    b = c.shape[0]
    rows = 8
    c8 = jnp.pad(c, ((0, rows - b), (0, 0)))
    tn = 1024
    nj = d6 // tn
    cast_in, cast_specs, cast_shapes = _cast_specs(later_weights, depth * nj, lambda i, j: i * nj + j)
    in_specs = [pl.BlockSpec((rows, d), lambda i, j: (0, 0)),
                pl.BlockSpec((1, d, tn), lambda i, j: (i, 0, j)),
                pl.BlockSpec((1, 1, tn), lambda i, j: (i, 0, j))]
    outs = pl.pallas_call(
        _with_casts(_mod_kernel, len(in_specs), 1, len(cast_in)),
        grid=(depth, nj),
        in_specs=in_specs + cast_specs,
        out_specs=[pl.BlockSpec((1, rows, tn), lambda i, j: (i, 0, j))] + cast_specs,
        out_shape=[jax.ShapeDtypeStruct((depth, rows, d6), F32)] + cast_shapes,
        compiler_params=pltpu.CompilerParams(dimension_semantics=("arbitrary", "arbitrary")),
        name="ada_mod",
    )(c8, ada_w, ada_b.reshape(depth, 1, d6), *cast_in)
    return outs[0][:, :b], [o.reshape(w.shape) for o, w in zip(outs[1:], later_weights)]


def _rwkv_pre_kernel(x_ref, sh_ref, sc_ref, ng_ref, mu_ref, wrkv_ref, w1_ref, w2_ref,
                     a1_ref, a2_ref, g1_ref, g2_ref, vec_ref, seg_ref, segt_ref, sel_ref,
                     r_out, lw_out, k_out, v_out, kk_out, b_out, g_out, bonus_out,
                     hbuf):
    t = pl.program_id(1)
    tm = x_ref.shape[1]
    d = x_ref.shape[2]

    h = _modulate(x_ref[0], ng_ref[...], sh_ref[0], sc_ref[0])

    @pl.when(t == 0)
    def _():
        hbuf[...] = jnp.zeros((8, d), F32)

    hprev = _dot(sel_ref[...], h.astype(BF16))
    first = lax.broadcasted_iota(I32, (tm, d), 0) == 0
    hprev = jnp.where(first, hbuf[7:8, :], hprev)
    hbuf[...] = h[tm - 8:tm, :]
    xx = hprev - h

    def mix(p):
        return (h + xx * mu_ref[p:p + 1, :]).astype(BF16)

    r = _dot(mix(0), wrkv_ref[0])
    k = _dot(mix(1), wrkv_ref[1])
    v = _dot(mix(2), wrkv_ref[2])
    wl = _dot(jnp.tanh(_dot(mix(3), w1_ref[...])).astype(BF16), w2_ref[...])
    al = _dot(_dot(mix(4), a1_ref[...]).astype(BF16), a2_ref[...])
    g = _dot(jax.nn.sigmoid(_dot(mix(5), g1_ref[...])).astype(BF16), g2_ref[...])

    w0 = vec_ref[0:1, :]
    a0 = vec_ref[1:2, :]
    k_k = vec_ref[2:3, :]
    k_a = vec_ref[3:4, :]
    r_k = vec_ref[4:5, :]

    def headsum(z):
        s = _dot(z.astype(BF16), seg_ref[...])
        s_hi, s_lo = _split_bf16(s)
        return _dot(jnp.concatenate([s_hi, s_lo], axis=1), segt_ref[...])

    lw = -0.6065306597126334 * jax.nn.sigmoid(w0 + wl)
    a = jax.nn.sigmoid(a0 + al)
    kk = k * k_k
    kk = kk * lax.rsqrt(jnp.maximum(headsum(kk * kk), 1e-24))
    k2 = k * (1.0 + (a - 1.0) * k_a)
    bonus = headsum(r * k2 * r_k) * v

    r_out[0] = r
    lw_out[0] = lw
    k_out[0] = k2
    v_out[0] = v
    kk_out[0] = kk
    b_out[0] = kk * a
    g_out[0] = g
    bonus_out[0] = bonus


def _pad_to(x, axis, size):
    pad = [(0, 0)] * x.ndim
    pad[axis] = (0, size - x.shape[axis])
    return jnp.pad(x, pad)


def _rwkv_pre(x, sh, sc, ng, mu, w_rkv, w1, w2, a1, a2, g1, g2, w0, a0, k_k, k_a, r_k, tm,
              later_weights):
    b, t, d = x.shape
    nh = d // HEAD
    lw_pad = LANES * pl.cdiv(w1.shape[1], LANES)
    la_pad = LANES * pl.cdiv(a1.shape[1], LANES)
    lg_pad = LANES * pl.cdiv(g1.shape[1], LANES)
    vecs = _pad_to(jnp.stack([w0, a0, k_k, k_a, r_k.reshape(d)]), 0, 8)
    head_of_lane = jnp.arange(d) // HEAD
    seg = (head_of_lane[:, None] == jnp.arange(LANES)[None, :]).astype(BF16)
    segt = jnp.concatenate([seg.T, seg.T], axis=0)
    del nh
    full = lambda *shape: pl.BlockSpec(shape, lambda bi, ti: (0,) * len(shape))
    tok = pl.BlockSpec((1, tm, d), lambda bi, ti: (bi, ti, 0))
    vec = pl.BlockSpec((1, 1, d), lambda bi, ti: (bi, 0, 0))
    nt = t // tm
    cast_in, cast_specs, cast_shapes = _cast_specs(later_weights, b * nt, lambda bi, ti: bi * nt + ti)
    in_specs = [tok, vec, vec, full(1, d), full(8, d), full(3, d, d),
                full(d, lw_pad), full(lw_pad, d), full(d, la_pad), full(la_pad, d),
                full(d, lg_pad), full(lg_pad, d), full(8, d), full(d, LANES), full(2 * LANES, d),
                full(tm, tm)]
    outs = pl.pallas_call(
        _with_casts(_rwkv_pre_kernel, len(in_specs), 8, len(cast_in)),
        grid=(b, nt),
        in_specs=in_specs + cast_specs,
        out_specs=[tok] * 8 + cast_specs,
        out_shape=[jax.ShapeDtypeStruct((b, t, d), F32)] * 8 + cast_shapes,
        scratch_shapes=[pltpu.VMEM((8, d), F32)],
        compiler_params=pltpu.CompilerParams(
            dimension_semantics=("arbitrary", "arbitrary"), vmem_limit_bytes=VMEM_LIMIT),
        name="rwkv_pre",
    )(x, sh[:, None, :], sc[:, None, :], ng[None, :], _pad_to(mu, 0, 8), w_rkv.astype(BF16),
      _pad_to(w1, 1, lw_pad).astype(BF16), _pad_to(w2, 0, lw_pad).astype(BF16),
      _pad_to(a1, 1, la_pad).astype(BF16), _pad_to(a2, 0, la_pad).astype(BF16),
      _pad_to(g1, 1, lg_pad).astype(BF16), _pad_to(g2, 0, lg_pad).astype(BF16),
      vecs, seg, segt, _shift_select(tm, [1]), *cast_in)
    casts = [o.reshape(w.shape) for o, w in zip(outs[8:], later_weights)]
    return outs[:8], casts


def _wkv_scan_kernel(r_ref, lw_ref, k_ref, v_ref, kk_ref, b_ref, g_ref, bonus_ref,
                     gnw_ref, gnb_ref, o_ref, h_scr, *, chunks_per_step):
    L = CHUNK
    W = GROUP
    ngroups = r_ref.shape[2] // W

    @pl.when(pl.program_id(1) == 0)
    def _():
        h_scr[...] = jnp.zeros_like(h_scr)

    row = lax.broadcasted_iota(I32, (L, W), 0)
    lane = lax.broadcasted_iota(I32, (L, W), 1)
    sidx = lane & (L - 1)
    lane_head = lane >> 6
    strict = sidx < row
    incl = sidx <= row
    eye = (sidx == row).astype(F32)
    same16 = (row >> 4) == (sidx >> 4)
    same32 = (row >> 5) == (sidx >> 5)
    m16 = strict & same16
    m32 = strict & same32 & jnp.logical_not(same16)
    m64 = strict & jnp.logical_not(same32)
    rb = lax.broadcasted_iota(I32, (W, W), 0)
    cb = lax.broadcasted_iota(I32, (W, W), 1)
    bmask = (rb >> 6) == (cb >> 6)
    diag = rb == cb
    ones_bd = bmask.astype(BF16)
    tri_r = lax.broadcasted_iota(I32, (L, 2 * L), 0)
    tri_c = lax.broadcasted_iota(I32, (L, 2 * L), 1)
    tri2 = ((tri_c & (L - 1)) <= tri_r).astype(BF16)

    def bd(y):
        yt = jnp.concatenate([y] * HEADS_PER_GROUP, axis=0)
        return jnp.where(bmask, yt, 0.0).astype(BF16)

    def hmm(x, ybd):
        return _dot(x.astype(BF16), ybd)

    streams = [(q, j) for j in range(chunks_per_step) for q in range(ngroups)]
    S = range(len(streams))

    def ld(ref, s):
        q, j = streams[s]
        return ref[0, j * L:(j + 1) * L, q * W:(q + 1) * W]

    r = [ld(r_ref, s) for s in S]
    lw = [ld(lw_ref, s) for s in S]
    k = [ld(k_ref, s) for s in S]
    v = [ld(v_ref, s) for s in S]
    kk = [ld(kk_ref, s) for s in S]
    bv = [ld(b_ref, s) for s in S]

    def cumlog(x):
        hi, lo = _split_bf16(x)
        return _dot(tri2, jnp.concatenate([hi, lo], axis=0))

    cl = [cumlog(lw[s]) for s in S]
    cl_last = [cl[s][L - 1:L, :] for s in S]
    e_pos = [jnp.exp(cl[s]) for s in S]
    e_neg = [jnp.exp(-cl[s]) for s in S]
    e_end = [jnp.exp(cl_last[s] - cl[s]) for s in S]
    rt = [r[s] * e_pos[s] for s in S]
    at = [-kk[s] * jnp.exp(cl[s] - lw[s]) for s in S]
    bt = [bv[s] * e_neg[s] for s in S]
    kt = [k[s] * e_neg[s] for s in S]

    def gram(s):
        x = jnp.concatenate([at[s], rt[s]], axis=0).astype(BF16)
        ys = [jnp.where(lane_head == hh, bt[s], 0.0) for hh in range(HEADS_PER_GROUP)]
        ys += [jnp.where(lane_head == hh, kt[s], 0.0) for hh in range(HEADS_PER_GROUP)]
        return _dot_nt(x, jnp.concatenate(ys, axis=0).astype(BF16))

    gm = [gram(s) for s in S]
    a_ab = [jnp.where(strict, gm[s][:L, :W], 0.0) for s in S]
    a_ak = [jnp.where(strict, gm[s][:L, W:], 0.0) for s in S]
    a_rb = [jnp.where(incl, gm[s][L:, :W], 0.0) for s in S]
    a_rk = [jnp.where(incl, gm[s][L:, W:], 0.0) for s in S]

    a0 = [jnp.where(m16, a_ab[s], 0.0) for s in S]
    pw = [hmm(a0[s], bd(a0[s])) for s in S]
    tinv = [eye + a0[s] for s in S]
    for _ in range(2):
        ts = [_dot(jnp.concatenate([tinv[s], pw[s]], axis=0).astype(BF16), bd(pw[s])) for s in S]
        tinv = [tinv[s] + ts[s][:L] for s in S]
        pw = [ts[s][L:] for s in S]
    tinv = [tinv[s] + hmm(tinv[s], bd(pw[s])) for s in S]
    for msk in (m32, m64):
        inner = [hmm(jnp.where(msk, a_ab[s], 0.0), bd(tinv[s])) for s in S]
        tinv = [tinv[s] + hmm(tinv[s], bd(inner[s])) for s in S]

    vbd = [bd(v[s]) for s in S]
    avs = [_dot(jnp.concatenate([a_ak[s], a_rk[s]], axis=0).astype(BF16), vbd[s]) for s in S]
    av = [avs[s][:L] for s in S]
    tx = [_dot(tinv[s].astype(BF16), jnp.concatenate([bd(at[s]), bd(av[s])], axis=1)) for s in S]
    ahat = [tx[s][:, :W] for s in S]
    vp = [tx[s][:, W:] for s in S]
    ox = [_dot(a_rb[s].astype(BF16), jnp.concatenate([bd(ahat[s]), bd(vp[s])], axis=1)) for s in S]
    rhat = [rt[s] + ox[s][:, :W] for s in S]
    o_intra = [ox[s][:, W:] + avs[s][L:] for s in S]

    def state_terms(s):
        z = jnp.concatenate([bv[s] * e_end[s], k[s] * e_end[s]], axis=0).astype(BF16)
        wm = jnp.concatenate(
            [jnp.concatenate([ahat[s], vp[s]], axis=1),
             jnp.concatenate([jnp.zeros((L, W), F32), v[s]], axis=1)], axis=0).astype(BF16)
        mn = _dot_tn(z, wm)
        m_mat = jnp.where(bmask, mn[:, :W], 0.0) + jnp.where(diag, jnp.exp(cl_last[s]), 0.0)
        return m_mat, jnp.where(bmask, mn[:, W:], 0.0)

    mn = [state_terms(s) for s in S]

    o = [None] * len(streams)
    hq = [h_scr[q] for q in range(ngroups)]
    for j in range(chunks_per_step):
        for q in range(ngroups):
            s = streams.index((q, j))
            m_hi, m_lo = _split_bf16(mn[s][0])
            lhs = jnp.concatenate([m_hi, m_lo, rhat[s].astype(BF16)], axis=0)
            res = _dot(lhs, hq[q].astype(BF16))
            o[s] = res[2 * W:] + o_intra[s]
            hq[q] = res[:W] + res[W:2 * W] + mn[s][1]
    for q in range(ngroups):
        h_scr[q] = hq[q]

    def headmean(zs):
        parts = []
        for z in zs:
            parts += list(_split_bf16(z))
        red = _dot(jnp.concatenate(parts, axis=0), ones_bd) * (1.0 / HEAD)
        return [red[2 * L * s:2 * L * s + L] + red[2 * L * s + L:2 * L * (s + 1)] for s in S]

    mean = headmean(o)
    dlt = [o[s] - mean[s] for s in S]
    sq = jnp.concatenate([(dlt[s] * dlt[s]).astype(BF16) for s in S], axis=0)
    var_all = _dot(sq, ones_bd) * (1.0 / HEAD)
    var = [var_all[L * s:L * (s + 1)] for s in S]
    for s in S:
        q, j = streams[s]
        gsl = slice(q * W, (q + 1) * W)
        rsl = slice(j * L, (j + 1) * L)
        yn = dlt[s] * lax.rsqrt(var[s] + GN_EPS) * gnw_ref[:, gsl] + gnb_ref[:, gsl]
        o_ref[0, rsl, gsl] = ((yn + bonus_ref[0, rsl, gsl]) * g_ref[0, rsl, gsl]).astype(o_ref.dtype)


def _wkv_scan(r, lw, k, v, kk, bv, g, bonus, gn_w, gn_b, chunks_per_step):
    b, t, d = r.shape
    lb = CHUNK * chunks_per_step
    tok = pl.BlockSpec((1, lb, d), lambda bi, ci: (bi, ci, 0))
    vec = pl.BlockSpec((1, d), lambda bi, ci: (0, 0))
    return pl.pallas_call(
        functools.partial(_wkv_scan_kernel, chunks_per_step=chunks_per_step),
        grid=(b, t // lb),
        in_specs=[tok] * 8 + [vec, vec],
        out_specs=tok,
        out_shape=jax.ShapeDtypeStruct((b, t, d), BF16),
        scratch_shapes=[pltpu.VMEM((d // GROUP, GROUP, GROUP), F32)],
        compiler_params=pltpu.CompilerParams(
            dimension_semantics=("arbitrary", "arbitrary"), vmem_limit_bytes=VMEM_LIMIT),
        name="wkv_scan",
    )(r, lw, k, v, kk, bv, g, bonus, gn_w[None, :], gn_b[None, :])


def _swiglu(h, wg, wu, wd):
    g = _dot(h, wg)
    u = _dot(h, wu)
    return _dot((g * jax.nn.sigmoid(g) * u).astype(BF16), wd)


def _ffn_dense_kernel(a_ref, x_ref, gt1_ref, sh_ref, sc_ref, gt2_ref, ng_ref, wo_ref,
                      wg_ref, wu_ref, wd_ref, o_ref):
    x1 = x_ref[0] + (1.0 + gt1_ref[0]) * _dot(a_ref[0], wo_ref[...])
    h = _modulate(x1, ng_ref[...], sh_ref[0], sc_ref[0]).astype(BF16)
    o_ref[0] = x1 + (1.0 + gt2_ref[0]) * _swiglu(h, wg_ref[...], wu_ref[...], wd_ref[...])


def _resident(shape, index_map):
    return pl.BlockSpec(shape, index_map, pipeline_mode=pl.Buffered(1))


def _ffn_dense(a, x, gt1, sh, sc, gt2, ng, w_o, w_gu, w_d, tm, later_weights):
    b, t, d = x.shape
    f = w_d.shape[0]
    nt = t // tm
    tok = pl.BlockSpec((1, tm, d), lambda bi, ti: (bi, ti, 0))
    vec = pl.BlockSpec((1, 1, d), lambda bi, ti: (bi, 0, 0))
    cast_in, cast_specs, cast_shapes = _cast_specs(later_weights, b * nt, lambda bi, ti: bi * nt + ti)
    in_specs = [tok, tok, vec, vec, vec, vec, pl.BlockSpec((1, d), lambda bi, ti: (0, 0)),
                _resident((d, d), lambda bi, ti: (0, 0)),
                _resident((d, f), lambda bi, ti: (0, 0)),
                _resident((d, f), lambda bi, ti: (0, 1)),
                _resident((f, d), lambda bi, ti: (0, 0))]
    outs = pl.pallas_call(
        _with_casts(_ffn_dense_kernel, len(in_specs), 1, len(cast_in)),
        grid=(b, nt),
        in_specs=in_specs + cast_specs,
        out_specs=[tok] + cast_specs,
        out_shape=[jax.ShapeDtypeStruct((b, t, d), F32)] + cast_shapes,
        compiler_params=pltpu.CompilerParams(
            dimension_semantics=("arbitrary", "arbitrary"), vmem_limit_bytes=VMEM_LIMIT),
        name="ffn_dense",
    )(a, x, gt1[:, None, :], sh[:, None, :], sc[:, None, :], gt2[:, None, :], ng[None, :],
      w_o, w_gu, w_gu, w_d, *cast_in)
    return outs[0], [o.reshape(w.shape) for o, w in zip(outs[1:], later_weights)]


def _lru_kernel(x_ref, sh_ref, sc_ref, gt_ref, ng_ref, win_ref, cw_ref, cb_ref, wg_ref, bg_ref,
                lam_ref, wout_ref, o_ref, xbuf, abuf, bbuf, hbuf, ga, gb, carry):
    t = pl.program_id(1)
    tm = x_ref.shape[1]
    w = win_ref.shape[1] // 2
    nblk = w // LRU_BLOCK
    nslab = w // LANES
    ng = tm // 8

    @pl.when(t == 0)
    def _():
        xbuf[0:8, :] = jnp.zeros((8, w), F32)
        carry[...] = jnp.zeros_like(carry)
        ga[:, :, 0:ng, :] = jnp.ones((2, nslab, ng, LANES), F32)
        gb[:, :, 0:ng, :] = jnp.zeros((2, nslab, ng, LANES), F32)

    x = x_ref[0]
    h = _modulate(x, ng_ref[...], sh_ref[0], sc_ref[0]).astype(BF16)
    xg = _dot(h, win_ref[...])
    xb = xg[:, :w]
    gx = xg[:, w:]
    gate = 0.5 * gx * (1.0 + jnp.tanh(0.7978845608028654 * (gx + 0.044715 * gx * gx * gx)))

    xbuf[8:8 + tm, :] = xb
    conv = cb_ref[...] + cw_ref[CONV_WIDTH - 1:CONV_WIDTH, :] * xb
    for jj in range(CONV_WIDTH - 1):
        shift = CONV_WIDTH - 1 - jj
        conv = conv + cw_ref[jj:jj + 1, :] * xbuf[8 - shift:8 - shift + tm, :]
    xbuf[0:8, :] = xb[tm - 8:tm, :]

    conv_b = conv.astype(BF16)
    rs, is_ = [], []
    for n in range(nblk):
        gts = _dot(conv_b[:, n * LRU_BLOCK:(n + 1) * LRU_BLOCK], wg_ref[n]) + bg_ref[n]
        gts = jax.nn.sigmoid(gts)
        rs.append(gts[:, :LRU_BLOCK])
        is_.append(gts[:, LRU_BLOCK:])
    r_t = jnp.concatenate(rs, axis=1)
    i_t = jnp.concatenate(is_, axis=1)

    log_a = -LRU_C * r_t * _softplus(-lam_ref[...])
    a_t = jnp.exp(log_a)
    b_t = jnp.sqrt(-jnp.tanh(log_a) * (a_t * a_t + 1.0)) * (i_t * conv)

    for j in range(nslab):
        abuf[j] = a_t[:, j * LANES:(j + 1) * LANES]
        bbuf[j] = b_t[:, j * LANES:(j + 1) * LANES]
    hs_slabs = []
    for j in range(nslab):
        a_loc = [abuf[j, pl.ds(0, ng, stride=8), :]]
        b_loc = [bbuf[j, pl.ds(0, ng, stride=8), :]]
        for s in range(1, 8):
            a_s = abuf[j, pl.ds(s, ng, stride=8), :]
            b_s = bbuf[j, pl.ds(s, ng, stride=8), :]
            b_loc.append(a_s * b_loc[-1] + b_s)
            a_loc.append(a_s * a_loc[-1])
        ga[0, j, ng:2 * ng, :] = a_loc[-1]
        gb[0, j, ng:2 * ng, :] = b_loc[-1]
        step = 1
        src = 0
        while step < ng:
            a_cur = ga[src, j, ng:2 * ng, :]
            b_cur = gb[src, j, ng:2 * ng, :]
            a_sh = ga[src, j, ng - step:2 * ng - step, :]
            b_sh = gb[src, j, ng - step:2 * ng - step, :]
            ga[1 - src, j, ng:2 * ng, :] = a_cur * a_sh
            gb[1 - src, j, ng:2 * ng, :] = a_cur * b_sh + b_cur
            src = 1 - src
            step *= 2
        c_in = carry[0:1, j * LANES:(j + 1) * LANES]
        h_in = gb[src, j, ng - 1:2 * ng - 1, :] + ga[src, j, ng - 1:2 * ng - 1, :] * c_in
        for s in range(8):
            h_s = b_loc[s] + a_loc[s] * h_in
            hbuf[j, pl.ds(s, ng, stride=8), :] = h_s
        carry[:, j * LANES:(j + 1) * LANES] = jnp.broadcast_to(h_s[ng - 1:ng, :], (8, LANES))
        hs_slabs.append(hbuf[j])
    hs = jnp.concatenate(hs_slabs, axis=1)

    y = _dot((hs * gate).astype(BF16), wout_ref[...])
    o_ref[0] = x + (1.0 + gt_ref[0]) * y


def _lru_block(x, sh, sc, gt, ng, w_in, conv_w, conv_b, w_gates, b_gates, lam, w_out, tm,
               later_weights):
    b, t, d = x.shape
    w = w_out.shape[0]
    nblk = w // LRU_BLOCK
    nt = t // tm
    full = lambda *shape: pl.BlockSpec(shape, lambda bi, ti: (0,) * len(shape))
    tok = pl.BlockSpec((1, tm, d), lambda bi, ti: (bi, ti, 0))
    vec = pl.BlockSpec((1, 1, d), lambda bi, ti: (bi, 0, 0))
    cast_in, cast_specs, cast_shapes = _cast_specs(later_weights, b * nt, lambda bi, ti: bi * nt + ti)
    in_specs = [tok, vec, vec, vec, full(1, d), full(d, 2 * w), full(CONV_WIDTH, w), full(1, w),
                full(nblk, LRU_BLOCK, 2 * LRU_BLOCK), full(nblk, 1, 2 * LRU_BLOCK),
                full(1, w), full(w, d)]
    outs = pl.pallas_call(
        _with_casts(_lru_kernel, len(in_specs), 1, len(cast_in)),
        grid=(b, nt),
        in_specs=in_specs + cast_specs,
        out_specs=[tok] + cast_specs,
        out_shape=[jax.ShapeDtypeStruct((b, t, d), F32)] + cast_shapes,
        scratch_shapes=[pltpu.VMEM((tm + 8, w), F32)]
        + [pltpu.VMEM((w // LANES, tm, LANES), F32)] * 3
        + [pltpu.VMEM((2, w // LANES, tm // 4, LANES), F32)] * 2
        + [pltpu.VMEM((8, w), F32)],
        compiler_params=pltpu.CompilerParams(
            dimension_semantics=("arbitrary", "arbitrary"), vmem_limit_bytes=VMEM_LIMIT),
        name="rglru_block",
    )(x, sh[:, None, :], sc[:, None, :], gt[:, None, :], ng[None, :], w_in,
      conv_w, conv_b[None, :], w_gates, b_gates[:, None, :], lam[None, :], w_out, *cast_in)
    return outs[0], [o.reshape(wt.shape) for o, wt in zip(outs[1:], later_weights)]


def _router_kernel(x_ref, sh_ref, sc_ref, ng_ref, wr_ref, br_ref, tri_ref,
                   h_out, eid_out, rank_out, prob_out, cnt_out, cnt_scr):
    i = pl.program_id(0)

    @pl.when(i == 0)
    def _():
        cnt_scr[...] = jnp.zeros_like(cnt_scr)

    h = _modulate(x_ref[...], ng_ref[...], sh_ref[0], sc_ref[0])
    h_out[...] = h
    h_hi, h_lo = _split_bf16(h)
    w_hi, w_lo = _split_bf16(wr_ref[...])
    logits = _dot_nt(w_hi, h_hi) + _dot_nt(w_hi, h_lo) + _dot_nt(w_lo, h_hi) + br_ref[:, 0:1]

    n_rows = logits.shape[0]
    eidx = lax.broadcasted_iota(I32, logits.shape, 0)
    m1 = jnp.max(logits, axis=0, keepdims=True)
    i1 = jnp.min(jnp.where(logits == m1, eidx, n_rows), axis=0, keepdims=True)
    l2 = jnp.where(eidx == i1, -jnp.inf, logits)
    m2 = jnp.max(l2, axis=0, keepdims=True)
    i2 = jnp.min(jnp.where(l2 == m2, eidx, n_rows), axis=0, keepdims=True)
    e = jnp.exp(m2 - m1)
    p1 = 1.0 / (1.0 + e)
    p2 = e / (1.0 + e)

    oh1 = (eidx == i1).astype(F32)
    oh2 = (eidx == i2).astype(F32)
    oh = oh1 + oh2
    before = _dot(oh.astype(BF16), tri_ref[...]) + cnt_scr[:, 0:1]
    rank1 = jnp.sum(before * oh1, axis=0, keepdims=True)
    rank2 = jnp.sum(before * oh2, axis=0, keepdims=True)
    cnt_scr[...] = cnt_scr[...] + jnp.sum(oh, axis=1, keepdims=True)

    eid_out[0:1, :] = i1
    eid_out[1:2, :] = i2
    rank_out[0:1, :] = rank1.astype(I32)
    rank_out[1:2, :] = rank2.astype(I32)
    prob_out[0:1, :] = p1
    prob_out[1:2, :] = p2
    cnt_out[...] = cnt_scr[...].astype(I32)


def _router(x2d, sh, sc, ng, w_router, b_router, tiles_per_batch, tm):
    n, d = x2d.shape
    e = w_router.shape[1]
    ep = 16 * pl.cdiv(e, 16)
    wr = _pad_to(w_router.T, 0, ep)
    br = jnp.broadcast_to(jnp.concatenate([b_router, jnp.full((ep - e,), -1e30, F32)])[:, None],
                          (ep, LANES))
    tri = (jnp.arange(tm)[:, None] < jnp.arange(tm)[None, :]).astype(BF16)
    tok = pl.BlockSpec((tm, d), lambda i: (i, 0))
    vec = pl.BlockSpec((1, 1, d), lambda i: (i // tiles_per_batch, 0, 0))
    two = pl.BlockSpec((TOP_K, tm), lambda i: (0, i))
    const = lambda *shape: pl.BlockSpec(shape, lambda i: (0,) * len(shape))
    return pl.pallas_call(
        _router_kernel,
        grid=(n // tm,),
        in_specs=[tok, vec, vec, const(1, d), const(ep, d), const(ep, LANES), const(tm, tm)],
        out_specs=[tok, two, two, two, const(ep, LANES)],
        out_shape=[jax.ShapeDtypeStruct((n, d), F32), jax.ShapeDtypeStruct((TOP_K, n), I32),
                   jax.ShapeDtypeStruct((TOP_K, n), I32), jax.ShapeDtypeStruct((TOP_K, n), F32),
                   jax.ShapeDtypeStruct((ep, LANES), I32)],
        scratch_shapes=[pltpu.VMEM((ep, LANES), F32)],
        compiler_params=pltpu.CompilerParams(
            dimension_semantics=("arbitrary",), vmem_limit_bytes=VMEM_LIMIT),
        name="moe_router",
    )(x2d, sh[:, None, :], sc[:, None, :], ng[None, :], wr, br, tri)


def _moe_ffn_kernel(gid_ref, valid_ref, src0_ref, src_next_ref, dst_prev_ref, h_hbm,
                    wg_ref, wu_ref, wd_ref, ys_hbm, x0, x1, y0, y1, gsem, ssem,
                    *, n_tiles, tm, spare_row):
    del gid_ref
    i = pl.program_id(0)
    xbufs = (x0, x1)
    ybufs = (y0, y1)
    valid = valid_ref[jnp.minimum(i, n_tiles - 1)] > 0

    def gather(src_ref, xbuf):
        for t in range(tm):
            pltpu.make_async_copy(h_hbm.at[pl.ds(src_ref[0, 0, t], 1)], xbuf.at[pl.ds(t, 1)],
                                  gsem).start()

    def scatter(ybuf):
        for t in range(tm):
            pltpu.make_async_copy(ybuf.at[pl.ds(t, 1)],
                                  ys_hbm.at[pl.ds(dst_prev_ref[0, 0, t], 1)], ssem).start()

    def wait_gather(xbuf):
        pltpu.make_async_copy(h_hbm.at[pl.ds(0, tm)], xbuf, gsem).wait()

    def wait_scatter(ybuf):
        pltpu.make_async_copy(ybuf, ys_hbm.at[pl.ds(0, tm)], ssem).wait()

    @pl.when(i == 0)
    def _():
        y0[...] = jnp.zeros_like(y0)
        y1[...] = jnp.zeros_like(y1)
        pltpu.make_async_copy(y0, ys_hbm.at[pl.ds(spare_row, tm)], ssem).start()
        gather(src0_ref, x0)

    for par in range(2):
        for compute in (True, False):
            @pl.when((i < n_tiles) & (i % 2 == par) & (valid if compute else jnp.logical_not(valid)))
            def _(par=par, compute=compute):
                xa, xb, ya, yb = xbufs[par], xbufs[1 - par], ybufs[par], ybufs[1 - par]
                wait_gather(xa)
                wait_scatter(ya)
                gather(src_next_ref, xb)
                scatter(yb)
                if compute:
                    ya[...] = _swiglu(xa[...].astype(BF16), wg_ref[0], wu_ref[0], wd_ref[0])

    @pl.when(i == n_tiles)
    def _():
        par = n_tiles % 2
        wait_gather(xbufs[par])
        wait_scatter(ybufs[par])
        scatter(ybufs[1 - par])
        wait_scatter(ybufs[1 - par])


def _moe_ffn(gid, valid, src, dst, h2d, w_gu, w_d, out_rows, spare_row, tm):
    n_tiles = gid.shape[0]
    _, d = h2d.shape
    f = w_d.shape[1]
    idx = lambda fn: pl.BlockSpec((1, 1, tm), fn, memory_space=pltpu.SMEM)
    last = n_tiles - 1
    grid_spec = pltpu.PrefetchScalarGridSpec(
        num_scalar_prefetch=2,
        grid=(n_tiles + 1,),
        in_specs=[idx(lambda i, gid, vld: (0, 0, 0)),
                  idx(lambda i, gid, vld: (jnp.minimum(i + 1, n_tiles), 0, 0)),
                  idx(lambda i, gid, vld: (i, 0, 0)),
                  pl.BlockSpec(memory_space=pl.ANY),
                  pl.BlockSpec((1, d, f), lambda i, gid, vld: (gid[jnp.minimum(i, last)], 0, 0)),
                  pl.BlockSpec((1, d, f), lambda i, gid, vld: (gid[jnp.minimum(i, last)], 0, 1)),
                  _resident((1, f, d), lambda i, gid, vld: (gid[jnp.minimum(i, last)], 0, 0))],
        out_specs=pl.BlockSpec(memory_space=pl.ANY),
        scratch_shapes=[pltpu.VMEM((tm, d), F32)] * 4 + [pltpu.SemaphoreType.DMA(())] * 2,
    )
    return pl.pallas_call(
        functools.partial(_moe_ffn_kernel, n_tiles=n_tiles, tm=tm, spare_row=spare_row),
        grid_spec=grid_spec,
        out_shape=jax.ShapeDtypeStruct((out_rows, d), F32),
        compiler_params=pltpu.CompilerParams(
            dimension_semantics=("arbitrary",), vmem_limit_bytes=VMEM_LIMIT),
        name="moe_ffn",
    )(gid, valid, src, src, dst, h2d, w_gu, w_gu, w_d)


def _combine_kernel(y0_ref, y1_ref, x_ref, gt_ref, prob_ref, fg_ref, o_ref):
    y = prob_ref[:, 0:1] * y0_ref[...] + prob_ref[:, 1:2] * y1_ref[...]
    x = x_ref[...] + (1.0 + gt_ref[0]) * y
    ms = jnp.mean(x * x, axis=-1, keepdims=True)
    o_ref[...] = x * lax.rsqrt(ms + NORM_EPS) * fg_ref[...]


def _combine(ys, x2d, gt, prob, final_g, tiles_per_batch, tc):
    n, d = x2d.shape
    nt = n // tc
    tok = pl.BlockSpec((tc, d), lambda i: (i, 0))
    return pl.pallas_call(
        _combine_kernel,
        grid=(nt,),
        in_specs=[tok, pl.BlockSpec((tc, d), lambda i: (nt + i, 0)), tok,
                  pl.BlockSpec((1, 1, d), lambda i: (i // tiles_per_batch, 0, 0)),
                  pl.BlockSpec((tc, TOP_K), lambda i: (i, 0)),
                  pl.BlockSpec((1, d), lambda i: (0, 0))],
        out_specs=tok,
        out_shape=jax.ShapeDtypeStruct((n, d), F32),
        compiler_params=pltpu.CompilerParams(
            dimension_semantics=("arbitrary",), vmem_limit_bytes=VMEM_LIMIT),
        name="moe_combine",
    )(ys, ys, x2d, gt[:, None, :], prob, final_g[None, :])


def _moe_layout(eid, rank, counts, n_experts, tm, n_tiles):
    n = eid.shape[1]
    rows = n_tiles * tm
    cnt = counts[:n_experts, 0]
    tiles = (cnt + tm - 1) // tm
    tile_end = jnp.cumsum(tiles)
    offsets = (tile_end - tiles) * tm
    first_rank = jnp.cumsum(cnt) - cnt

    def per_expert(table, idx):
        out = jnp.zeros(idx.shape, table.dtype)
        for e in range(n_experts):
            out = jnp.where(idx == e, table[e], out)
        return out

    pos = (per_expert(offsets, eid) + rank).T.reshape(-1)
    order = jnp.argsort(pos).astype(I32)
    tile_ids = jnp.arange(n_tiles, dtype=I32)
    last_id = jnp.minimum(tile_ids, tile_end[-1] - 1)
    gid = jnp.sum((last_id[:, None] >= tile_end[None, :]).astype(I32), axis=1)
    p = jnp.arange(rows, dtype=I32)
    e_of_p = jnp.repeat(gid, tm)
    within = p - per_expert(offsets, e_of_p)
    valid = within < per_expert(cnt, e_of_p)
    flat = order[jnp.clip(per_expert(first_rank, e_of_p) + within, 0, TOP_K * n - 1)]
    tok = flat // TOP_K
    src = jnp.where(valid, tok, 0)
    pad_id = jnp.cumsum(jnp.logical_not(valid).astype(I32)) - 1
    dst = jnp.where(valid, (flat % TOP_K) * n + tok, TOP_K * n + pad_id)
    spare = rows
    src = jnp.concatenate([src, jnp.zeros((tm,), I32)]).reshape(n_tiles + 1, 1, tm)
    dst = jnp.concatenate([spare + jnp.arange(tm, dtype=I32), dst]).reshape(n_tiles + 1, 1, tm)
    tile_valid = (tile_ids < tile_end[-1]).astype(I32)
    return gid, tile_valid, src.astype(I32), dst.astype(I32), spare + 2 * tm, spare + tm


def kernel(x, c, ada_w, ada_b, norm_g, final_g, rwkv_mu, rwkv_w_rkv, rwkv_w_o, rwkv_w0, rwkv_w1, rwkv_w2, rwkv_a0, rwkv_a1, rwkv_a2, rwkv_g1, rwkv_g2, rwkv_k_k, rwkv_k_a, rwkv_r_k, rwkv_gn_w, rwkv_gn_b, lru_w_in, lru_conv_w, lru_conv_b, lru_w_gates, lru_b_gates, lru_lam, lru_w_out, ffn_w_gu, ffn_w_d, moe_w_router, moe_b_router, moe_w_gu, moe_w_d):
    b, t, d = x.shape
    n = b * t
    n_experts = moe_w_router.shape[-1]
    mod, (w_rkv_b,) = _ada_mod(c, ada_w, ada_b, [rwkv_w_rkv[0]])

    def mods(i):
        return [mod[i, :, q * d:(q + 1) * d] for q in range(6)]

    sh1, sc1, gt1, sh2, sc2, gt2 = mods(0)
    (r, lw, k2, v, kk, bv, g, bonus), (w_o_b, ffn_gu_b, ffn_d_b, lru_in_b, lru_gates_b, lru_out_b) = _rwkv_pre(
        x, sh1, sc1, norm_g[0, 0], rwkv_mu[0], w_rkv_b, rwkv_w1[0], rwkv_w2[0],
        rwkv_a1[0], rwkv_a2[0], rwkv_g1[0], rwkv_g2[0], rwkv_w0[0], rwkv_a0[0],
        rwkv_k_k[0], rwkv_k_a[0], rwkv_r_k[0], tm=min(256, t),
        later_weights=[rwkv_w_o[0], ffn_w_gu[0], ffn_w_d[0], lru_w_in[0], lru_w_gates[0], lru_w_out[0]])
    yg = _wkv_scan(r, lw, k2, v, kk, bv, g, bonus, rwkv_gn_w[0], rwkv_gn_b[0],
                   chunks_per_step=min(4, t // CHUNK))
    x, (moe_gu_b,) = _ffn_dense(yg, x, gt1, sh2, sc2, gt2, norm_g[0, 1], w_o_b, ffn_gu_b, ffn_d_b,
                                tm=min(256, t), later_weights=[moe_w_gu[0]])

    sh1, sc1, gt1, sh2, sc2, gt2 = mods(1)
    x, (moe_d_b,) = _lru_block(x, sh1, sc1, gt1, norm_g[1, 0], lru_in_b, lru_conv_w[0], lru_conv_b[0],
                               lru_gates_b, lru_b_gates[0], lru_lam[0], lru_out_b, tm=min(512, t),
                               later_weights=[moe_w_d[0]])

    x2d = x.reshape(n, d)
    tm_r = min(512, t)
    h2, eid, rank, prob, counts = _router(x2d, sh2, sc2, norm_g[1, 1], moe_w_router[0],
                                          moe_b_router[0], t // tm_r, tm_r)
    tm_g = min(256, t)
    n_tiles = (TOP_K * n) // tm_g + n_experts
    gid, tile_valid, src, dst, out_rows, spare_row = _moe_layout(eid, rank, counts, n_experts, tm_g, n_tiles)
    ys = _moe_ffn(gid, tile_valid, src, dst, h2, moe_gu_b, moe_d_b, out_rows, spare_row, tm_g)
    tc = min(512, t)
    out = _combine(ys, x2d, gt2, prob.T, final_g, t // tc, tc)
    return out.reshape(b, t, d)
```

```python
import functools

import jax
import jax.numpy as jnp
from jax import lax
from jax.experimental import pallas as pl
from jax.experimental.pallas import tpu as pltpu

F32 = jnp.float32
BF16 = jnp.bfloat16
I32 = jnp.int32

HEAD = 64
CHUNK = 64
GROUP = 256
HEADS_PER_GROUP = GROUP // HEAD
GN_EPS = 64e-5
NORM_EPS = 1e-6
LRU_C = 8.0
CONV_WIDTH = 4
LRU_BLOCK = 256
TOP_K = 2
LANES = 128
VMEM_LIMIT = 56 * 1024 * 1024


def _dot(a, b):
    return jnp.dot(a, b, preferred_element_type=F32)


def _dot_nt(a, b):
    return lax.dot_general(a, b, (((1,), (1,)), ((), ())), preferred_element_type=F32)


def _dot_tn(a, b):
    return lax.dot_general(a, b, (((0,), (0,)), ((), ())), preferred_element_type=F32)


def _softplus(u):
    return jnp.maximum(u, 0.0) + jnp.log1p(jnp.exp(-jnp.abs(u)))


def _modulate(x, ng, sh, sc):
    ms = jnp.mean(x * x, axis=-1, keepdims=True)
    return x * lax.rsqrt(ms + NORM_EPS) * ng * (1.0 + sc) + sh


def _split_bf16(x):
    hi = x.astype(BF16)
    lo = (x - hi.astype(F32)).astype(BF16)
    return hi, lo


def _cast_specs(weights, steps, step_index):
    ins, specs, out_shapes = [], [], []
    for w in weights:
        w2 = w.reshape(-1, w.shape[-1])
        rows = w2.shape[0]
        nblk = max(n for n in range(1, steps + 1) if rows % n == 0 and (rows // n) % 16 == 0)
        specs.append(pl.BlockSpec(
            (rows // nblk, w2.shape[1]),
            lambda *idx, nblk=nblk: (jnp.minimum(step_index(*idx), nblk - 1), 0)))
        ins.append(w2)
        out_shapes.append(jax.ShapeDtypeStruct(w2.shape, BF16))
    return ins, specs, out_shapes


def _with_casts(body, n_in, n_out, n_cast):
    def kernel(*refs):
        cast_in = refs[n_in:n_in + n_cast]
        outs = refs[n_in + n_cast:n_in + n_cast + n_out]
        cast_out = refs[n_in + n_cast + n_out:n_in + 2 * n_cast + n_out]
        for src, dst in zip(cast_in, cast_out):
            dst[...] = src[...].astype(BF16)
        body(*refs[:n_in], *outs, *refs[n_in + 2 * n_cast + n_out:])
    return kernel


def _shift_select(tm, shifts):
    rr = jnp.arange(tm)[:, None]
    cc = jnp.arange(tm)[None, :]
    return jnp.concatenate([(cc == rr - s) for s in shifts], axis=0).astype(BF16)


def _mod_kernel(c_ref, w_ref, b_ref, o_ref):
    c = c_ref[...]
    cond = c * jax.nn.sigmoid(c)
    o_ref[0] = _dot(cond.astype(BF16), w_ref[0].astype(BF16)) + b_ref[0]


def _ada_mod(c, ada_w, ada_b, later_weights):
    depth, d, d6 = ada_w.shape
    b = c.shape[0]
    rows = 8
    c8 = jnp.pad(c, ((0, rows - b), (0, 0)))
    tn = 1024
    nj = d6 // tn
    cast_in, cast_specs, cast_shapes = _cast_specs(later_weights, depth * nj, lambda i, j: i * nj + j)
    in_specs = [pl.BlockSpec((rows, d), lambda i, j: (0, 0)),
                pl.BlockSpec((1, d, tn), lambda i, j: (i, 0, j)),
                pl.BlockSpec((1, 1, tn), lambda i, j: (i, 0, j))]
    outs = pl.pallas_call(
        _with_casts(_mod_kernel, len(in_specs), 1, len(cast_in)),
        grid=(depth, nj),
        in_specs=in_specs + cast_specs,
        out_specs=[pl.BlockSpec((1, rows, tn), lambda i, j: (i, 0, j))] + cast_specs,
        out_shape=[jax.ShapeDtypeStruct((depth, rows, d6), F32)] + cast_shapes,
        compiler_params=pltpu.CompilerParams(dimension_semantics=("arbitrary", "arbitrary")),
        name="ada_mod",
    )(c8, ada_w, ada_b.reshape(depth, 1, d6), *cast_in)
    return outs[0][:, :b], [o.reshape(w.shape) for o, w in zip(outs[1:], later_weights)]


def _rwkv_pre_kernel(x_ref, sh_ref, sc_ref, ng_ref, mu_ref, wrkv_ref, w1_ref, w2_ref,
                     a1_ref, a2_ref, g1_ref, g2_ref, vec_ref, seg_ref, segt_ref, sel_ref,
                     r_out, lw_out, k_out, v_out, kk_out, b_out, g_out, bonus_out,
                     hbuf):
    t = pl.program_id(1)
    tm = x_ref.shape[1]
    d = x_ref.shape[2]

    h = _modulate(x_ref[0], ng_ref[...], sh_ref[0], sc_ref[0])

    @pl.when(t == 0)
    def _():
        hbuf[...] = jnp.zeros((8, d), F32)

    hprev = _dot(sel_ref[...], h.astype(BF16))
    first = lax.broadcasted_iota(I32, (tm, d), 0) == 0
    hprev = jnp.where(first, hbuf[7:8, :], hprev)
    hbuf[...] = h[tm - 8:tm, :]
    xx = hprev - h

    def mix(p):
        return (h + xx * mu_ref[p:p + 1, :]).astype(BF16)

    r = _dot(mix(0), wrkv_ref[0])
    k = _dot(mix(1), wrkv_ref[1])
    v = _dot(mix(2), wrkv_ref[2])
    wl = _dot(jnp.tanh(_dot(mix(3), w1_ref[...])).astype(BF16), w2_ref[...])
    al = _dot(_dot(mix(4), a1_ref[...]).astype(BF16), a2_ref[...])
    g = _dot(jax.nn.sigmoid(_dot(mix(5), g1_ref[...])).astype(BF16), g2_ref[...])

    w0 = vec_ref[0:1, :]
    a0 = vec_ref[1:2, :]
    k_k = vec_ref[2:3, :]
    k_a = vec_ref[3:4, :]
    r_k = vec_ref[4:5, :]

    def headsum(z):
        s = _dot(z.astype(BF16), seg_ref[...])
        s_hi, s_lo = _split_bf16(s)
        return _dot(jnp.concatenate([s_hi, s_lo], axis=1), segt_ref[...])

    lw = -0.6065306597126334 * jax.nn.sigmoid(w0 + wl)
    a = jax.nn.sigmoid(a0 + al)
    kk = k * k_k
    kk = kk * lax.rsqrt(jnp.maximum(headsum(kk * kk), 1e-24))
    k2 = k * (1.0 + (a - 1.0) * k_a)
    bonus = headsum(r * k2 * r_k) * v

    r_out[0] = r
    lw_out[0] = lw
    k_out[0] = k2
    v_out[0] = v
    kk_out[0] = kk
    b_out[0] = kk * a
    g_out[0] = g
    bonus_out[0] = bonus


def _pad_to(x, axis, size):
    pad = [(0, 0)] * x.ndim
    pad[axis] = (0, size - x.shape[axis])
    return jnp.pad(x, pad)


def _rwkv_pre(x, sh, sc, ng, mu, w_rkv, w1, w2, a1, a2, g1, g2, w0, a0, k_k, k_a, r_k, tm,
              later_weights):
    b, t, d = x.shape
    nh = d // HEAD
    lw_pad = LANES * pl.cdiv(w1.shape[1], LANES)
    la_pad = LANES * pl.cdiv(a1.shape[1], LANES)
    lg_pad = LANES * pl.cdiv(g1.shape[1], LANES)
    vecs = _pad_to(jnp.stack([w0, a0, k_k, k_a, r_k.reshape(d)]), 0, 8)
    head_of_lane = jnp.arange(d) // HEAD
    seg = (head_of_lane[:, None] == jnp.arange(LANES)[None, :]).astype(BF16)
    segt = jnp.concatenate([seg.T, seg.T], axis=0)
    del nh
    full = lambda *shape: _resident(shape, lambda bi, ti: (0,) * len(shape))
    tok = pl.BlockSpec((1, tm, d), lambda bi, ti: (bi, ti, 0))
    vec = pl.BlockSpec((1, 1, d), lambda bi, ti: (bi, 0, 0))
    nt = t // tm
    cast_in, cast_specs, cast_shapes = _cast_specs(later_weights, b * nt, lambda bi, ti: bi * nt + ti)
    in_specs = [tok, vec, vec, full(1, d), full(8, d), full(3, d, d),
                full(d, lw_pad), full(lw_pad, d), full(d, la_pad), full(la_pad, d),
                full(d, lg_pad), full(lg_pad, d), full(8, d), full(d, LANES), full(2 * LANES, d),
                full(tm, tm)]
    outs = pl.pallas_call(
        _with_casts(_rwkv_pre_kernel, len(in_specs), 8, len(cast_in)),
        grid=(b, nt),
        in_specs=in_specs + cast_specs,
        out_specs=[tok] * 8 + cast_specs,
        out_shape=[jax.ShapeDtypeStruct((b, t, d), F32)] * 8 + cast_shapes,
        scratch_shapes=[pltpu.VMEM((8, d), F32)],
        compiler_params=pltpu.CompilerParams(
            dimension_semantics=("arbitrary", "arbitrary"), vmem_limit_bytes=VMEM_LIMIT),
        name="rwkv_pre",
    )(x, sh[:, None, :], sc[:, None, :], ng[None, :], _pad_to(mu, 0, 8), w_rkv.astype(BF16),
      _pad_to(w1, 1, lw_pad).astype(BF16), _pad_to(w2, 0, lw_pad).astype(BF16),
      _pad_to(a1, 1, la_pad).astype(BF16), _pad_to(a2, 0, la_pad).astype(BF16),
      _pad_to(g1, 1, lg_pad).astype(BF16), _pad_to(g2, 0, lg_pad).astype(BF16),
      vecs, seg, segt, _shift_select(tm, [1]), *cast_in)
    casts = [o.reshape(w.shape) for o, w in zip(outs[8:], later_weights)]
    return outs[:8], casts


def _wkv_scan_kernel(r_ref, lw_ref, k_ref, v_ref, kk_ref, b_ref, g_ref, bonus_ref,
                     gnw_ref, gnb_ref, o_ref, h_scr, *, chunks_per_step):
    L = CHUNK
    W = GROUP
    ngroups = r_ref.shape[2] // W

    @pl.when(pl.program_id(1) == 0)
    def _():
        h_scr[...] = jnp.zeros_like(h_scr)

    row = lax.broadcasted_iota(I32, (L, W), 0)
    lane = lax.broadcasted_iota(I32, (L, W), 1)
    sidx = lane & (L - 1)
    lane_head = lane >> 6
    strict = sidx < row
    incl = sidx <= row
    eye = (sidx == row).astype(F32)
    same16 = (row >> 4) == (sidx >> 4)
    same32 = (row >> 5) == (sidx >> 5)
    m16 = strict & same16
    m32 = strict & same32 & jnp.logical_not(same16)
    m64 = strict & jnp.logical_not(same32)
    rb = lax.broadcasted_iota(I32, (W, W), 0)
    cb = lax.broadcasted_iota(I32, (W, W), 1)
    bmask = (rb >> 6) == (cb >> 6)
    diag = rb == cb
    ones_bd = bmask.astype(BF16)
    tri_r = lax.broadcasted_iota(I32, (L, 2 * L), 0)
    tri_c = lax.broadcasted_iota(I32, (L, 2 * L), 1)
    tri2 = ((tri_c & (L - 1)) <= tri_r).astype(BF16)

    def bd(y):
        yt = jnp.concatenate([y] * HEADS_PER_GROUP, axis=0)
        return jnp.where(bmask, yt, 0.0).astype(BF16)

    def hmm(x, ybd):
        return _dot(x.astype(BF16), ybd)

    streams = [(q, j) for j in range(chunks_per_step) for q in range(ngroups)]
    S = range(len(streams))

    def ld(ref, s):
        q, j = streams[s]
        return ref[0, j * L:(j + 1) * L, q * W:(q + 1) * W]

    r = [ld(r_ref, s) for s in S]
    lw = [ld(lw_ref, s) for s in S]
    k = [ld(k_ref, s) for s in S]
    v = [ld(v_ref, s) for s in S]
    kk = [ld(kk_ref, s) for s in S]
    bv = [ld(b_ref, s) for s in S]

    def cumlog(x):
        hi, lo = _split_bf16(x)
        return _dot(tri2, jnp.concatenate([hi, lo], axis=0))

    cl = [cumlog(lw[s]) for s in S]
    cl_last = [cl[s][L - 1:L, :] for s in S]
    e_pos = [jnp.exp(cl[s]) for s in S]
    e_neg = [jnp.exp(-cl[s]) for s in S]
    e_end = [jnp.exp(cl_last[s] - cl[s]) for s in S]
    rt = [r[s] * e_pos[s] for s in S]
    at = [-kk[s] * jnp.exp(cl[s] - lw[s]) for s in S]
    bt = [bv[s] * e_neg[s] for s in S]
    kt = [k[s] * e_neg[s] for s in S]

    def gram(s):
        x = jnp.concatenate([at[s], rt[s]], axis=0).astype(BF16)
        ys = [jnp.where(lane_head == hh, bt[s], 0.0) for hh in range(HEADS_PER_GROUP)]
        ys += [jnp.where(lane_head == hh, kt[s], 0.0) for hh in range(HEADS_PER_GROUP)]
        return _dot_nt(x, jnp.concatenate(ys, axis=0).astype(BF16))

    gm = [gram(s) for s in S]
    a_ab = [jnp.where(strict, gm[s][:L, :W], 0.0) for s in S]
    a_ak = [jnp.where(strict, gm[s][:L, W:], 0.0) for s in S]
    a_rb = [jnp.where(incl, gm[s][L:, :W], 0.0) for s in S]
    a_rk = [jnp.where(incl, gm[s][L:, W:], 0.0) for s in S]

    a0 = [jnp.where(m16, a_ab[s], 0.0) for s in S]
    pw = [hmm(a0[s], bd(a0[s])) for s in S]
    tinv = [eye + a0[s] for s in S]
    for _ in range(2):
        ts = [_dot(jnp.concatenate([tinv[s], pw[s]], axis=0).astype(BF16), bd(pw[s])) for s in S]
        tinv = [tinv[s] + ts[s][:L] for s in S]
        pw = [ts[s][L:] for s in S]
    tinv = [tinv[s] + hmm(tinv[s], bd(pw[s])) for s in S]
    for msk in (m32, m64):
        inner = [hmm(jnp.where(msk, a_ab[s], 0.0), bd(tinv[s])) for s in S]
        tinv = [tinv[s] + hmm(tinv[s], bd(inner[s])) for s in S]

    vbd = [bd(v[s]) for s in S]
    avs = [_dot(jnp.concatenate([a_ak[s], a_rk[s]], axis=0).astype(BF16), vbd[s]) for s in S]
    av = [avs[s][:L] for s in S]
    tx = [_dot(tinv[s].astype(BF16), jnp.concatenate([bd(at[s]), bd(av[s])], axis=1)) for s in S]
    ahat = [tx[s][:, :W] for s in S]
    vp = [tx[s][:, W:] for s in S]
    ox = [_dot(a_rb[s].astype(BF16), jnp.concatenate([bd(ahat[s]), bd(vp[s])], axis=1)) for s in S]
    rhat = [rt[s] + ox[s][:, :W] for s in S]
    o_intra = [ox[s][:, W:] + avs[s][L:] for s in S]

    def state_terms(s):
        z = jnp.concatenate([bv[s] * e_end[s], k[s] * e_end[s]], axis=0).astype(BF16)
        wm = jnp.concatenate(
            [jnp.concatenate([ahat[s], vp[s]], axis=1),
             jnp.concatenate([jnp.zeros((L, W), F32), v[s]], axis=1)], axis=0).astype(BF16)
        mn = _dot_tn(z, wm)
        m_mat = jnp.where(bmask, mn[:, :W], 0.0) + jnp.where(diag, jnp.exp(cl_last[s]), 0.0)
        return m_mat, jnp.where(bmask, mn[:, W:], 0.0)

    mn = [state_terms(s) for s in S]

    o = [None] * len(streams)
    hq = [h_scr[q] for q in range(ngroups)]
    for j in range(chunks_per_step):
        for q in range(ngroups):
            s = streams.index((q, j))
            m_hi, m_lo = _split_bf16(mn[s][0])
            lhs = jnp.concatenate([m_hi, m_lo, rhat[s].astype(BF16)], axis=0)
            res = _dot(lhs, hq[q].astype(BF16))
            o[s] = res[2 * W:] + o_intra[s]
            hq[q] = res[:W] + res[W:2 * W] + mn[s][1]
    for q in range(ngroups):
        h_scr[q] = hq[q]

    def headmean(zs):
        parts = []
        for z in zs:
            parts += list(_split_bf16(z))
        red = _dot(jnp.concatenate(parts, axis=0), ones_bd) * (1.0 / HEAD)
        return [red[2 * L * s:2 * L * s + L] + red[2 * L * s + L:2 * L * (s + 1)] for s in S]

    mean = headmean(o)
    dlt = [o[s] - mean[s] for s in S]
    sq = jnp.concatenate([(dlt[s] * dlt[s]).astype(BF16) for s in S], axis=0)
    var_all = _dot(sq, ones_bd) * (1.0 / HEAD)
    var = [var_all[L * s:L * (s + 1)] for s in S]
    for s in S:
        q, j = streams[s]
        gsl = slice(q * W, (q + 1) * W)
        rsl = slice(j * L, (j + 1) * L)
        yn = dlt[s] * lax.rsqrt(var[s] + GN_EPS) * gnw_ref[:, gsl] + gnb_ref[:, gsl]
        o_ref[0, rsl, gsl] = ((yn + bonus_ref[0, rsl, gsl]) * g_ref[0, rsl, gsl]).astype(o_ref.dtype)


def _wkv_scan(r, lw, k, v, kk, bv, g, bonus, gn_w, gn_b, chunks_per_step):
    b, t, d = r.shape
    lb = CHUNK * chunks_per_step
    tok = pl.BlockSpec((1, lb, d), lambda bi, ci: (bi, ci, 0))
    vec = pl.BlockSpec((1, d), lambda bi, ci: (0, 0))
    return pl.pallas_call(
        functools.partial(_wkv_scan_kernel, chunks_per_step=chunks_per_step),
        grid=(b, t // lb),
        in_specs=[tok] * 8 + [vec, vec],
        out_specs=tok,
        out_shape=jax.ShapeDtypeStruct((b, t, d), BF16),
        scratch_shapes=[pltpu.VMEM((d // GROUP, GROUP, GROUP), F32)],
        compiler_params=pltpu.CompilerParams(
            dimension_semantics=("arbitrary", "arbitrary"), vmem_limit_bytes=VMEM_LIMIT),
        name="wkv_scan",
    )(r, lw, k, v, kk, bv, g, bonus, gn_w[None, :], gn_b[None, :])


def _swiglu(h, wg, wu, wd):
    g = _dot(h, wg)
    u = _dot(h, wu)
    return _dot((g * jax.nn.sigmoid(g) * u).astype(BF16), wd)


def _ffn_dense_kernel(a_ref, x_ref, gt1_ref, sh_ref, sc_ref, gt2_ref, ng_ref, wo_ref,
                      wg_ref, wu_ref, wd_ref, o_ref):
    x1 = x_ref[0] + (1.0 + gt1_ref[0]) * _dot(a_ref[0], wo_ref[...])
    h = _modulate(x1, ng_ref[...], sh_ref[0], sc_ref[0]).astype(BF16)
    o_ref[0] = x1 + (1.0 + gt2_ref[0]) * _swiglu(h, wg_ref[...], wu_ref[...], wd_ref[...])


def _resident(shape, index_map):
    return pl.BlockSpec(shape, index_map, pipeline_mode=pl.Buffered(1))


def _ffn_dense(a, x, gt1, sh, sc, gt2, ng, w_o, w_gu, w_d, tm, later_weights):
    b, t, d = x.shape
    f = w_d.shape[0]
    nt = t // tm
    tok = pl.BlockSpec((1, tm, d), lambda bi, ti: (bi, ti, 0))
    vec = pl.BlockSpec((1, 1, d), lambda bi, ti: (bi, 0, 0))
    cast_in, cast_specs, cast_shapes = _cast_specs(later_weights, b * nt, lambda bi, ti: bi * nt + ti)
    in_specs = [tok, tok, vec, vec, vec, vec, pl.BlockSpec((1, d), lambda bi, ti: (0, 0)),
                _resident((d, d), lambda bi, ti: (0, 0)),
                _resident((d, f), lambda bi, ti: (0, 0)),
                _resident((d, f), lambda bi, ti: (0, 1)),
                _resident((f, d), lambda bi, ti: (0, 0))]
    outs = pl.pallas_call(
        _with_casts(_ffn_dense_kernel, len(in_specs), 1, len(cast_in)),
        grid=(b, nt),
        in_specs=in_specs + cast_specs,
        out_specs=[tok] + cast_specs,
        out_shape=[jax.ShapeDtypeStruct((b, t, d), F32)] + cast_shapes,
        compiler_params=pltpu.CompilerParams(
            dimension_semantics=("arbitrary", "arbitrary"), vmem_limit_bytes=VMEM_LIMIT),
        name="ffn_dense",
    )(a, x, gt1[:, None, :], sh[:, None, :], sc[:, None, :], gt2[:, None, :], ng[None, :],
      w_o, w_gu, w_gu, w_d, *cast_in)
    return outs[0], [o.reshape(w.shape) for o, w in zip(outs[1:], later_weights)]


def _lru_kernel(x_ref, sh_ref, sc_ref, gt_ref, ng_ref, win_ref, cw_ref, cb_ref, wg_ref, bg_ref,
                lam_ref, wout_ref, o_ref, xbuf, abuf, bbuf, hbuf, ga, gb, carry):
    t = pl.program_id(1)
    tm = x_ref.shape[1]
    w = win_ref.shape[1] // 2
    nblk = w // LRU_BLOCK
    nslab = w // LANES
    ng = tm // 8

    @pl.when(t == 0)
    def _():
        xbuf[0:8, :] = jnp.zeros((8, w), F32)
        carry[...] = jnp.zeros_like(carry)
        ga[:, :, 0:ng, :] = jnp.ones((2, nslab, ng, LANES), F32)
        gb[:, :, 0:ng, :] = jnp.zeros((2, nslab, ng, LANES), F32)

    x = x_ref[0]
    h = _modulate(x, ng_ref[...], sh_ref[0], sc_ref[0]).astype(BF16)
    xg = _dot(h, win_ref[...])
    xb = xg[:, :w]
    gx = xg[:, w:]
    gate = 0.5 * gx * (1.0 + jnp.tanh(0.7978845608028654 * (gx + 0.044715 * gx * gx * gx)))

    xbuf[8:8 + tm, :] = xb
    conv = cb_ref[...] + cw_ref[CONV_WIDTH - 1:CONV_WIDTH, :] * xb
    for jj in range(CONV_WIDTH - 1):
        shift = CONV_WIDTH - 1 - jj
        conv = conv + cw_ref[jj:jj + 1, :] * xbuf[8 - shift:8 - shift + tm, :]
    xbuf[0:8, :] = xb[tm - 8:tm, :]

    conv_b = conv.astype(BF16)
    rs, is_ = [], []
    for n in range(nblk):
        gts = _dot(conv_b[:, n * LRU_BLOCK:(n + 1) * LRU_BLOCK], wg_ref[n]) + bg_ref[n]
        gts = jax.nn.sigmoid(gts)
        rs.append(gts[:, :LRU_BLOCK])
        is_.append(gts[:, LRU_BLOCK:])
    r_t = jnp.concatenate(rs, axis=1)
    i_t = jnp.concatenate(is_, axis=1)

    log_a = -LRU_C * r_t * _softplus(-lam_ref[...])
    a_t = jnp.exp(log_a)
    b_t = jnp.sqrt(-jnp.tanh(log_a) * (a_t * a_t + 1.0)) * (i_t * conv)

    for j in range(nslab):
        abuf[j] = a_t[:, j * LANES:(j + 1) * LANES]
        bbuf[j] = b_t[:, j * LANES:(j + 1) * LANES]
    hs_slabs = []
    for j in range(nslab):
        a_loc = [abuf[j, pl.ds(0, ng, stride=8), :]]
        b_loc = [bbuf[j, pl.ds(0, ng, stride=8), :]]
        for s in range(1, 8):
            a_s = abuf[j, pl.ds(s, ng, stride=8), :]
            b_s = bbuf[j, pl.ds(s, ng, stride=8), :]
            b_loc.append(a_s * b_loc[-1] + b_s)
            a_loc.append(a_s * a_loc[-1])
        ga[0, j, ng:2 * ng, :] = a_loc[-1]
        gb[0, j, ng:2 * ng, :] = b_loc[-1]
        step = 1
        src = 0
        while step < ng:
            a_cur = ga[src, j, ng:2 * ng, :]
            b_cur = gb[src, j, ng:2 * ng, :]
            a_sh = ga[src, j, ng - step:2 * ng - step, :]
            b_sh = gb[src, j, ng - step:2 * ng - step, :]
            ga[1 - src, j, ng:2 * ng, :] = a_cur * a_sh
            gb[1 - src, j, ng:2 * ng, :] = a_cur * b_sh + b_cur
            src = 1 - src
            step *= 2
        c_in = carry[0:1, j * LANES:(j + 1) * LANES]
        h_in = gb[src, j, ng - 1:2 * ng - 1, :] + ga[src, j, ng - 1:2 * ng - 1, :] * c_in
        for s in range(8):
            h_s = b_loc[s] + a_loc[s] * h_in
            hbuf[j, pl.ds(s, ng, stride=8), :] = h_s
        carry[:, j * LANES:(j + 1) * LANES] = jnp.broadcast_to(h_s[ng - 1:ng, :], (8, LANES))
        hs_slabs.append(hbuf[j])
    hs = jnp.concatenate(hs_slabs, axis=1)

    y = _dot((hs * gate).astype(BF16), wout_ref[...])
    o_ref[0] = x + (1.0 + gt_ref[0]) * y


def _lru_block(x, sh, sc, gt, ng, w_in, conv_w, conv_b, w_gates, b_gates, lam, w_out, tm,
               later_weights):
    b, t, d = x.shape
    w = w_out.shape[0]
    nblk = w // LRU_BLOCK
    nt = t // tm
    full = lambda *shape: _resident(shape, lambda bi, ti: (0,) * len(shape))
    tok = pl.BlockSpec((1, tm, d), lambda bi, ti: (bi, ti, 0))
    vec = pl.BlockSpec((1, 1, d), lambda bi, ti: (bi, 0, 0))
    cast_in, cast_specs, cast_shapes = _cast_specs(later_weights, b * nt, lambda bi, ti: bi * nt + ti)
    in_specs = [tok, vec, vec, vec, full(1, d), full(d, 2 * w), full(CONV_WIDTH, w), full(1, w),
                full(nblk, LRU_BLOCK, 2 * LRU_BLOCK), full(nblk, 1, 2 * LRU_BLOCK),
                full(1, w), full(w, d)]
    outs = pl.pallas_call(
        _with_casts(_lru_kernel, len(in_specs), 1, len(cast_in)),
        grid=(b, nt),
        in_specs=in_specs + cast_specs,
        out_specs=[tok] + cast_specs,
        out_shape=[jax.ShapeDtypeStruct((b, t, d), F32)] + cast_shapes,
        scratch_shapes=[pltpu.VMEM((tm + 8, w), F32)]
        + [pltpu.VMEM((w // LANES, tm, LANES), F32)] * 3
        + [pltpu.VMEM((2, w // LANES, tm // 4, LANES), F32)] * 2
        + [pltpu.VMEM((8, w), F32)],
        compiler_params=pltpu.CompilerParams(
            dimension_semantics=("arbitrary", "arbitrary"), vmem_limit_bytes=VMEM_LIMIT),
        name="rglru_block",
    )(x, sh[:, None, :], sc[:, None, :], gt[:, None, :], ng[None, :], w_in,
      conv_w, conv_b[None, :], w_gates, b_gates[:, None, :], lam[None, :], w_out, *cast_in)
    return outs[0], [o.reshape(wt.shape) for o, wt in zip(outs[1:], later_weights)]


def _router_kernel(x_ref, sh_ref, sc_ref, ng_ref, wr_ref, br_ref, tri_ref,
                   h_out, eid_out, rank_out, prob_out, cnt_out, cnt_scr):
    i = pl.program_id(0)

    @pl.when(i == 0)
    def _():
        cnt_scr[...] = jnp.zeros_like(cnt_scr)

    h = _modulate(x_ref[...], ng_ref[...], sh_ref[0], sc_ref[0])
    h_out[...] = h
    h_hi, h_lo = _split_bf16(h)
    w_hi, w_lo = _split_bf16(wr_ref[...])
    logits = _dot_nt(w_hi, h_hi) + _dot_nt(w_hi, h_lo) + _dot_nt(w_lo, h_hi) + br_ref[:, 0:1]

    n_rows = logits.shape[0]
    eidx = lax.broadcasted_iota(I32, logits.shape, 0)
    m1 = jnp.max(logits, axis=0, keepdims=True)
    i1 = jnp.min(jnp.where(logits == m1, eidx, n_rows), axis=0, keepdims=True)
    l2 = jnp.where(eidx == i1, -jnp.inf, logits)
    m2 = jnp.max(l2, axis=0, keepdims=True)
    i2 = jnp.min(jnp.where(l2 == m2, eidx, n_rows), axis=0, keepdims=True)
    e = jnp.exp(m2 - m1)
    p1 = 1.0 / (1.0 + e)
    p2 = e / (1.0 + e)

    oh1 = (eidx == i1).astype(F32)
    oh2 = (eidx == i2).astype(F32)
    oh = oh1 + oh2
    before = _dot(oh.astype(BF16), tri_ref[...]) + cnt_scr[:, 0:1]
    rank1 = jnp.sum(before * oh1, axis=0, keepdims=True)
    rank2 = jnp.sum(before * oh2, axis=0, keepdims=True)
    cnt_scr[...] = cnt_scr[...] + jnp.sum(oh, axis=1, keepdims=True)

    eid_out[0:1, :] = i1
    eid_out[1:2, :] = i2
    rank_out[0:1, :] = rank1.astype(I32)
    rank_out[1:2, :] = rank2.astype(I32)
    prob_out[0:1, :] = p1
    prob_out[1:2, :] = p2
    cnt_out[...] = cnt_scr[...].astype(I32)


def _router(x2d, sh, sc, ng, w_router, b_router, tiles_per_batch, tm):
    n, d = x2d.shape
    e = w_router.shape[1]
    ep = 16 * pl.cdiv(e, 16)
    wr = _pad_to(w_router.T, 0, ep)
    br = jnp.broadcast_to(jnp.concatenate([b_router, jnp.full((ep - e,), -1e30, F32)])[:, None],
                          (ep, LANES))
    tri = (jnp.arange(tm)[:, None] < jnp.arange(tm)[None, :]).astype(BF16)
    tok = pl.BlockSpec((tm, d), lambda i: (i, 0))
    vec = pl.BlockSpec((1, 1, d), lambda i: (i // tiles_per_batch, 0, 0))
    two = pl.BlockSpec((TOP_K, tm), lambda i: (0, i))
    const = lambda *shape: pl.BlockSpec(shape, lambda i: (0,) * len(shape))
    return pl.pallas_call(
        _router_kernel,
        grid=(n // tm,),
        in_specs=[tok, vec, vec, const(1, d), const(ep, d), const(ep, LANES), const(tm, tm)],
        out_specs=[tok, two, two, two, const(ep, LANES)],
        out_shape=[jax.ShapeDtypeStruct((n, d), F32), jax.ShapeDtypeStruct((TOP_K, n), I32),
                   jax.ShapeDtypeStruct((TOP_K, n), I32), jax.ShapeDtypeStruct((TOP_K, n), F32),
                   jax.ShapeDtypeStruct((ep, LANES), I32)],
        scratch_shapes=[pltpu.VMEM((ep, LANES), F32)],
        compiler_params=pltpu.CompilerParams(
            dimension_semantics=("arbitrary",), vmem_limit_bytes=VMEM_LIMIT),
        name="moe_router",
    )(x2d, sh[:, None, :], sc[:, None, :], ng[None, :], wr, br, tri)


def _moe_ffn_kernel(gid_ref, valid_ref, src0_ref, src_next_ref, dst_prev_ref, h_hbm,
                    wg_ref, wu_ref, wd_ref, ys_hbm, x0, x1, y0, y1, gsem, ssem,
                    *, n_tiles, tm, spare_row):
    del gid_ref
    i = pl.program_id(0)
    xbufs = (x0, x1)
    ybufs = (y0, y1)
    valid = valid_ref[jnp.minimum(i, n_tiles - 1)] > 0

    def gather(src_ref, xbuf):
        for t in range(tm):
            pltpu.make_async_copy(h_hbm.at[pl.ds(src_ref[0, 0, t], 1)], xbuf.at[pl.ds(t, 1)],
                                  gsem).start()

    def scatter(ybuf):
        for t in range(tm):
            pltpu.make_async_copy(ybuf.at[pl.ds(t, 1)],
                                  ys_hbm.at[pl.ds(dst_prev_ref[0, 0, t], 1)], ssem).start()

    def wait_gather(xbuf):
        pltpu.make_async_copy(h_hbm.at[pl.ds(0, tm)], xbuf, gsem).wait()

    def wait_scatter(ybuf):
        pltpu.make_async_copy(ybuf, ys_hbm.at[pl.ds(0, tm)], ssem).wait()

    @pl.when(i == 0)
    def _():
        y0[...] = jnp.zeros_like(y0)
        y1[...] = jnp.zeros_like(y1)
        pltpu.make_async_copy(y0, ys_hbm.at[pl.ds(spare_row, tm)], ssem).start()
        gather(src0_ref, x0)

    for par in range(2):
        for compute in (True, False):
            @pl.when((i < n_tiles) & (i % 2 == par) & (valid if compute else jnp.logical_not(valid)))
            def _(par=par, compute=compute):
                xa, xb, ya, yb = xbufs[par], xbufs[1 - par], ybufs[par], ybufs[1 - par]
                wait_gather(xa)
                wait_scatter(ya)
                gather(src_next_ref, xb)
                scatter(yb)
                if compute:
                    ya[...] = _swiglu(xa[...].astype(BF16), wg_ref[0], wu_ref[0], wd_ref[0])

    @pl.when(i == n_tiles)
    def _():
        par = n_tiles % 2
        wait_gather(xbufs[par])
        wait_scatter(ybufs[par])
        scatter(ybufs[1 - par])
        wait_scatter(ybufs[1 - par])


def _moe_ffn(gid, valid, src, dst, h2d, w_gu, w_d, out_rows, spare_row, tm):
    n_tiles = gid.shape[0]
    _, d = h2d.shape
    f = w_d.shape[1]
    idx = lambda fn: pl.BlockSpec((1, 1, tm), fn, memory_space=pltpu.SMEM)
    last = n_tiles - 1
    grid_spec = pltpu.PrefetchScalarGridSpec(
        num_scalar_prefetch=2,
        grid=(n_tiles + 1,),
        in_specs=[idx(lambda i, gid, vld: (0, 0, 0)),
                  idx(lambda i, gid, vld: (jnp.minimum(i + 1, n_tiles), 0, 0)),
                  idx(lambda i, gid, vld: (i, 0, 0)),
                  pl.BlockSpec(memory_space=pl.ANY),
                  _resident((1, d, f), lambda i, gid, vld: (gid[jnp.minimum(i, last)], 0, 0)),
                  _resident((1, d, f), lambda i, gid, vld: (gid[jnp.minimum(i, last)], 0, 1)),
                  _resident((1, f, d), lambda i, gid, vld: (gid[jnp.minimum(i, last)], 0, 0))],
        out_specs=pl.BlockSpec(memory_space=pl.ANY),
        scratch_shapes=[pltpu.VMEM((tm, d), F32)] * 4 + [pltpu.SemaphoreType.DMA(())] * 2,
    )
    return pl.pallas_call(
        functools.partial(_moe_ffn_kernel, n_tiles=n_tiles, tm=tm, spare_row=spare_row),
        grid_spec=grid_spec,
        out_shape=jax.ShapeDtypeStruct((out_rows, d), F32),
        compiler_params=pltpu.CompilerParams(
            dimension_semantics=("arbitrary",), vmem_limit_bytes=VMEM_LIMIT),
        name="moe_ffn",
    )(gid, valid, src, src, dst, h2d, w_gu, w_gu, w_d)


def _combine_kernel(y0_ref, y1_ref, x_ref, gt_ref, prob_ref, fg_ref, o_ref):
    y = prob_ref[:, 0:1] * y0_ref[...] + prob_ref[:, 1:2] * y1_ref[...]
    x = x_ref[...] + (1.0 + gt_ref[0]) * y
    ms = jnp.mean(x * x, axis=-1, keepdims=True)
    o_ref[...] = x * lax.rsqrt(ms + NORM_EPS) * fg_ref[...]


def _combine(ys, x2d, gt, prob, final_g, tiles_per_batch, tc):
    n, d = x2d.shape
    nt = n // tc
    tok = pl.BlockSpec((tc, d), lambda i: (i, 0))
    return pl.pallas_call(
        _combine_kernel,
        grid=(nt,),
        in_specs=[tok, pl.BlockSpec((tc, d), lambda i: (nt + i, 0)), tok,
                  pl.BlockSpec((1, 1, d), lambda i: (i // tiles_per_batch, 0, 0)),
                  pl.BlockSpec((tc, TOP_K), lambda i: (i, 0)),
                  pl.BlockSpec((1, d), lambda i: (0, 0))],
        out_specs=tok,
        out_shape=jax.ShapeDtypeStruct((n, d), F32),
        compiler_params=pltpu.CompilerParams(
            dimension_semantics=("arbitrary",), vmem_limit_bytes=VMEM_LIMIT),
        name="moe_combine",
    )(ys, ys, x2d, gt[:, None, :], prob, final_g[None, :])


def _moe_layout(eid, rank, counts, n_experts, tm, n_tiles):
    n = eid.shape[1]
    rows = n_tiles * tm
    cnt = counts[:n_experts, 0]
    tiles = (cnt + tm - 1) // tm
    tile_end = jnp.cumsum(tiles)
    offsets = (tile_end - tiles) * tm
    first_rank = jnp.cumsum(cnt) - cnt

    def per_expert(table, idx):
        out = jnp.zeros(idx.shape, table.dtype)
        for e in range(n_experts):
            out = jnp.where(idx == e, table[e], out)
        return out

    pos = (per_expert(offsets, eid) + rank).T.reshape(-1)
    order = jnp.argsort(pos).astype(I32)
    tile_ids = jnp.arange(n_tiles, dtype=I32)
    last_id = jnp.minimum(tile_ids, tile_end[-1] - 1)
    gid = jnp.sum((last_id[:, None] >= tile_end[None, :]).astype(I32), axis=1)
    p = jnp.arange(rows, dtype=I32)
    e_of_p = jnp.repeat(gid, tm)
    within = p - per_expert(offsets, e_of_p)
    valid = within < per_expert(cnt, e_of_p)
    flat = order[jnp.clip(per_expert(first_rank, e_of_p) + within, 0, TOP_K * n - 1)]
    tok = flat // TOP_K
    src = jnp.where(valid, tok, 0)
    pad_id = jnp.cumsum(jnp.logical_not(valid).astype(I32)) - 1
    dst = jnp.where(valid, (flat % TOP_K) * n + tok, TOP_K * n + pad_id)
    spare = rows
    src = jnp.concatenate([src, jnp.zeros((tm,), I32)]).reshape(n_tiles + 1, 1, tm)
    dst = jnp.concatenate([spare + jnp.arange(tm, dtype=I32), dst]).reshape(n_tiles + 1, 1, tm)
    tile_valid = (tile_ids < tile_end[-1]).astype(I32)
    return gid, tile_valid, src.astype(I32), dst.astype(I32), spare + 2 * tm, spare + tm


def kernel(x, c, ada_w, ada_b, norm_g, final_g, rwkv_mu, rwkv_w_rkv, rwkv_w_o, rwkv_w0, rwkv_w1, rwkv_w2, rwkv_a0, rwkv_a1, rwkv_a2, rwkv_g1, rwkv_g2, rwkv_k_k, rwkv_k_a, rwkv_r_k, rwkv_gn_w, rwkv_gn_b, lru_w_in, lru_conv_w, lru_conv_b, lru_w_gates, lru_b_gates, lru_lam, lru_w_out, ffn_w_gu, ffn_w_d, moe_w_router, moe_b_router, moe_w_gu, moe_w_d):
    b, t, d = x.shape
    n = b * t
    n_experts = moe_w_router.shape[-1]
    mod, (w_rkv_b,) = _ada_mod(c, ada_w, ada_b, [rwkv_w_rkv[0]])

    def mods(i):
        return [mod[i, :, q * d:(q + 1) * d] for q in range(6)]

    sh1, sc1, gt1, sh2, sc2, gt2 = mods(0)
    (r, lw, k2, v, kk, bv, g, bonus), (w_o_b, ffn_gu_b, ffn_d_b, lru_in_b, lru_gates_b, lru_out_b) = _rwkv_pre(
        x, sh1, sc1, norm_g[0, 0], rwkv_mu[0], w_rkv_b, rwkv_w1[0], rwkv_w2[0],
        rwkv_a1[0], rwkv_a2[0], rwkv_g1[0], rwkv_g2[0], rwkv_w0[0], rwkv_a0[0],
        rwkv_k_k[0], rwkv_k_a[0], rwkv_r_k[0], tm=min(256, t),
        later_weights=[rwkv_w_o[0], ffn_w_gu[0], ffn_w_d[0], lru_w_in[0], lru_w_gates[0], lru_w_out[0]])
    yg = _wkv_scan(r, lw, k2, v, kk, bv, g, bonus, rwkv_gn_w[0], rwkv_gn_b[0],
                   chunks_per_step=min(4, t // CHUNK))
    x, (moe_gu_b,) = _ffn_dense(yg, x, gt1, sh2, sc2, gt2, norm_g[0, 1], w_o_b, ffn_gu_b, ffn_d_b,
                                tm=min(256, t), later_weights=[moe_w_gu[0]])

    sh1, sc1, gt1, sh2, sc2, gt2 = mods(1)
    x, (moe_d_b,) = _lru_block(x, sh1, sc1, gt1, norm_g[1, 0], lru_in_b, lru_conv_w[0], lru_conv_b[0],
                               lru_gates_b, lru_b_gates[0], lru_lam[0], lru_out_b, tm=min(512, t),
                               later_weights=[moe_w_d[0]])

    x2d = x.reshape(n, d)
    tm_r = min(512, t)
    h2, eid, rank, prob, counts = _router(x2d, sh2, sc2, norm_g[1, 1], moe_w_router[0],
                                          moe_b_router[0], t // tm_r, tm_r)
    tm_g = min(256, t)
    n_tiles = (TOP_K * n) // tm_g + n_experts
    gid, tile_valid, src, dst, out_rows, spare_row = _moe_layout(eid, rank, counts, n_experts, tm_g, n_tiles)
    ys = _moe_ffn(gid, tile_valid, src, dst, h2, moe_gu_b, moe_d_b, out_rows, spare_row, tm_g)
    tc = min(512, t)
    out = _combine(ys, x2d, gt2, prob.T, final_g, t // tc, tc)
    return out.reshape(b, t, d)
```

```python
import functools

import jax
import jax.numpy as jnp
from jax import lax
from jax.experimental import pallas as pl
from jax.experimental.pallas import tpu as pltpu

F32 = jnp.float32
BF16 = jnp.bfloat16
I32 = jnp.int32

HEAD = 64
CHUNK = 64
GROUP = 256
HEADS_PER_GROUP = GROUP // HEAD
GN_EPS = 64e-5
NORM_EPS = 1e-6
LRU_C = 8.0
CONV_WIDTH = 4
LRU_BLOCK = 256
TOP_K = 2
LANES = 128
VMEM_LIMIT = 56 * 1024 * 1024


def _dot(a, b):
    return jnp.dot(a, b, preferred_element_type=F32)


def _dot_nt(a, b):
    return lax.dot_general(a, b, (((1,), (1,)), ((), ())), preferred_element_type=F32)


def _dot_tn(a, b):
    return lax.dot_general(a, b, (((0,), (0,)), ((), ())), preferred_element_type=F32)


def _softplus(u):
    return jnp.maximum(u, 0.0) + jnp.log1p(jnp.exp(-jnp.abs(u)))


def _modulate(x, ng, sh, sc):
    ms = jnp.mean(x * x, axis=-1, keepdims=True)
    return x * lax.rsqrt(ms + NORM_EPS) * ng * (1.0 + sc) + sh


def _split_bf16(x):
    hi = x.astype(BF16)
    lo = (x - hi.astype(F32)).astype(BF16)
    return hi, lo


def _cast_specs(weights, steps, step_index):
    ins, specs, out_shapes = [], [], []
    for w in weights:
        w2 = w.reshape(-1, w.shape[-1])
        rows = w2.shape[0]
        nblk = max(n for n in range(1, steps + 1) if rows % n == 0 and (rows // n) % 16 == 0)
        specs.append(pl.BlockSpec(
            (rows // nblk, w2.shape[1]),
            lambda *idx, nblk=nblk: (jnp.minimum(step_index(*idx), nblk - 1), 0)))
        ins.append(w2)
        out_shapes.append(jax.ShapeDtypeStruct(w2.shape, BF16))
    return ins, specs, out_shapes


def _with_casts(body, n_in, n_out, n_cast):
    def kernel(*refs):
        cast_in = refs[n_in:n_in + n_cast]
        outs = refs[n_in + n_cast:n_in + n_cast + n_out]
        cast_out = refs[n_in + n_cast + n_out:n_in + 2 * n_cast + n_out]
        for src, dst in zip(cast_in, cast_out):
            dst[...] = src[...].astype(BF16)
        body(*refs[:n_in], *outs, *refs[n_in + 2 * n_cast + n_out:])
    return kernel


def _shift_select(tm, shifts):
    rr = jnp.arange(tm)[:, None]
    cc = jnp.arange(tm)[None, :]
    return jnp.concatenate([(cc == rr - s) for s in shifts], axis=0).astype(BF16)


def _mod_kernel(c_ref, w_ref, b_ref, o_ref):
    c = c_ref[...]
    cond = c * jax.nn.sigmoid(c)
    o_ref[0] = _dot(cond.astype(BF16), w_ref[0].astype(BF16)) + b_ref[0]


def _ada_mod(c, ada_w, ada_b, later_weights):
    depth, d, d6 = ada_w.shape
    b = c.shape[0]
    rows = 8
    c8 = jnp.pad(c, ((0, rows - b), (0, 0)))
    tn = 1024
    nj = d6 // tn
    cast_in, cast_specs, cast_shapes = _cast_specs(later_weights, depth * nj, lambda i, j: i * nj + j)
    in_specs = [pl.BlockSpec((rows, d), lambda i, j: (0, 0)),
                pl.BlockSpec((1, d, tn), lambda i, j: (i, 0, j)),
                pl.BlockSpec((1, 1, tn), lambda i, j: (i, 0, j))]
    outs = pl.pallas_call(
        _with_casts(_mod_kernel, len(in_specs), 1, len(cast_in)),
        grid=(depth, nj),
        in_specs=in_specs + cast_specs,
        out_specs=[pl.BlockSpec((1, rows, tn), lambda i, j: (i, 0, j))] + cast_specs,
        out_shape=[jax.ShapeDtypeStruct((depth, rows, d6), F32)] + cast_shapes,
        compiler_params=pltpu.CompilerParams(dimension_semantics=("arbitrary", "arbitrary")),
        name="ada_mod",
    )(c8, ada_w, ada_b.reshape(depth, 1, d6), *cast_in)
    return outs[0][:, :b], [o.reshape(w.shape) for o, w in zip(outs[1:], later_weights)]


def _rwkv_pre_kernel(x_ref, sh_ref, sc_ref, ng_ref, mu_ref, wrkv_ref, w1_ref, w2_ref,
                     a1_ref, a2_ref, g1_ref, g2_ref, vec_ref, seg_ref, segt_ref, sel_ref,
                     r_out, lw_out, k_out, v_out, kk_out, b_out, g_out, bonus_out,
                     hbuf):
    t = pl.program_id(1)
    tm = x_ref.shape[1]
    d = x_ref.shape[2]

    h = _modulate(x_ref[0], ng_ref[...], sh_ref[0], sc_ref[0])

    @pl.when(t == 0)
    def _():
        hbuf[...] = jnp.zeros((8, d), F32)

    hprev = _dot(sel_ref[...], h.astype(BF16))
    first = lax.broadcasted_iota(I32, (tm, d), 0) == 0
    hprev = jnp.where(first, hbuf[7:8, :], hprev)
    hbuf[...] = h[tm - 8:tm, :]
    xx = hprev - h

    def mix(p):
        return (h + xx * mu_ref[p:p + 1, :]).astype(BF16)

    r = _dot(mix(0), wrkv_ref[0])
    k = _dot(mix(1), wrkv_ref[1])
    v = _dot(mix(2), wrkv_ref[2])
    wl = _dot(jnp.tanh(_dot(mix(3), w1_ref[...])).astype(BF16), w2_ref[...])
    al = _dot(_dot(mix(4), a1_ref[...]).astype(BF16), a2_ref[...])
    g = _dot(jax.nn.sigmoid(_dot(mix(5), g1_ref[...])).astype(BF16), g2_ref[...])

    w0 = vec_ref[0:1, :]
    a0 = vec_ref[1:2, :]
    k_k = vec_ref[2:3, :]
    k_a = vec_ref[3:4, :]
    r_k = vec_ref[4:5, :]

    def headsum(z):
        s = _dot(z.astype(BF16), seg_ref[...])
        s_hi, s_lo = _split_bf16(s)
        return _dot(jnp.concatenate([s_hi, s_lo], axis=1), segt_ref[...])

    lw = -0.6065306597126334 * jax.nn.sigmoid(w0 + wl)
    a = jax.nn.sigmoid(a0 + al)
    kk = k * k_k
    kk = kk * lax.rsqrt(jnp.maximum(headsum(kk * kk), 1e-24))
    k2 = k * (1.0 + (a - 1.0) * k_a)
    bonus = headsum(r * k2 * r_k) * v

    r_out[0] = r
    lw_out[0] = lw
    k_out[0] = k2
    v_out[0] = v
    kk_out[0] = kk
    b_out[0] = kk * a
    g_out[0] = g
    bonus_out[0] = bonus


def _pad_to(x, axis, size):
    pad = [(0, 0)] * x.ndim
    pad[axis] = (0, size - x.shape[axis])
    return jnp.pad(x, pad)


def _rwkv_pre(x, sh, sc, ng, mu, w_rkv, w1, w2, a1, a2, g1, g2, w0, a0, k_k, k_a, r_k, tm,
              later_weights):
    b, t, d = x.shape
    nh = d // HEAD
    lw_pad = LANES * pl.cdiv(w1.shape[1], LANES)
    la_pad = LANES * pl.cdiv(a1.shape[1], LANES)
    lg_pad = LANES * pl.cdiv(g1.shape[1], LANES)
    vecs = _pad_to(jnp.stack([w0, a0, k_k, k_a, r_k.reshape(d)]), 0, 8)
    head_of_lane = jnp.arange(d) // HEAD
    seg = (head_of_lane[:, None] == jnp.arange(LANES)[None, :]).astype(BF16)
    segt = jnp.concatenate([seg.T, seg.T], axis=0)
    del nh
    full = lambda *shape: _resident(shape, lambda bi, ti: (0,) * len(shape))
    tok = pl.BlockSpec((1, tm, d), lambda bi, ti: (bi, ti, 0))
    vec = pl.BlockSpec((1, 1, d), lambda bi, ti: (bi, 0, 0))
    nt = t // tm
    cast_in, cast_specs, cast_shapes = _cast_specs(later_weights, b * nt, lambda bi, ti: bi * nt + ti)
    in_specs = [tok, vec, vec, full(1, d), full(8, d), full(3, d, d),
                full(d, lw_pad), full(lw_pad, d), full(d, la_pad), full(la_pad, d),
                full(d, lg_pad), full(lg_pad, d), full(8, d), full(d, LANES), full(2 * LANES, d),
                full(tm, tm)]
    outs = pl.pallas_call(
        _with_casts(_rwkv_pre_kernel, len(in_specs), 8, len(cast_in)),
        grid=(b, nt),
        in_specs=in_specs + cast_specs,
        out_specs=[tok] * 8 + cast_specs,
        out_shape=[jax.ShapeDtypeStruct((b, t, d), F32)] * 8 + cast_shapes,
        scratch_shapes=[pltpu.VMEM((8, d), F32)],
        compiler_params=pltpu.CompilerParams(
            dimension_semantics=("arbitrary", "arbitrary"), vmem_limit_bytes=VMEM_LIMIT),
        name="rwkv_pre",
    )(x, sh[:, None, :], sc[:, None, :], ng[None, :], _pad_to(mu, 0, 8), w_rkv.astype(BF16),
      _pad_to(w1, 1, lw_pad).astype(BF16), _pad_to(w2, 0, lw_pad).astype(BF16),
      _pad_to(a1, 1, la_pad).astype(BF16), _pad_to(a2, 0, la_pad).astype(BF16),
      _pad_to(g1, 1, lg_pad).astype(BF16), _pad_to(g2, 0, lg_pad).astype(BF16),
      vecs, seg, segt, _shift_select(tm, [1]), *cast_in)
    casts = [o.reshape(w.shape) for o, w in zip(outs[8:], later_weights)]
    return outs[:8], casts


def _wkv_scan_kernel(r_ref, lw_ref, k_ref, v_ref, kk_ref, b_ref, g_ref, bonus_ref,
                     gnw_ref, gnb_ref, o_ref, h_scr, *, chunks_per_step):
    L = CHUNK
    W = GROUP
    ngroups = r_ref.shape[2] // W

    @pl.when(pl.program_id(1) == 0)
    def _():
        h_scr[...] = jnp.zeros_like(h_scr)

    row = lax.broadcasted_iota(I32, (L, W), 0)
    lane = lax.broadcasted_iota(I32, (L, W), 1)
    sidx = lane & (L - 1)
    lane_head = lane >> 6
    strict = sidx < row
    incl = sidx <= row
    eye = (sidx == row).astype(F32)
    same16 = (row >> 4) == (sidx >> 4)
    same32 = (row >> 5) == (sidx >> 5)
    m16 = strict & same16
    m32 = strict & same32 & jnp.logical_not(same16)
    m64 = strict & jnp.logical_not(same32)
    rb = lax.broadcasted_iota(I32, (W, W), 0)
    cb = lax.broadcasted_iota(I32, (W, W), 1)
    bmask = (rb >> 6) == (cb >> 6)
    diag = rb == cb
    ones_bd = bmask.astype(BF16)
    tri_r = lax.broadcasted_iota(I32, (L, 2 * L), 0)
    tri_c = lax.broadcasted_iota(I32, (L, 2 * L), 1)
    tri2 = ((tri_c & (L - 1)) <= tri_r).astype(BF16)

    def bd(y):
        yt = jnp.concatenate([y] * HEADS_PER_GROUP, axis=0)
        return jnp.where(bmask, yt, 0.0).astype(BF16)

    def hmm(x, ybd):
        return _dot(x.astype(BF16), ybd)

    streams = [(q, j) for j in range(chunks_per_step) for q in range(ngroups)]
    S = range(len(streams))

    def ld(ref, s):
        q, j = streams[s]
        return ref[0, j * L:(j + 1) * L, q * W:(q + 1) * W]

    r = [ld(r_ref, s) for s in S]
    lw = [ld(lw_ref, s) for s in S]
    k = [ld(k_ref, s) for s in S]
    v = [ld(v_ref, s) for s in S]
    kk = [ld(kk_ref, s) for s in S]
    bv = [ld(b_ref, s) for s in S]

    def cumlog(x):
        hi, lo = _split_bf16(x)
        return _dot(tri2, jnp.concatenate([hi, lo], axis=0))

    cl = [cumlog(lw[s]) for s in S]
    cl_last = [cl[s][L - 1:L, :] for s in S]
    e_pos = [jnp.exp(cl[s]) for s in S]
    e_neg = [jnp.exp(-cl[s]) for s in S]
    e_end = [jnp.exp(cl_last[s] - cl[s]) for s in S]
    rt = [r[s] * e_pos[s] for s in S]
    at = [-kk[s] * jnp.exp(cl[s] - lw[s]) for s in S]
    bt = [bv[s] * e_neg[s] for s in S]
    kt = [k[s] * e_neg[s] for s in S]

    def gram(s):
        x = jnp.concatenate([at[s], rt[s]], axis=0).astype(BF16)
        ys = [jnp.where(lane_head == hh, bt[s], 0.0) for hh in range(HEADS_PER_GROUP)]
        ys += [jnp.where(lane_head == hh, kt[s], 0.0) for hh in range(HEADS_PER_GROUP)]
        return _dot_nt(x, jnp.concatenate(ys, axis=0).astype(BF16))

    gm = [gram(s) for s in S]
    a_ab = [jnp.where(strict, gm[s][:L, :W], 0.0) for s in S]
    a_ak = [jnp.where(strict, gm[s][:L, W:], 0.0) for s in S]
    a_rb = [jnp.where(incl, gm[s][L:, :W], 0.0) for s in S]
    a_rk = [jnp.where(incl, gm[s][L:, W:], 0.0) for s in S]

    a0 = [jnp.where(m16, a_ab[s], 0.0) for s in S]
    pw = [hmm(a0[s], bd(a0[s])) for s in S]
    tinv = [eye + a0[s] for s in S]
    for _ in range(2):
        ts = [_dot(jnp.concatenate([tinv[s], pw[s]], axis=0).astype(BF16), bd(pw[s])) for s in S]
        tinv = [tinv[s] + ts[s][:L] for s in S]
        pw = [ts[s][L:] for s in S]
    tinv = [tinv[s] + hmm(tinv[s], bd(pw[s])) for s in S]
    for msk in (m32, m64):
        inner = [hmm(jnp.where(msk, a_ab[s], 0.0), bd(tinv[s])) for s in S]
        tinv = [tinv[s] + hmm(tinv[s], bd(inner[s])) for s in S]

    vbd = [bd(v[s]) for s in S]
    avs = [_dot(jnp.concatenate([a_ak[s], a_rk[s]], axis=0).astype(BF16), vbd[s]) for s in S]
    av = [avs[s][:L] for s in S]
    tx = [_dot(tinv[s].astype(BF16), jnp.concatenate([bd(at[s]), bd(av[s])], axis=1)) for s in S]
    ahat = [tx[s][:, :W] for s in S]
    vp = [tx[s][:, W:] for s in S]
    ox = [_dot(a_rb[s].astype(BF16), jnp.concatenate([bd(ahat[s]), bd(vp[s])], axis=1)) for s in S]
    rhat = [rt[s] + ox[s][:, :W] for s in S]
    o_intra = [ox[s][:, W:] + avs[s][L:] for s in S]

    def state_terms(s):
        z = jnp.concatenate([bv[s] * e_end[s], k[s] * e_end[s]], axis=0).astype(BF16)
        wm = jnp.concatenate(
            [jnp.concatenate([ahat[s], vp[s]], axis=1),
             jnp.concatenate([jnp.zeros((L, W), F32), v[s]], axis=1)], axis=0).astype(BF16)
        mn = _dot_tn(z, wm)
        m_mat = jnp.where(bmask, mn[:, :W], 0.0) + jnp.where(diag, jnp.exp(cl_last[s]), 0.0)
        return m_mat, jnp.where(bmask, mn[:, W:], 0.0)

    mn = [state_terms(s) for s in S]

    o = [None] * len(streams)
    hq = [h_scr[q] for q in range(ngroups)]
    for j in range(chunks_per_step):
        for q in range(ngroups):
            s = streams.index((q, j))
            m_hi, m_lo = _split_bf16(mn[s][0])
            lhs = jnp.concatenate([m_hi, m_lo, rhat[s].astype(BF16)], axis=0)
            res = _dot(lhs, hq[q].astype(BF16))
            o[s] = res[2 * W:] + o_intra[s]
            hq[q] = res[:W] + res[W:2 * W] + mn[s][1]
    for q in range(ngroups):
        h_scr[q] = hq[q]

    def headmean(zs):
        parts = []
        for z in zs:
            parts += list(_split_bf16(z))
        red = _dot(jnp.concatenate(parts, axis=0), ones_bd) * (1.0 / HEAD)
        return [red[2 * L * s:2 * L * s + L] + red[2 * L * s + L:2 * L * (s + 1)] for s in S]

    mean = headmean(o)
    dlt = [o[s] - mean[s] for s in S]
    sq = jnp.concatenate([(dlt[s] * dlt[s]).astype(BF16) for s in S], axis=0)
    var_all = _dot(sq, ones_bd) * (1.0 / HEAD)
    var = [var_all[L * s:L * (s + 1)] for s in S]
    for s in S:
        q, j = streams[s]
        gsl = slice(q * W, (q + 1) * W)
        rsl = slice(j * L, (j + 1) * L)
        yn = dlt[s] * lax.rsqrt(var[s] + GN_EPS) * gnw_ref[:, gsl] + gnb_ref[:, gsl]
        o_ref[0, rsl, gsl] = ((yn + bonus_ref[0, rsl, gsl]) * g_ref[0, rsl, gsl]).astype(o_ref.dtype)


def _wkv_scan(r, lw, k, v, kk, bv, g, bonus, gn_w, gn_b, chunks_per_step):
    b, t, d = r.shape
    lb = CHUNK * chunks_per_step
    tok = pl.BlockSpec((1, lb, d), lambda bi, ci: (bi, ci, 0))
    vec = pl.BlockSpec((1, d), lambda bi, ci: (0, 0))
    return pl.pallas_call(
        functools.partial(_wkv_scan_kernel, chunks_per_step=chunks_per_step),
        grid=(b, t // lb),
        in_specs=[tok] * 8 + [vec, vec],
        out_specs=tok,
        out_shape=jax.ShapeDtypeStruct((b, t, d), BF16),
        scratch_shapes=[pltpu.VMEM((d // GROUP, GROUP, GROUP), F32)],
        compiler_params=pltpu.CompilerParams(
            dimension_semantics=("arbitrary", "arbitrary"), vmem_limit_bytes=VMEM_LIMIT),
        name="wkv_scan",
    )(r, lw, k, v, kk, bv, g, bonus, gn_w[None, :], gn_b[None, :])


def _swiglu(h, wg, wu, wd):
    g = _dot(h, wg)
    u = _dot(h, wu)
    return _dot((g * jax.nn.sigmoid(g) * u).astype(BF16), wd)


def _ffn_dense_kernel(a_ref, x_ref, gt1_ref, sh_ref, sc_ref, gt2_ref, ng_ref, wo_ref,
                      wg_ref, wu_ref, wd_ref, o_ref):
    x1 = x_ref[0] + (1.0 + gt1_ref[0]) * _dot(a_ref[0], wo_ref[...])
    h = _modulate(x1, ng_ref[...], sh_ref[0], sc_ref[0]).astype(BF16)
    o_ref[0] = x1 + (1.0 + gt2_ref[0]) * _swiglu(h, wg_ref[...], wu_ref[...], wd_ref[...])


def _resident(shape, index_map):
    return pl.BlockSpec(shape, index_map, pipeline_mode=pl.Buffered(1))


def _ffn_dense(a, x, gt1, sh, sc, gt2, ng, w_o, w_gu, w_d, tm, later_weights):
    b, t, d = x.shape
    f = w_d.shape[0]
    nt = t // tm
    tok = pl.BlockSpec((1, tm, d), lambda bi, ti: (bi, ti, 0))
    vec = pl.BlockSpec((1, 1, d), lambda bi, ti: (bi, 0, 0))
    cast_in, cast_specs, cast_shapes = _cast_specs(later_weights, b * nt, lambda bi, ti: bi * nt + ti)
    in_specs = [tok, tok, vec, vec, vec, vec, pl.BlockSpec((1, d), lambda bi, ti: (0, 0)),
                _resident((d, d), lambda bi, ti: (0, 0)),
                _resident((d, f), lambda bi, ti: (0, 0)),
                _resident((d, f), lambda bi, ti: (0, 1)),
                _resident((f, d), lambda bi, ti: (0, 0))]
    outs = pl.pallas_call(
        _with_casts(_ffn_dense_kernel, len(in_specs), 1, len(cast_in)),
        grid=(b, nt),
        in_specs=in_specs + cast_specs,
        out_specs=[tok] + cast_specs,
        out_shape=[jax.ShapeDtypeStruct((b, t, d), F32)] + cast_shapes,
        compiler_params=pltpu.CompilerParams(
            dimension_semantics=("arbitrary", "arbitrary"), vmem_limit_bytes=VMEM_LIMIT),
        name="ffn_dense",
    )(a, x, gt1[:, None, :], sh[:, None, :], sc[:, None, :], gt2[:, None, :], ng[None, :],
      w_o, w_gu, w_gu, w_d, *cast_in)
    return outs[0], [o.reshape(w.shape) for o, w in zip(outs[1:], later_weights)]


def _lru_kernel(x_ref, sh_ref, sc_ref, gt_ref, ng_ref, win_ref, cw_ref, cb_ref, wg_ref, bg_ref,
                lam_ref, wout_ref, o_ref, xbuf, abuf, bbuf, hbuf, ga, gb, carry):
    t = pl.program_id(1)
    tm = x_ref.shape[1]
    w = win_ref.shape[1] // 2
    nblk = w // LRU_BLOCK
    nslab = w // LANES
    ng = tm // 8

    @pl.when(t == 0)
    def _():
        xbuf[0:8, :] = jnp.zeros((8, w), F32)
        carry[...] = jnp.zeros_like(carry)
        ga[:, :, 0:ng, :] = jnp.ones((2, nslab, ng, LANES), F32)
        gb[:, :, 0:ng, :] = jnp.zeros((2, nslab, ng, LANES), F32)

    x = x_ref[0]
    h = _modulate(x, ng_ref[...], sh_ref[0], sc_ref[0]).astype(BF16)
    xg = _dot(h, win_ref[...])
    xb = xg[:, :w]
    gx = xg[:, w:]
    gate = 0.5 * gx * (1.0 + jnp.tanh(0.7978845608028654 * (gx + 0.044715 * gx * gx * gx)))

    xbuf[8:8 + tm, :] = xb
    conv = cb_ref[...] + cw_ref[CONV_WIDTH - 1:CONV_WIDTH, :] * xb
    for jj in range(CONV_WIDTH - 1):
        shift = CONV_WIDTH - 1 - jj
        conv = conv + cw_ref[jj:jj + 1, :] * xbuf[8 - shift:8 - shift + tm, :]
    xbuf[0:8, :] = xb[tm - 8:tm, :]

    conv_b = conv.astype(BF16)
    rs, is_ = [], []
    for n in range(nblk):
        gts = _dot(conv_b[:, n * LRU_BLOCK:(n + 1) * LRU_BLOCK], wg_ref[n]) + bg_ref[n]
        gts = jax.nn.sigmoid(gts)
        rs.append(gts[:, :LRU_BLOCK])
        is_.append(gts[:, LRU_BLOCK:])
    r_t = jnp.concatenate(rs, axis=1)
    i_t = jnp.concatenate(is_, axis=1)

    log_a = -LRU_C * r_t * _softplus(-lam_ref[...])
    a_t = jnp.exp(log_a)
    b_t = jnp.sqrt(-jnp.tanh(log_a) * (a_t * a_t + 1.0)) * (i_t * conv)

    for j in range(nslab):
        abuf[j] = a_t[:, j * LANES:(j + 1) * LANES]
        bbuf[j] = b_t[:, j * LANES:(j + 1) * LANES]
    hs_slabs = []
    for j in range(nslab):
        a_loc = [abuf[j, pl.ds(0, ng, stride=8), :]]
        b_loc = [bbuf[j, pl.ds(0, ng, stride=8), :]]
        for s in range(1, 8):
            a_s = abuf[j, pl.ds(s, ng, stride=8), :]
            b_s = bbuf[j, pl.ds(s, ng, stride=8), :]
            b_loc.append(a_s * b_loc[-1] + b_s)
            a_loc.append(a_s * a_loc[-1])
        ga[0, j, ng:2 * ng, :] = a_loc[-1]
        gb[0, j, ng:2 * ng, :] = b_loc[-1]
        step = 1
        src = 0
        while step < ng:
            a_cur = ga[src, j, ng:2 * ng, :]
            b_cur = gb[src, j, ng:2 * ng, :]
            a_sh = ga[src, j, ng - step:2 * ng - step, :]
            b_sh = gb[src, j, ng - step:2 * ng - step, :]
            ga[1 - src, j, ng:2 * ng, :] = a_cur * a_sh
            gb[1 - src, j, ng:2 * ng, :] = a_cur * b_sh + b_cur
            src = 1 - src
            step *= 2
        c_in = carry[0:1, j * LANES:(j + 1) * LANES]
        h_in = gb[src, j, ng - 1:2 * ng - 1, :] + ga[src, j, ng - 1:2 * ng - 1, :] * c_in
        for s in range(8):
            h_s = b_loc[s] + a_loc[s] * h_in
            hbuf[j, pl.ds(s, ng, stride=8), :] = h_s
        carry[:, j * LANES:(j + 1) * LANES] = jnp.broadcast_to(h_s[ng - 1:ng, :], (8, LANES))
        hs_slabs.append(hbuf[j])
    hs = jnp.concatenate(hs_slabs, axis=1)

    y = _dot((hs * gate).astype(BF16), wout_ref[...])
    o_ref[0] = x + (1.0 + gt_ref[0]) * y


def _lru_block(x, sh, sc, gt, ng, w_in, conv_w, conv_b, w_gates, b_gates, lam, w_out, tm,
               later_weights):
    b, t, d = x.shape
    w = w_out.shape[0]
    nblk = w // LRU_BLOCK
    nt = t // tm
    full = lambda *shape: _resident(shape, lambda bi, ti: (0,) * len(shape))
    tok = pl.BlockSpec((1, tm, d), lambda bi, ti: (bi, ti, 0))
    vec = pl.BlockSpec((1, 1, d), lambda bi, ti: (bi, 0, 0))
    cast_in, cast_specs, cast_shapes = _cast_specs(later_weights, b * nt, lambda bi, ti: bi * nt + ti)
    in_specs = [tok, vec, vec, vec, full(1, d), full(d, 2 * w), full(CONV_WIDTH, w), full(1, w),
                full(nblk, LRU_BLOCK, 2 * LRU_BLOCK), full(nblk, 1, 2 * LRU_BLOCK),
                full(1, w), full(w, d)]
    outs = pl.pallas_call(
        _with_casts(_lru_kernel, len(in_specs), 1, len(cast_in)),
        grid=(b, nt),
        in_specs=in_specs + cast_specs,
        out_specs=[tok] + cast_specs,
        out_shape=[jax.ShapeDtypeStruct((b, t, d), F32)] + cast_shapes,
        scratch_shapes=[pltpu.VMEM((tm + 8, w), F32)]
        + [pltpu.VMEM((w // LANES, tm, LANES), F32)] * 3
        + [pltpu.VMEM((2, w // LANES, tm // 4, LANES), F32)] * 2
        + [pltpu.VMEM((8, w), F32)],
        compiler_params=pltpu.CompilerParams(
            dimension_semantics=("arbitrary", "arbitrary"), vmem_limit_bytes=VMEM_LIMIT),
        name="rglru_block",
    )(x, sh[:, None, :], sc[:, None, :], gt[:, None, :], ng[None, :], w_in,
      conv_w, conv_b[None, :], w_gates, b_gates[:, None, :], lam[None, :], w_out, *cast_in)
    return outs[0], [o.reshape(wt.shape) for o, wt in zip(outs[1:], later_weights)]


def _router_kernel(x_ref, sh_ref, sc_ref, ng_ref, wr_ref, br_ref, tri_ref,
                   h_out, eid_out, rank_out, prob_out, cnt_out, cnt_scr):
    i = pl.program_id(0)

    @pl.when(i == 0)
    def _():
        cnt_scr[...] = jnp.zeros_like(cnt_scr)

    h = _modulate(x_ref[...], ng_ref[...], sh_ref[0], sc_ref[0])
    h_out[...] = h
    h_hi, h_lo = _split_bf16(h)
    w_hi, w_lo = _split_bf16(wr_ref[...])
    logits = _dot_nt(w_hi, h_hi) + _dot_nt(w_hi, h_lo) + _dot_nt(w_lo, h_hi) + br_ref[:, 0:1]

    n_rows = logits.shape[0]
    eidx = lax.broadcasted_iota(I32, logits.shape, 0)
    m1 = jnp.max(logits, axis=0, keepdims=True)
    i1 = jnp.min(jnp.where(logits == m1, eidx, n_rows), axis=0, keepdims=True)
    l2 = jnp.where(eidx == i1, -jnp.inf, logits)
    m2 = jnp.max(l2, axis=0, keepdims=True)
    i2 = jnp.min(jnp.where(l2 == m2, eidx, n_rows), axis=0, keepdims=True)
    e = jnp.exp(m2 - m1)
    p1 = 1.0 / (1.0 + e)
    p2 = e / (1.0 + e)

    oh1 = (eidx == i1).astype(F32)
    oh2 = (eidx == i2).astype(F32)
    oh = oh1 + oh2
    before = _dot(oh.astype(BF16), tri_ref[...]) + cnt_scr[:, 0:1]
    rank1 = jnp.sum(before * oh1, axis=0, keepdims=True)
    rank2 = jnp.sum(before * oh2, axis=0, keepdims=True)
    cnt_scr[...] = cnt_scr[...] + jnp.sum(oh, axis=1, keepdims=True)

    eid_out[0:1, :] = i1
    eid_out[1:2, :] = i2
    rank_out[0:1, :] = rank1.astype(I32)
    rank_out[1:2, :] = rank2.astype(I32)
    prob_out[0:1, :] = p1
    prob_out[1:2, :] = p2
    cnt_out[...] = cnt_scr[...].astype(I32)


def _router(x2d, sh, sc, ng, w_router, b_router, tiles_per_batch, tm):
    n, d = x2d.shape
    e = w_router.shape[1]
    ep = 16 * pl.cdiv(e, 16)
    wr = _pad_to(w_router.T, 0, ep)
    br = jnp.broadcast_to(jnp.concatenate([b_router, jnp.full((ep - e,), -1e30, F32)])[:, None],
                          (ep, LANES))
    tri = (jnp.arange(tm)[:, None] < jnp.arange(tm)[None, :]).astype(BF16)
    tok = pl.BlockSpec((tm, d), lambda i: (i, 0))
    vec = pl.BlockSpec((1, 1, d), lambda i: (i // tiles_per_batch, 0, 0))
    two = pl.BlockSpec((TOP_K, tm), lambda i: (0, i))
    const = lambda *shape: _resident(shape, lambda i: (0,) * len(shape))
    return pl.pallas_call(
        _router_kernel,
        grid=(n // tm,),
        in_specs=[tok, vec, vec, const(1, d), const(ep, d), const(ep, LANES), const(tm, tm)],
        out_specs=[tok, two, two, two, pl.BlockSpec((ep, LANES), lambda i: (0, 0))],
        out_shape=[jax.ShapeDtypeStruct((n, d), F32), jax.ShapeDtypeStruct((TOP_K, n), I32),
                   jax.ShapeDtypeStruct((TOP_K, n), I32), jax.ShapeDtypeStruct((TOP_K, n), F32),
                   jax.ShapeDtypeStruct((ep, LANES), I32)],
        scratch_shapes=[pltpu.VMEM((ep, LANES), F32)],
        compiler_params=pltpu.CompilerParams(
            dimension_semantics=("arbitrary",), vmem_limit_bytes=VMEM_LIMIT),
        name="moe_router",
    )(x2d, sh[:, None, :], sc[:, None, :], ng[None, :], wr, br, tri)


def _moe_ffn_kernel(gid_ref, valid_ref, src0_ref, src_next_ref, dst_prev_ref, h_hbm,
                    wg_ref, wu_ref, wd_ref, ys_hbm, x0, x1, y0, y1, gsem, ssem,
                    *, n_tiles, tm, spare_row):
    del gid_ref
    i = pl.program_id(0)
    xbufs = (x0, x1)
    ybufs = (y0, y1)
    valid = valid_ref[jnp.minimum(i, n_tiles - 1)] > 0

    def gather(src_ref, xbuf):
        for t in range(tm):
            pltpu.make_async_copy(h_hbm.at[pl.ds(src_ref[0, 0, t], 1)], xbuf.at[pl.ds(t, 1)],
                                  gsem).start()

    def scatter(ybuf):
        for t in range(tm):
            pltpu.make_async_copy(ybuf.at[pl.ds(t, 1)],
                                  ys_hbm.at[pl.ds(dst_prev_ref[0, 0, t], 1)], ssem).start()

    def wait_gather(xbuf):
        pltpu.make_async_copy(h_hbm.at[pl.ds(0, tm)], xbuf, gsem).wait()

    def wait_scatter(ybuf):
        pltpu.make_async_copy(ybuf, ys_hbm.at[pl.ds(0, tm)], ssem).wait()

    @pl.when(i == 0)
    def _():
        y0[...] = jnp.zeros_like(y0)
        y1[...] = jnp.zeros_like(y1)
        pltpu.make_async_copy(y0, ys_hbm.at[pl.ds(spare_row, tm)], ssem).start()
        gather(src0_ref, x0)

    for par in range(2):
        for compute in (True, False):
            @pl.when((i < n_tiles) & (i % 2 == par) & (valid if compute else jnp.logical_not(valid)))
            def _(par=par, compute=compute):
                xa, xb, ya, yb = xbufs[par], xbufs[1 - par], ybufs[par], ybufs[1 - par]
                wait_gather(xa)
                wait_scatter(ya)
                gather(src_next_ref, xb)
                scatter(yb)
                if compute:
                    ya[...] = _swiglu(xa[...].astype(BF16), wg_ref[0], wu_ref[0], wd_ref[0])

    @pl.when(i == n_tiles)
    def _():
        par = n_tiles % 2
        wait_gather(xbufs[par])
        wait_scatter(ybufs[par])
        scatter(ybufs[1 - par])
        wait_scatter(ybufs[1 - par])


def _moe_ffn(gid, valid, src, dst, h2d, w_gu, w_d, out_rows, spare_row, tm):
    n_tiles = gid.shape[0]
    _, d = h2d.shape
    f = w_d.shape[1]
    idx = lambda fn: pl.BlockSpec((1, 1, tm), fn, memory_space=pltpu.SMEM)
    last = n_tiles - 1
    grid_spec = pltpu.PrefetchScalarGridSpec(
        num_scalar_prefetch=2,
        grid=(n_tiles + 1,),
        in_specs=[idx(lambda i, gid, vld: (0, 0, 0)),
                  idx(lambda i, gid, vld: (jnp.minimum(i + 1, n_tiles), 0, 0)),
                  idx(lambda i, gid, vld: (i, 0, 0)),
                  pl.BlockSpec(memory_space=pl.ANY),
                  _resident((1, d, f), lambda i, gid, vld: (gid[jnp.minimum(i, last)], 0, 0)),
                  _resident((1, d, f), lambda i, gid, vld: (gid[jnp.minimum(i, last)], 0, 1)),
                  _resident((1, f, d), lambda i, gid, vld: (gid[jnp.minimum(i, last)], 0, 0))],
        out_specs=pl.BlockSpec(memory_space=pl.ANY),
        scratch_shapes=[pltpu.VMEM((tm, d), F32)] * 4 + [pltpu.SemaphoreType.DMA(())] * 2,
    )
    return pl.pallas_call(
        functools.partial(_moe_ffn_kernel, n_tiles=n_tiles, tm=tm, spare_row=spare_row),
        grid_spec=grid_spec,
        out_shape=jax.ShapeDtypeStruct((out_rows, d), F32),
        compiler_params=pltpu.CompilerParams(
            dimension_semantics=("arbitrary",), vmem_limit_bytes=VMEM_LIMIT),
        name="moe_ffn",
    )(gid, valid, src, src, dst, h2d, w_gu, w_gu, w_d)


def _combine_kernel(y0_ref, y1_ref, x_ref, gt_ref, prob_ref, fg_ref, o_ref):
    y = prob_ref[:, 0:1] * y0_ref[...] + prob_ref[:, 1:2] * y1_ref[...]
    x = x_ref[...] + (1.0 + gt_ref[0]) * y
    ms = jnp.mean(x * x, axis=-1, keepdims=True)
    o_ref[...] = x * lax.rsqrt(ms + NORM_EPS) * fg_ref[...]


def _combine(ys, x2d, gt, prob, final_g, tiles_per_batch, tc):
    n, d = x2d.shape
    nt = n // tc
    tok = pl.BlockSpec((tc, d), lambda i: (i, 0))
    return pl.pallas_call(
        _combine_kernel,
        grid=(nt,),
        in_specs=[tok, pl.BlockSpec((tc, d), lambda i: (nt + i, 0)), tok,
                  pl.BlockSpec((1, 1, d), lambda i: (i // tiles_per_batch, 0, 0)),
                  pl.BlockSpec((tc, TOP_K), lambda i: (i, 0)),
                  pl.BlockSpec((1, d), lambda i: (0, 0))],
        out_specs=tok,
        out_shape=jax.ShapeDtypeStruct((n, d), F32),
        compiler_params=pltpu.CompilerParams(
            dimension_semantics=("arbitrary",), vmem_limit_bytes=VMEM_LIMIT),
        name="moe_combine",
    )(ys, ys, x2d, gt[:, None, :], prob, final_g[None, :])


def _moe_layout(eid, rank, counts, n_experts, tm, n_tiles):
    n = eid.shape[1]
    rows = n_tiles * tm
    cnt = counts[:n_experts, 0]
    tiles = (cnt + tm - 1) // tm
    tile_end = jnp.cumsum(tiles)
    offsets = (tile_end - tiles) * tm
    first_rank = jnp.cumsum(cnt) - cnt

    def per_expert(table, idx):
        out = jnp.zeros(idx.shape, table.dtype)
        for e in range(n_experts):
            out = jnp.where(idx == e, table[e], out)
        return out

    pos = (per_expert(offsets, eid) + rank).T.reshape(-1)
    order = jnp.argsort(pos).astype(I32)
    tile_ids = jnp.arange(n_tiles, dtype=I32)
    last_id = jnp.minimum(tile_ids, tile_end[-1] - 1)
    gid = jnp.sum((last_id[:, None] >= tile_end[None, :]).astype(I32), axis=1)
    p = jnp.arange(rows, dtype=I32)
    e_of_p = jnp.repeat(gid, tm)
    within = p - per_expert(offsets, e_of_p)
    valid = within < per_expert(cnt, e_of_p)
    flat = order[jnp.clip(per_expert(first_rank, e_of_p) + within, 0, TOP_K * n - 1)]
    tok = flat // TOP_K
    src = jnp.where(valid, tok, 0)
    pad_id = jnp.cumsum(jnp.logical_not(valid).astype(I32)) - 1
    dst = jnp.where(valid, (flat % TOP_K) * n + tok, TOP_K * n + pad_id)
    spare = rows
    src = jnp.concatenate([src, jnp.zeros((tm,), I32)]).reshape(n_tiles + 1, 1, tm)
    dst = jnp.concatenate([spare + jnp.arange(tm, dtype=I32), dst]).reshape(n_tiles + 1, 1, tm)
    tile_valid = (tile_ids < tile_end[-1]).astype(I32)
    return gid, tile_valid, src.astype(I32), dst.astype(I32), spare + 2 * tm, spare + tm


def kernel(x, c, ada_w, ada_b, norm_g, final_g, rwkv_mu, rwkv_w_rkv, rwkv_w_o, rwkv_w0, rwkv_w1, rwkv_w2, rwkv_a0, rwkv_a1, rwkv_a2, rwkv_g1, rwkv_g2, rwkv_k_k, rwkv_k_a, rwkv_r_k, rwkv_gn_w, rwkv_gn_b, lru_w_in, lru_conv_w, lru_conv_b, lru_w_gates, lru_b_gates, lru_lam, lru_w_out, ffn_w_gu, ffn_w_d, moe_w_router, moe_b_router, moe_w_gu, moe_w_d):
    b, t, d = x.shape
    n = b * t
    n_experts = moe_w_router.shape[-1]
    mod, (w_rkv_b,) = _ada_mod(c, ada_w, ada_b, [rwkv_w_rkv[0]])

    def mods(i):
        return [mod[i, :, q * d:(q + 1) * d] for q in range(6)]

    sh1, sc1, gt1, sh2, sc2, gt2 = mods(0)
    (r, lw, k2, v, kk, bv, g, bonus), (w_o_b, ffn_gu_b, ffn_d_b, lru_in_b, lru_gates_b, lru_out_b) = _rwkv_pre(
        x, sh1, sc1, norm_g[0, 0], rwkv_mu[0], w_rkv_b, rwkv_w1[0], rwkv_w2[0],
        rwkv_a1[0], rwkv_a2[0], rwkv_g1[0], rwkv_g2[0], rwkv_w0[0], rwkv_a0[0],
        rwkv_k_k[0], rwkv_k_a[0], rwkv_r_k[0], tm=min(256, t),
        later_weights=[rwkv_w_o[0], ffn_w_gu[0], ffn_w_d[0], lru_w_in[0], lru_w_gates[0], lru_w_out[0]])
    yg = _wkv_scan(r, lw, k2, v, kk, bv, g, bonus, rwkv_gn_w[0], rwkv_gn_b[0],
                   chunks_per_step=min(4, t // CHUNK))
    x, (moe_gu_b,) = _ffn_dense(yg, x, gt1, sh2, sc2, gt2, norm_g[0, 1], w_o_b, ffn_gu_b, ffn_d_b,
                                tm=min(256, t), later_weights=[moe_w_gu[0]])

    sh1, sc1, gt1, sh2, sc2, gt2 = mods(1)
    x, (moe_d_b,) = _lru_block(x, sh1, sc1, gt1, norm_g[1, 0], lru_in_b, lru_conv_w[0], lru_conv_b[0],
                               lru_gates_b, lru_b_gates[0], lru_lam[0], lru_out_b, tm=min(512, t),
                               later_weights=[moe_w_d[0]])

    x2d = x.reshape(n, d)
    tm_r = min(1024, t)
    h2, eid, rank, prob, counts = _router(x2d, sh2, sc2, norm_g[1, 1], moe_w_router[0],
                                          moe_b_router[0], t // tm_r, tm_r)
    tm_g = min(256, t)
    n_tiles = (TOP_K * n) // tm_g + n_experts
    gid, tile_valid, src, dst, out_rows, spare_row = _moe_layout(eid, rank, counts, n_experts, tm_g, n_tiles)
    ys = _moe_ffn(gid, tile_valid, src, dst, h2, moe_gu_b, moe_d_b, out_rows, spare_row, tm_g)
    tc = min(1024, t)
    out = _combine(ys, x2d, gt2, prob.T, final_g, t // tc, tc)
    return out.reshape(b, t, d)
```
